```python
import math
import jax, jax.numpy as jnp
from jax import lax
import numpy as np

D_MODEL = 1024
BATCH = 8
SEQ = 2048
DEPTH = 4
DEC_BATCH = 128
DEC_SEQ = 1
PAST_LEN = 2048
PAGE_SIZE = 128

N_AC = (DEPTH + 1) // 2
N_C = DEPTH // 2

H_B = 4
HD_B = 64
DV_B = 2 * HD_B
W_B = H_B * DV_B
QK_W = H_B * 2 * HD_B
ROT_DIM = HD_B // 4
ROPE_THETA = 500000.0
Q_BLOCK = 128
NEG_INF = -1e30

W_A = D_MODEL // 2
CONV_W = 31

H_C = 8
DK_C = D_MODEL // H_C
DV_C = D_MODEL // H_C
W_C = H_C * DV_C
CHUNK_C = 64

EPS = 1e-6

AC_IN = 2 * QK_W + 2 * W_B + 3 * W_A
AC_SPLITS = (QK_W, 2 * QK_W, 2 * QK_W + W_B, 2 * QK_W + 2 * W_B,
             2 * QK_W + 2 * W_B + W_A, 2 * QK_W + 2 * W_B + 2 * W_A)
C_IN = 2 * H_C * DK_C + 2 * W_C
C_SPLITS = (H_C * DK_C, 2 * H_C * DK_C, 2 * H_C * DK_C + W_C)

kernel_name = "hybrid_conv_diffattn_hgrn2_adaln_step"


def rms_norm(x, g):
    xf = x.astype(jnp.float32)
    y = xf * lax.rsqrt(jnp.mean(xf * xf, axis=-1, keepdims=True) + EPS)
    return (y * g.astype(jnp.float32)).astype(x.dtype)


def layer_norm(x, g, b):
    xf = x.astype(jnp.float32)
    xc = xf - jnp.mean(xf, axis=-1, keepdims=True)
    y = xc * lax.rsqrt(jnp.mean(xc * xc, axis=-1, keepdims=True) + EPS)
    return (y * g.astype(jnp.float32) + b.astype(jnp.float32)).astype(x.dtype)


def ada_modulate(x, c, norm_g, ada_w, ada_b):
    m = jnp.dot(jax.nn.silu(c), ada_w) + ada_b
    shift, scale, gate = jnp.split(m[:, None, :], 3, axis=-1)
    h = (rms_norm(x, norm_g) * (1.0 + scale) + shift).astype(x.dtype)
    return h, gate


def partial_rope(x, pos):
    half = ROT_DIM // 2
    inv_freq = ROPE_THETA ** (-jnp.arange(half, dtype=jnp.float32) / half)
    ang = pos.astype(jnp.float32)[:, None] * inv_freq[None, :]
    shape = (1, ang.shape[0]) + (1,) * (x.ndim - 3) + (half,)
    cos = jnp.cos(ang).reshape(shape)
    sin = jnp.sin(ang).reshape(shape)
    xr = x[..., :ROT_DIM].astype(jnp.float32)
    x1, x2 = xr[..., :half], xr[..., half:]
    rot = jnp.concatenate([x1 * cos - x2 * sin, x2 * cos + x1 * sin], axis=-1).astype(x.dtype)
    return jnp.concatenate([rot, x[..., ROT_DIM:]], axis=-1)


def diff_attention(q, k, v, q_pos, k_pos, lam):
    s = jnp.einsum("bqhcd,bkhcd->bhcqk", q, k).astype(jnp.float32) * (HD_B ** -0.5)
    s = jnp.where(k_pos[None, :] <= q_pos[:, None], s, NEG_INF)
    p = jax.nn.softmax(s, axis=-1)
    a = p[:, :, 0] - lam * p[:, :, 1]
    return jnp.einsum("bhqk,bkhv->bqhv", a.astype(v.dtype), v)


def blocked_causal_diff_attention(q, k, v, lam):
    B, T = q.shape[:2]
    nb = T // Q_BLOCK
    q_blocks = jnp.moveaxis(q.reshape(B, nb, Q_BLOCK, H_B, 2, HD_B), 1, 0)
    k_pos = jnp.arange(T)

    def one_block(args):
        q_blk, start = args
        return diff_attention(q_blk, k, v, start + jnp.arange(Q_BLOCK), k_pos, lam)

    out = lax.map(one_block, (q_blocks, jnp.arange(nb) * Q_BLOCK))
    return jnp.moveaxis(out, 0, 1).reshape(B, T, H_B, DV_B)


def conv_attn_layer(x, c, pos, conv_buf, past_k, past_v, lam_init,
                    norm_g, ada_w, ada_b, w_in, w_out, conv_w, conv_b, ln_g, ln_b,
                    lam_q1, lam_k1, lam_q2, lam_k2, subln_g):
    B, T, _ = x.shape
    f32 = jnp.float32
    h, gate = ada_modulate(x, c, norm_g, ada_w, ada_b)
    z = jnp.dot(h, w_in)
    q, k, v, g_b, a_val, a_gate, g_a = jnp.split(z, AC_SPLITS, axis=-1)
    q = partial_rope(q.reshape(B, T, H_B, 2, HD_B), pos)
    k = partial_rope(k.reshape(B, T, H_B, 2, HD_B), pos)
    v = v.reshape(B, T, H_B, DV_B)
    lam = (jnp.exp(jnp.sum(lam_q1.astype(f32) * lam_k1.astype(f32)))
           - jnp.exp(jnp.sum(lam_q2.astype(f32) * lam_k2.astype(f32))) + lam_init)
    if past_k is None:
        o_b = blocked_causal_diff_attention(q, k, v, lam)
    else:
        k_all = jnp.concatenate([past_k.astype(k.dtype), k], axis=1)
        v_all = jnp.concatenate([past_v.astype(v.dtype), v], axis=1)
        o_b = diff_attention(q, k_all, v_all, pos, jnp.arange(k_all.shape[1]), lam)
    o_b = rms_norm(o_b, subln_g) * (1.0 - lam_init)
    o_b = o_b.reshape(B, T, W_B) * jax.nn.silu(g_b)
    u = a_val * jax.nn.sigmoid(a_gate)
    full = jnp.concatenate([conv_buf.astype(u.dtype), u], axis=1)
    y = lax.conv_general_dilated(full, conv_w[:, None, :].astype(u.dtype), (1,), "VALID",
                                 dimension_numbers=("NWC", "WIO", "NWC"),
                                 feature_group_count=W_A) + conv_b
    y = jax.nn.silu(layer_norm(y, ln_g, ln_b)) * jax.nn.silu(g_a)
    out = jnp.dot(jnp.concatenate([o_b.astype(y.dtype), y], axis=-1), w_out)
    x_new = (x + gate * out).astype(x.dtype)
    new_conv = full[:, full.shape[1] - (CONV_W - 1):]
    return x_new, k.reshape(B, T, H_B, 2 * HD_B), v, new_conv


def hgrn2_recurrence(q, k, log_f, v, s0):
    B, T, H, DK = q.shape
    L = CHUNK_C if T % CHUNK_C == 0 else T
    n = T // L
    causal = jnp.tril(jnp.ones((L, L), dtype=bool))

    def to_chunks(a):
        return jnp.swapaxes(a.reshape((B, n, L) + a.shape[2:]), 0, 1)

    def step(S, inp):
        qc, kc, gc, vc = inp
        b = jnp.cumsum(gc, axis=1)
        o_inter = jnp.einsum("blhk,bhkv->blhv", qc * jnp.exp(b), S)
        diff = b[:, :, None] - b[:, None, :]
        dec = jnp.exp(jnp.where(causal[None, :, :, None, None], diff, -jnp.inf))
        A = jnp.einsum("bthk,bshk,btshk->bhts", qc, kc, dec)
        o_intra = jnp.einsum("bhts,bshv->bthv", A, vc)
        bL = b[:, -1]
        S_new = (jnp.exp(bL)[..., None] * S
                 + jnp.einsum("bshk,bshv->bhkv", kc * jnp.exp(bL[:, None] - b), vc))
        return S_new, o_inter + o_intra

    S, o = lax.scan(step, s0, (to_chunks(q), to_chunks(k), to_chunks(log_f), to_chunks(v)))
    return jnp.swapaxes(o, 0, 1).reshape(B, T, H, v.shape[-1]), S


def hgrn2_layer(x, c, state, lb, norm_g, ada_w, ada_b, w_in, w_out, gn_g):
    B, T, _ = x.shape
    f32 = jnp.float32
    h, gate = ada_modulate(x, c, norm_g, ada_w, ada_b)
    z = jnp.dot(h, w_in)
    q, fz, i, g = jnp.split(z, C_SPLITS, axis=-1)
    f = lb + (1.0 - lb) * jax.nn.sigmoid(fz.astype(f32))
    log_f = jnp.log(f).reshape(B, T, H_C, DK_C)
    k = (1.0 - f).reshape(B, T, H_C, DK_C)
    q = jax.nn.silu(q.astype(f32)).reshape(B, T, H_C, DK_C)
    i = i.astype(f32).reshape(B, T, H_C, DV_C)
    o, s_new = hgrn2_recurrence(q, k, log_f, i, state.astype(f32))
    o = rms_norm(o, gn_g).reshape(B, T, W_C).astype(x.dtype) * jax.nn.silu(g)
    out = jnp.dot(o, w_out)
    return (x + gate * out).astype(x.dtype), s_new.astype(state.dtype)


def setup_inputs(seed: int = 0) -> dict:
    key = jax.random.key(seed)
    ks = jax.random.split(key, 32)
    f32 = jnp.float32
    n_pages = PAST_LEN // PAGE_SIZE
    n_used = DEC_BATCH * n_pages
    n_pool = n_used + max(1, n_used // 4)

    def nrm(k, shape, s):
        return s * jax.random.normal(k, shape, f32)

    page_table = jax.random.permutation(ks[0], n_pool)[:n_used].reshape(DEC_BATCH, n_pages).astype(jnp.int32)
    return {
        "x_prompt": nrm(ks[1], (BATCH, SEQ, D_MODEL), 1.0),
        "x_sample": nrm(ks[2], (DEC_BATCH, DEC_SEQ, D_MODEL), 1.0),
        "c_prompt": nrm(ks[3], (BATCH, D_MODEL), 1.0),
        "c_sample": nrm(ks[4], (DEC_BATCH, D_MODEL), 1.0),
        "cache_k": nrm(ks[5], (N_AC, n_pool, PAGE_SIZE, H_B, 2 * HD_B), 1.0),
        "cache_v": nrm(ks[6], (N_AC, n_pool, PAGE_SIZE, H_B, 2 * HD_B), 1.0),
        "page_table": page_table,
        "state_conv": nrm(ks[7], (N_AC, DEC_BATCH, CONV_W - 1, W_A), 0.5),
        "state_hgrn": nrm(ks[8], (N_C, DEC_BATCH, H_C, DK_C, DV_C), 0.5),
        "norm_g": 1.0 + nrm(ks[9], (DEPTH, D_MODEL), 0.02),
        "ada_w": nrm(ks[10], (DEPTH, D_MODEL, 3 * D_MODEL), 0.5 * D_MODEL ** -0.5),
        "ada_b": nrm(ks[11], (DEPTH, 3 * D_MODEL), 0.02),
        "w_in_ac": nrm(ks[12], (N_AC, D_MODEL, AC_IN), D_MODEL ** -0.5),
        "w_out_ac": nrm(ks[13], (N_AC, W_B + W_A, D_MODEL), (W_B + W_A) ** -0.5),
        "conv_w": nrm(ks[14], (N_AC, CONV_W, W_A), CONV_W ** -0.5),
        "conv_b": nrm(ks[15], (N_AC, W_A), 0.02),
        "ln_g": 1.0 + nrm(ks[16], (N_AC, W_A), 0.02),
        "ln_b": nrm(ks[17], (N_AC, W_A), 0.02),
        "lam_q1": nrm(ks[18], (N_AC, HD_B), 0.1),
        "lam_k1": nrm(ks[19], (N_AC, HD_B), 0.1),
        "lam_q2": nrm(ks[20], (N_AC, HD_B), 0.1),
        "lam_k2": nrm(ks[21], (N_AC, HD_B), 0.1),
        "subln_g": 1.0 + nrm(ks[22], (N_AC, DV_B), 0.02),
        "w_in_c": nrm(ks[23], (N_C, D_MODEL, C_IN), D_MODEL ** -0.5),
        "w_out_c": nrm(ks[24], (N_C, W_C, D_MODEL), W_C ** -0.5),
        "gn_g": 1.0 + nrm(ks[25], (N_C, DV_C), 0.02),
        "lb_logits": nrm(ks[26], (DEPTH, H_C * DK_C), 1.0),
        "final_g": 1.0 + nrm(ks[27], (D_MODEL,), 0.02),
    }


def reference(x_prompt, x_sample, c_prompt, c_sample, cache_k, cache_v, page_table,
              state_conv, state_hgrn, norm_g, ada_w, ada_b, w_in_ac, w_out_ac, conv_w, conv_b,
              ln_g, ln_b, lam_q1, lam_k1, lam_q2, lam_k2, subln_g, w_in_c, w_out_c, gn_g,
              lb_logits, final_g):
    Bp, Tp = x_prompt.shape[:2]
    Bs, Ts = x_sample.shape[:2]
    n_past = page_table.shape[1] * PAGE_SIZE
    pos_p = jnp.arange(Tp)
    pos_s = n_past + jnp.arange(Ts)
    lb_sm = jax.nn.softmax(lb_logits.astype(jnp.float32), axis=0)
    lb_all = jnp.cumsum(lb_sm, axis=0) - lb_sm[0]

    hp, hs = x_prompt, x_sample
    kp_l, vp_l, cp_l, sp_l = [], [], [], []
    ks_l, vs_l, cs_l, ss_l = [], [], [], []
    for l in range(DEPTH):
        if l % 2 == 0:
            a = l // 2
            lam_init = 0.8 - 0.6 * math.exp(-0.3 * l)
            prm = (norm_g[l], ada_w[l], ada_b[l], w_in_ac[a], w_out_ac[a], conv_w[a], conv_b[a],
                   ln_g[a], ln_b[a], lam_q1[a], lam_k1[a], lam_q2[a], lam_k2[a], subln_g[a])
            zero_buf = jnp.zeros((Bp, CONV_W - 1, W_A), x_prompt.dtype)
            hp, k_new, v_new, c_new = conv_attn_layer(hp, c_prompt, pos_p, zero_buf, None, None,
                                                      lam_init, *prm)
            kp_l.append(k_new); vp_l.append(v_new); cp_l.append(c_new)
            past_k = cache_k[a][page_table].reshape(Bs, n_past, H_B, 2, HD_B)
            past_v = cache_v[a][page_table].reshape(Bs, n_past, H_B, DV_B)
            hs, k_new, v_new, c_new = conv_attn_layer(hs, c_sample, pos_s, state_conv[a], past_k, past_v,
                                                      lam_init, *prm)
            ks_l.append(k_new); vs_l.append(v_new); cs_l.append(c_new)
        else:
            ci = l // 2
            prm = (norm_g[l], ada_w[l], ada_b[l], w_in_c[ci], w_out_c[ci], gn_g[ci])
            zero_state = jnp.zeros((Bp, H_C, DK_C, DV_C), state_hgrn.dtype)
            hp, s_new = hgrn2_layer(hp, c_prompt, zero_state, lb_all[l], *prm)
            sp_l.append(s_new)
            hs, s_new = hgrn2_layer(hs, c_sample, state_hgrn[ci], lb_all[l], *prm)
            ss_l.append(s_new)

    y_prompt = rms_norm(hp, final_g)
    y_sample = rms_norm(hs, final_g)
    return (y_prompt, y_sample,
            jnp.stack(kp_l), jnp.stack(vp_l), jnp.stack(cp_l), jnp.stack(sp_l),
            jnp.stack(ks_l), jnp.stack(vs_l), jnp.stack(cs_l), jnp.stack(ss_l))
```

```python
import functools
import math

import numpy as np
import jax
import jax.numpy as jnp
from jax import lax
from jax.experimental import pallas as pl
from jax.experimental.pallas import tpu as pltpu

F32 = jnp.float32
BF16 = jnp.bfloat16

D_MODEL = 1024
DEPTH = 4
PAGE_SIZE = 128
N_AC = (DEPTH + 1) // 2
N_C = DEPTH // 2

H_B = 4
HD_B = 64
DV_B = 2 * HD_B
W_B = H_B * DV_B
QK_W = H_B * 2 * HD_B
ROT_DIM = HD_B // 4
ROPE_THETA = 500000.0
NEG_INF = -1e30

W_A = D_MODEL // 2
CONV_W = 31

H_C = 8
DK_C = D_MODEL // H_C
DV_C = D_MODEL // H_C
W_C = H_C * DV_C

EPS = 1e-6

AC_IN = 2 * QK_W + 2 * W_B + 3 * W_A
C_IN = 2 * H_C * DK_C + 2 * W_C

LANES = 128
SUBLANES = 8
VMEM_LIMIT_BYTES = 56 * 1024 * 1024

ROW_TILE = 512
ATTN_TILE = 256
CONV_TILE = 512
CONV_HALO = 32
HGRN_CHUNK = 128
HGRN_DIAG = SUBLANES
SAMPLE_STATE_TILE = 8


def _cparams(*sem):
    return pltpu.CompilerParams(dimension_semantics=sem, vmem_limit_bytes=VMEM_LIMIT_BYTES)


def _silu(x):
    return x * jax.nn.sigmoid(x)


def _dot(a, b):
    return jnp.dot(a, b, preferred_element_type=F32)


def _dot_nt(a, b):
    return lax.dot_general(a, b, (((1,), (1,)), ((), ())), preferred_element_type=F32)


def _dot_tn(a, b):
    return lax.dot_general(a, b, (((0,), (0,)), ((), ())), preferred_element_type=F32)


def _rms(x, g):
    return x * lax.rsqrt(jnp.mean(x * x, axis=-1, keepdims=True) + EPS) * g


def _ada_kernel(c_ref, w_ref, b_ref, o_ref):
    s = _silu(c_ref[...]).astype(BF16)
    o_ref[...] = _dot(s, w_ref[...].astype(BF16)) + b_ref[...]


def _ada_modulation(c_all, ada_w, ada_b):
    rows = c_all.shape[0]
    return pl.pallas_call(
        _ada_kernel,
        grid=(DEPTH, 3),
        in_specs=[
            pl.BlockSpec((rows, D_MODEL), lambda l, j: (0, 0)),
            pl.BlockSpec((None, D_MODEL, D_MODEL), lambda l, j: (l, 0, j)),
            pl.BlockSpec((None, 1, D_MODEL), lambda l, j: (l, 0, j)),
        ],
        out_specs=pl.BlockSpec((None, rows, D_MODEL), lambda l, j: (l, 0, j)),
        out_shape=jax.ShapeDtypeStruct((DEPTH, rows, 3 * D_MODEL), F32),
        compiler_params=_cparams("arbitrary", "arbitrary"),
        name="ada_modulation",
    )(c_all, ada_w, ada_b.reshape(DEPTH, 1, 3 * D_MODEL))


def _modulated(x_ref, ng_ref, sc_ref, sh_ref):
    h = _rms(x_ref[...], ng_ref[...])
    return (h * (1.0 + sc_ref[...]) + sh_ref[...]).astype(BF16)


def _inproj_ac_kernel(x_ref, ng_ref, sc_ref, sh_ref, w_ref, cos_ref, sa_ref, sb_ref,
                      q_ref, k_ref, kb_ref, v_ref, vb_ref, gb_ref, u_ref, ga_ref, h_scr):
    h_scr[...] = _modulated(x_ref, ng_ref, sc_ref, sh_ref)

    def proj(s):
        return _dot(h_scr[...], w_ref[:, s * QK_W:(s + 1) * QK_W])

    cos, sa, sb = cos_ref[...], sa_ref[...], sb_ref[...]

    def rope(z):
        return (z * cos + pltpu.roll(z, LANES - ROT_DIM // 2, 1) * sa
                + pltpu.roll(z, ROT_DIM // 2, 1) * sb)

    zq = proj(0)
    for g in range(H_B):
        sl = slice(g * LANES, (g + 1) * LANES)
        q_ref[:, sl] = (rope(zq[:, sl]) * (HD_B ** -0.5)).astype(BF16)
    zk = proj(1)
    for g in range(H_B):
        sl = slice(g * LANES, (g + 1) * LANES)
        r = rope(zk[:, sl])
        k_ref[:, sl] = r
        kb_ref[:, sl] = r.astype(BF16)
    zv = proj(2)
    v_ref[...] = zv
    vb_ref[...] = zv.astype(BF16)
    gb_ref[...] = _silu(proj(3)).astype(BF16)
    a_val = proj(4)
    u_ref[...] = a_val * jax.nn.sigmoid(proj(5))
    ga_ref[...] = _silu(proj(6)).astype(BF16)


def _mod_specs(rows_per_mod, tm, tiles_per_mod):
    r = rows_per_mod
    return pl.BlockSpec((None, r, D_MODEL), lambda i: (i // tiles_per_mod, 0, 0))


def _inproj_ac(x, norm_g, scale, shift, w, cos_t, sa_t, sb_t, tm, tiles_per_mod):
    m = x.shape[0]
    tab_tiles = cos_t.shape[0] // tm
    mod_spec = _mod_specs(scale.shape[1], tm, tiles_per_mod)
    row512 = pl.BlockSpec((tm, QK_W), lambda i: (i, 0))
    tab_spec = pl.BlockSpec((tm, LANES), lambda i: (i % tab_tiles, 0))
    out_shape = [
        jax.ShapeDtypeStruct((m, QK_W), BF16),
        jax.ShapeDtypeStruct((m, QK_W), F32),
        jax.ShapeDtypeStruct((m, QK_W), BF16),
        jax.ShapeDtypeStruct((m, W_B), F32),
        jax.ShapeDtypeStruct((m, W_B), BF16),
        jax.ShapeDtypeStruct((m, W_B), BF16),
        jax.ShapeDtypeStruct((m, W_A), F32),
        jax.ShapeDtypeStruct((m, W_A), BF16),
    ]
    return pl.pallas_call(
        _inproj_ac_kernel,
        grid=(m // tm,),
        in_specs=[
            pl.BlockSpec((tm, D_MODEL), lambda i: (i, 0)),
            pl.BlockSpec((1, D_MODEL), lambda i: (0, 0)),
            mod_spec, mod_spec,
            pl.BlockSpec((D_MODEL, AC_IN), lambda i: (0, 0)),
            tab_spec, tab_spec, tab_spec,
        ],
        out_specs=[row512] * 8,
        out_shape=out_shape,
        scratch_shapes=[pltpu.VMEM((tm, D_MODEL), BF16)],
        compiler_params=_cparams("arbitrary"),
        name="inproj_conv_attn",
    )(x, norm_g.reshape(1, D_MODEL), scale, shift, w, cos_t, sa_t, sb_t)


def _inproj_c_kernel(x_ref, ng_ref, sc_ref, sh_ref, w_ref, lb_ref, q_ref, lg_ref, i_ref, sg_ref, h_scr):
    h_scr[...] = _modulated(x_ref, ng_ref, sc_ref, sh_ref)

    def proj(s):
        return _dot(h_scr[...], w_ref[:, s * W_C:(s + 1) * W_C])

    q_ref[...] = _silu(proj(0))
    lb = lb_ref[...]
    lg_ref[...] = jnp.log(lb + (1.0 - lb) * jax.nn.sigmoid(proj(1)))
    i_ref[...] = proj(2)
    sg_ref[...] = _silu(proj(3)).astype(BF16)


def _inproj_c(x, norm_g, scale, shift, w, lb, tm, tiles_per_mod):
    m = x.shape[0]
    mod_spec = _mod_specs(scale.shape[1], tm, tiles_per_mod)
    row = pl.BlockSpec((tm, W_C), lambda i: (i, 0))
    out_shape = [
        jax.ShapeDtypeStruct((m, W_C), F32),
        jax.ShapeDtypeStruct((m, W_C), F32),
        jax.ShapeDtypeStruct((m, W_C), F32),
        jax.ShapeDtypeStruct((m, W_C), BF16),
    ]
    return pl.pallas_call(
        _inproj_c_kernel,
        grid=(m // tm,),
        in_specs=[
            pl.BlockSpec((tm, D_MODEL), lambda i: (i, 0)),
            pl.BlockSpec((1, D_MODEL), lambda i: (0, 0)),
            mod_spec, mod_spec,
            pl.BlockSpec((D_MODEL, C_IN), lambda i: (0, 0)),
            pl.BlockSpec((1, W_C), lambda i: (0, 0)),
        ],
        out_specs=[row] * 4,
        out_shape=out_shape,
        scratch_shapes=[pltpu.VMEM((tm, D_MODEL), BF16)],
        compiler_params=_cparams("arbitrary"),
        name="inproj_hgrn",
    )(x, norm_g.reshape(1, D_MODEL), scale, shift, w, lb.reshape(1, W_C))


def _outproj_kernel(*refs, n_act, final_norm):
    acts = refs[:n_act]
    w_ref, x_ref, gate_ref = refs[n_act:n_act + 3]
    o_ref = refs[-1]
    acc = None
    lo = 0
    for a_ref in acts:
        width = a_ref.shape[-1]
        part = _dot(a_ref[...], w_ref[lo:lo + width, :])
        acc = part if acc is None else acc + part
        lo += width
    y = x_ref[...] + gate_ref[...] * acc
    if final_norm:
        y = _rms(y, refs[n_act + 3][...])
    o_ref[...] = y


def _outproj(acts, w, x, gate, tm, tiles_per_mod, final_g=None):
    m = x.shape[0]
    mod_spec = _mod_specs(gate.shape[1], tm, tiles_per_mod)
    in_specs = [pl.BlockSpec((tm, a.shape[1]), lambda i: (i, 0)) for a in acts]
    in_specs += [
        pl.BlockSpec((D_MODEL, D_MODEL), lambda i: (0, 0)),
        pl.BlockSpec((tm, D_MODEL), lambda i: (i, 0)),
        mod_spec,
    ]
    args = list(acts) + [w, x, gate]
    if final_g is not None:
        in_specs.append(pl.BlockSpec((1, D_MODEL), lambda i: (0, 0)))
        args.append(final_g.reshape(1, D_MODEL))
    return pl.pallas_call(
        functools.partial(_outproj_kernel, n_act=len(acts), final_norm=final_g is not None),
        grid=(m // tm,),
        in_specs=in_specs,
        out_specs=pl.BlockSpec((tm, D_MODEL), lambda i: (i, 0)),
        out_shape=jax.ShapeDtypeStruct((m, D_MODEL), F32),
        compiler_params=_cparams("arbitrary"),
        name="outproj",
    )(*args)


def _attn_finish(acc, l, lam, gb, subln_g, lam_init, rows):
    o = acc[:rows] / l[:rows] - lam * (acc[rows:] / l[rows:])
    return _rms(o, subln_g) * (1.0 - lam_init) * gb


def _attn_kernel(lam_ref, sg_ref, q_ref, k_ref, v_ref, gb_ref, o_ref, *, tile, lam_init):
    qi = pl.program_id(2)
    q = q_ref[...]
    lane = lax.broadcasted_iota(jnp.int32, q.shape, 1)
    zero = jnp.zeros_like(q)
    q2 = jnp.concatenate([jnp.where(lane < HD_B, q, zero), jnp.where(lane >= HD_B, q, zero)], axis=0)

    def update(carry, kt, vt, mask):
        m, l, acc = carry
        s = _dot_nt(q2, kt)
        if mask is not None:
            s = jnp.where(mask, s, NEG_INF)
        m_new = jnp.maximum(m, jnp.max(s, axis=-1, keepdims=True))
        alpha = jnp.exp(m - m_new)
        p = jnp.exp(s - m_new)
        l = alpha * l + jnp.sum(p, axis=-1, keepdims=True)
        acc = alpha * acc + _dot(p.astype(BF16), vt)
        return m_new, l, acc

    def body(j, carry):
        start = pl.multiple_of(j * tile, tile)
        return update(carry, k_ref[pl.ds(start, tile), :], v_ref[pl.ds(start, tile), :], None)

    init = (jnp.full((2 * tile, 1), NEG_INF, F32), jnp.zeros((2 * tile, 1), F32),
            jnp.zeros((2 * tile, DV_B), F32))
    carry = lax.fori_loop(0, qi, body, init)
    row = lax.broadcasted_iota(jnp.int32, (2 * tile, tile), 0)
    col = lax.broadcasted_iota(jnp.int32, (2 * tile, tile), 1)
    causal = col <= jnp.where(row >= tile, row - tile, row)
    start = pl.multiple_of(qi * tile, tile)
    _, l, acc = update(carry, k_ref[pl.ds(start, tile), :], v_ref[pl.ds(start, tile), :], causal)
    out = _attn_finish(acc, l, lam_ref[...], gb_ref[...].astype(F32), sg_ref[...], lam_init, tile)
    o_ref[...] = out.astype(BF16)


def _prompt_attention(q, kb, vb, gb, lam, subln_g, lam_init, batch, seq):
    tile = ATTN_TILE
    q3, k3, v3, g3 = (a.reshape(batch, seq, QK_W) for a in (q, kb, vb, gb))
    qspec = pl.BlockSpec((None, tile, LANES), lambda b, h, i: (b, i, h))
    kvspec = pl.BlockSpec((None, seq, LANES), lambda b, h, i: (b, 0, h))
    vec = pl.BlockSpec((1, LANES), lambda b, h, i: (0, 0))
    out = pl.pallas_call(
        functools.partial(_attn_kernel, tile=tile, lam_init=lam_init),
        grid=(batch, H_B, seq // tile),
        in_specs=[vec, vec, qspec, kvspec, kvspec, qspec],
        out_specs=qspec,
        out_shape=jax.ShapeDtypeStruct((batch, seq, W_B), BF16),
        compiler_params=_cparams("arbitrary", "arbitrary", "arbitrary"),
        name="prompt_diff_attention",
    )(lam, subln_g.reshape(1, DV_B), q3, k3, v3, g3)
    return out.reshape(batch * seq, W_B)


def _conv_post(y, ln_g, ln_b, ga):
    yc = y - jnp.mean(y, axis=-1, keepdims=True)
    yn = yc * lax.rsqrt(jnp.mean(yc * yc, axis=-1, keepdims=True) + EPS) * ln_g + ln_b
    return _silu(yn) * ga


def _conv_kernel(u_ref, ga_ref, w_ref, cb_ref, lg_ref, lb_ref, y_ref, st_ref, ext_scr, *, tile, rows):
    t = pl.program_id(1)
    first = CONV_HALO - (CONV_W - 1)

    @pl.when(t == 0)
    def _():
        ext_scr[0:CONV_HALO, :] = jnp.zeros((CONV_HALO, W_A), F32)

    @pl.when(t > 0)
    def _():
        ext_scr[0:CONV_HALO, :] = ext_scr[tile:tile + CONV_HALO, :]

    ext_scr[CONV_HALO:CONV_HALO + tile, :] = u_ref[...]
    for r0 in range(0, tile, rows):
        acc = jnp.zeros((rows, W_A), F32) + cb_ref[...]
        for j in range(CONV_W):
            acc = acc + ext_scr[first + r0 + j:first + r0 + j + rows, :] * w_ref[j:j + 1, :]
        y_ref[r0:r0 + rows, :] = _conv_post(acc, lg_ref[...], lb_ref[...],
                                            ga_ref[r0:r0 + rows, :].astype(F32)).astype(BF16)

    @pl.when(t == pl.num_programs(1) - 1)
    def _():
        st_ref[...] = ext_scr[CONV_HALO + tile - (CONV_W - 1):CONV_HALO + tile, :]


def _prompt_conv(u, ga, conv_w, conv_b, ln_g, ln_b, batch, seq):
    tile = CONV_TILE
    u3, g3 = u.reshape(batch, seq, W_A), ga.reshape(batch, seq, W_A)
    tspec = pl.BlockSpec((None, tile, W_A), lambda b, t: (b, t, 0))
    vec = pl.BlockSpec((1, W_A), lambda b, t: (0, 0))
    y, st = pl.pallas_call(
        functools.partial(_conv_kernel, tile=tile, rows=64),
        grid=(batch, seq // tile),
        in_specs=[tspec, tspec, pl.BlockSpec((CONV_W, W_A), lambda b, t: (0, 0)), vec, vec, vec],
        out_specs=[tspec, pl.BlockSpec((None, CONV_W - 1, W_A), lambda b, t: (b, 0, 0))],
        out_shape=[jax.ShapeDtypeStruct((batch, seq, W_A), BF16),
                   jax.ShapeDtypeStruct((batch, CONV_W - 1, W_A), F32)],
        scratch_shapes=[pltpu.VMEM((CONV_HALO + tile, W_A), F32)],
        compiler_params=_cparams("arbitrary", "arbitrary"),
        name="prompt_conv",
    )(u3, g3, conv_w, conv_b.reshape(1, W_A), ln_g.reshape(1, W_A), ln_b.reshape(1, W_A))
    return y.reshape(batch * seq, W_A), st


def _hgrn_tables(chunk):
    levels = int(math.log2(chunk // HGRN_DIAG))
    t = np.arange(chunk)[:, None]
    u = np.arange(chunk)[None, :]
    mats = [(u <= t)]
    masks = []
    for v in range(levels):
        m = HGRN_DIAG << v
        blk_t, pos_t = t // (2 * m), t % (2 * m)
        mid = blk_t * 2 * m + m - 1
        second = pos_t >= m
        mats.append(np.where(second, (u > mid) & (u <= t), (u > t) & (u <= mid)))
        masks.append((blk_t == u // (2 * m)) & second & ((u % (2 * m)) < m))
    return (jnp.asarray(np.concatenate(mats, axis=0), BF16),
            jnp.asarray(np.stack(masks), F32), levels)


def _split3(x):
    hi = x.astype(BF16)
    r = x - hi.astype(F32)
    mid = r.astype(BF16)
    lo = (r - mid.astype(F32)).astype(BF16)
    return hi, mid, lo


def _hgrn_diag(q, k, b, v, chunk):
    nb = chunk // HGRN_DIAG
    q3, k3, b3, v3 = (a.reshape(nb, HGRN_DIAG, LANES) for a in (q, k, b, v))
    sub = lax.broadcasted_iota(jnp.int32, (nb, HGRN_DIAG, 1), 1)
    out = jnp.zeros((nb, HGRN_DIAG, LANES), F32)
    for t in range(HGRN_DIAG):
        dec = jnp.exp(jnp.minimum(b3[:, t:t + 1, :] - b3, 0.0))
        a = jnp.sum(q3[:, t:t + 1, :] * k3 * dec, axis=2, keepdims=True)
        a = jnp.where(sub <= t, a, 0.0)
        o_t = jnp.sum(a * v3, axis=1, keepdims=True)
        out = jnp.where(sub == t, o_t, out)
    return out.reshape(chunk, LANES)


def _hgrn_kernel(q_ref, lg_ref, i_ref, sg_ref, mat_ref, msk_ref, gn_ref, o_ref, s_ref, st_scr,
                 *, chunk, levels):
    c = pl.program_id(1)

    @pl.when(c == 0)
    def _():
        st_scr[...] = jnp.zeros_like(st_scr)

    mats = mat_ref[...]
    for h in range(H_C):
        sl = slice(h * LANES, (h + 1) * LANES)
        g = lg_ref[:, sl]
        q = q_ref[:, sl]
        v = i_ref[:, sl]
        k = 1.0 - jnp.exp(g)
        e3 = _dot(mats, jnp.concatenate(_split3(g), axis=1))
        e = e3[:, :LANES] + e3[:, LANES:2 * LANES] + e3[:, 2 * LANES:]
        b = e[:chunk]
        vb = v.astype(BF16)
        a = jnp.zeros((chunk, chunk), F32)
        for lv in range(levels):
            ex = jnp.exp(e[(lv + 1) * chunk:(lv + 2) * chunk])
            a = a + msk_ref[lv] * _dot_nt((q * ex).astype(BF16), (k * ex).astype(BF16))
        st = st_scr[h]
        o = _dot(a.astype(BF16), vb)
        o = o + _dot_nt((q * jnp.exp(b)).astype(BF16), st.astype(BF16))
        o = o + _hgrn_diag(q, k, b, v, chunk)
        b_last = b[chunk - 1:chunk, :]
        kd = (k * jnp.exp(b_last - b)).astype(BF16)
        st_scr[h] = st * jnp.exp(b_last) + _dot_tn(vb, kd)
        o_ref[:, sl] = (_rms(o, gn_ref[...]) * sg_ref[:, sl].astype(F32)).astype(BF16)

    @pl.when(c == pl.num_programs(1) - 1)
    def _():
        for h in range(H_C):
            s_ref[h] = st_scr[h].T


def _prompt_hgrn(qs, lg, iv, sg, gn_g, batch, seq):
    chunk = HGRN_CHUNK
    mats, masks, levels = _hgrn_tables(chunk)
    a3 = [a.reshape(batch, seq, W_C) for a in (qs, lg, iv, sg)]
    tspec = pl.BlockSpec((None, chunk, W_C), lambda b, c: (b, c, 0))
    o, s = pl.pallas_call(
        functools.partial(_hgrn_kernel, chunk=chunk, levels=levels),
        grid=(batch, seq // chunk),
        in_specs=[tspec, tspec, tspec, tspec,
                  pl.BlockSpec(mats.shape, lambda b, c: (0, 0)),
                  pl.BlockSpec(masks.shape, lambda b, c: (0, 0, 0)),
                  pl.BlockSpec((1, DV_C), lambda b, c: (0, 0))],
        out_specs=[tspec, pl.BlockSpec((None, H_C, DK_C, DV_C), lambda b, c: (b, 0, 0, 0))],
        out_shape=[jax.ShapeDtypeStruct((batch, seq, W_C), BF16),
                   jax.ShapeDtypeStruct((batch, H_C, DK_C, DV_C), F32)],
        scratch_shapes=[pltpu.VMEM((H_C, DV_C, DK_C), F32)],
        compiler_params=_cparams("arbitrary", "arbitrary"),
        name="prompt_hgrn",
    )(*a3, mats, masks, gn_g.reshape(1, DV_C))
    return o.reshape(batch * seq, W_C), s


def _decode_attn_kernel(pt_ref, lam_ref, sg_ref, q_ref, kn_ref, vn_ref, gb_ref, *refs, n_pages, lam_init):
    k_pages = refs[:n_pages]
    v_pages = refs[n_pages:2 * n_pages]
    o_ref = refs[2 * n_pages]
    n_rows = 2 * H_B
    q = q_ref[...].astype(F32)
    row = lax.broadcasted_iota(jnp.int32, (n_rows, QK_W), 0)
    lane = lax.broadcasted_iota(jnp.int32, (n_rows, QK_W), 1)
    qm32 = jnp.where(lane // HD_B == row, jnp.broadcast_to(q, (n_rows, QK_W)), 0.0)
    qm = qm32.astype(BF16)
    s = jnp.concatenate([_dot_nt(qm, kp[...].astype(BF16)) for kp in k_pages], axis=1)
    s_new = jnp.sum(qm32 * kn_ref[...], axis=-1, keepdims=True)
    m = jnp.maximum(jnp.max(s, axis=-1, keepdims=True), s_new)
    p = jnp.exp(s - m)
    p_new = jnp.exp(s_new - m)
    l = jnp.sum(p, axis=-1, keepdims=True) + p_new
    comp = lax.broadcasted_iota(jnp.int32, (n_rows, 1), 0) % 2
    wgt = jnp.where(comp == 0, 1.0, -lam_ref[:, 0:1]) / l
    pw = (p * wgt).astype(BF16)
    o8 = (p_new * wgt) * vn_ref[...]
    for j, vp in enumerate(v_pages):
        o8 = o8 + _dot(pw[:, j * PAGE_SIZE:(j + 1) * PAGE_SIZE], vp[...].astype(BF16))
    o = jnp.sum(jnp.where(lane // DV_B == row // 2, o8, 0.0), axis=0, keepdims=True)
    for h in range(H_B):
        sl = slice(h * DV_B, (h + 1) * DV_B)
        y = _rms(o[:, sl], sg_ref[...]) * (1.0 - lam_init) * gb_ref[:, sl].astype(F32)
        o_ref[:, sl] = y.astype(BF16)


def _decode_attention(q, k_new, v_new, gb, cache_k4, cache_v4, layer, page_table, lam, subln_g, lam_init):
    n_seq, n_pages = page_table.shape
    row = lambda dt: pl.BlockSpec((None, 1, QK_W), lambda b, pt: (b, 0, 0))
    vec = pl.BlockSpec((1, LANES), lambda b, pt: (0, 0))

    def page_spec(j):
        return pl.BlockSpec((None, None, PAGE_SIZE, QK_W), lambda b, pt: (layer, pt[b, j], 0, 0))

    pages = [page_spec(j) for j in range(n_pages)]
    grid_spec = pltpu.PrefetchScalarGridSpec(
        num_scalar_prefetch=1,
        grid=(n_seq,),
        in_specs=[vec, vec, row(BF16), row(F32), row(F32), row(BF16)] + pages + pages,
        out_specs=pl.BlockSpec((None, 1, W_B), lambda b, pt: (b, 0, 0)),
    )
    r3 = lambda a: a.reshape(n_seq, 1, QK_W)
    out = pl.pallas_call(
        functools.partial(_decode_attn_kernel, n_pages=n_pages, lam_init=lam_init),
        grid_spec=grid_spec,
        out_shape=jax.ShapeDtypeStruct((n_seq, 1, W_B), BF16),
        compiler_params=_cparams("arbitrary"),
        name="decode_diff_attention",
    )(page_table, lam, subln_g.reshape(1, DV_B), r3(q), r3(k_new), r3(v_new), r3(gb),
      *([cache_k4] * n_pages), *([cache_v4] * n_pages))
    return out.reshape(n_seq, W_B)


def _decode_conv_kernel(st_ref, u_ref, ga_ref, w_ref, cb_ref, lg_ref, lb_ref, y_ref, ns_ref):
    st = st_ref[...]
    u = u_ref[...]
    w = w_ref[...]
    y = jnp.sum(st * w[None, :CONV_W - 1, :], axis=1) + u * w[CONV_W - 1:CONV_W, :] + cb_ref[...]
    y_ref[...] = _conv_post(y, lg_ref[...], lb_ref[...], ga_ref[...].astype(F32)).astype(BF16)
    ns_ref[:, 0:CONV_W - 2, :] = st[:, 1:CONV_W - 1, :]
    ns_ref[:, CONV_W - 2:CONV_W - 1, :] = u[:, None, :]


def _decode_conv(state_all, layer, u, ga, conv_w, conv_b, ln_g, ln_b):
    n = u.shape[0]
    tile = 32
    vec = pl.BlockSpec((1, W_A), lambda i: (0, 0))
    sspec = pl.BlockSpec((tile, CONV_W - 1, W_A), lambda i: (i, 0, 0))
    rspec = pl.BlockSpec((tile, W_A), lambda i: (i, 0))
    return pl.pallas_call(
        _decode_conv_kernel,
        grid=(n // tile,),
        in_specs=[pl.BlockSpec((None, tile, CONV_W - 1, W_A), lambda i: (layer, i, 0, 0)),
                  rspec, rspec, pl.BlockSpec((CONV_W, W_A), lambda i: (0, 0)), vec, vec, vec],
        out_specs=[rspec, sspec],
        out_shape=[jax.ShapeDtypeStruct((n, W_A), BF16),
                   jax.ShapeDtypeStruct((n, CONV_W - 1, W_A), F32)],
        compiler_params=_cparams("arbitrary"),
        name="decode_conv",
    )(state_all, u, ga, conv_w, conv_b.reshape(1, W_A), ln_g.reshape(1, W_A), ln_b.reshape(1, W_A))


def _decode_hgrn_kernel(q_ref, lg_ref, i_ref, sg_ref, gn_ref, s_ref, o_ref, ns_ref, *, n_seq):
    eye = (lax.broadcasted_iota(jnp.int32, (DK_C, LANES), 0)
           == lax.broadcasted_iota(jnp.int32, (DK_C, LANES), 1))

    def column(r):
        return jnp.sum(jnp.where(eye, jnp.broadcast_to(r, (DK_C, LANES)), 0.0), axis=1, keepdims=True)

    def body(n, carry):
        for h in range(H_C):
            sl = slice(h * LANES, (h + 1) * LANES)
            g = lg_ref[n, :, sl]
            f = jnp.exp(g)
            s_new = column(f) * s_ref[n, h] + column(1.0 - f) * i_ref[n, :, sl]
            ns_ref[n, h] = s_new
            o = jnp.sum(column(q_ref[n, :, sl]) * s_new, axis=0, keepdims=True)
            o_ref[n, :, sl] = (_rms(o, gn_ref[...]) * sg_ref[n, :, sl].astype(F32)).astype(BF16)
        return carry

    lax.fori_loop(0, n_seq, body, 0)


def _decode_hgrn(qs, lg, iv, sg, gn_g, state_all, layer):
    n = qs.shape[0]
    tile = SAMPLE_STATE_TILE
    rspec = pl.BlockSpec((tile, 1, W_C), lambda i: (i, 0, 0))
    sspec = pl.BlockSpec((tile, H_C, DK_C, DV_C), lambda i: (i, 0, 0, 0))
    sin_spec = pl.BlockSpec((None, tile, H_C, DK_C, DV_C), lambda i: (layer, i, 0, 0, 0))
    r3 = lambda a: a.reshape(n, 1, W_C)
    o, ns = pl.pallas_call(
        functools.partial(_decode_hgrn_kernel, n_seq=tile),
        grid=(n // tile,),
        in_specs=[rspec, rspec, rspec, rspec, pl.BlockSpec((1, DV_C), lambda i: (0, 0)), sin_spec],
        out_specs=[rspec, sspec],
        out_shape=[jax.ShapeDtypeStruct((n, 1, W_C), BF16),
                   jax.ShapeDtypeStruct((n, H_C, DK_C, DV_C), F32)],
        compiler_params=_cparams("arbitrary"),
        name="decode_hgrn",
    )(r3(qs), r3(lg), r3(iv), r3(sg), gn_g.reshape(1, DV_C), state_all)
    return o.reshape(n, W_C), ns


def _rope_tables(pos, rows):
    half = ROT_DIM // 2
    inv_freq = ROPE_THETA ** (-jnp.arange(half, dtype=F32) / half)
    ang = pos.astype(F32)[:, None] * inv_freq[None, :]
    cos, sin = jnp.cos(ang), jnp.sin(ang)
    n = pos.shape[0]
    ones = jnp.ones((n, HD_B - ROT_DIM), F32)
    zeros = jnp.zeros((n, HD_B - ROT_DIM), F32)
    z8 = jnp.zeros((n, half), F32)
    cos_t = jnp.concatenate([cos, cos, ones], axis=1)
    sa_t = jnp.concatenate([-sin, z8, zeros], axis=1)
    sb_t = jnp.concatenate([z8, sin, zeros], axis=1)
    out = []
    for t in (cos_t, sa_t, sb_t):
        t = jnp.concatenate([t, t], axis=1)
        out.append(jnp.broadcast_to(t, (rows, LANES)) if n == 1 else t)
    return out


def kernel(x_prompt, x_sample, c_prompt, c_sample, cache_k, cache_v, page_table, state_conv, state_hgrn,
           norm_g, ada_w, ada_b, w_in_ac, w_out_ac, conv_w, conv_b, ln_g, ln_b, lam_q1, lam_k1, lam_q2,
           lam_k2, subln_g, w_in_c, w_out_c, gn_g, lb_logits, final_g):
    bp, tp = x_prompt.shape[:2]
    bs, ts = x_sample.shape[:2]
    assert ts == 1
    n_pool = cache_k.shape[1]
    n_past = page_table.shape[1] * PAGE_SIZE
    mp = bp * tp

    lb_sm = jax.nn.softmax(lb_logits.astype(F32), axis=0)
    lb_all = jnp.cumsum(lb_sm, axis=0) - lb_sm[0]

    mod = _ada_modulation(jnp.concatenate([c_prompt, c_sample], axis=0), ada_w, ada_b)

    def mods(l):
        out = []
        for j in range(3):
            m = mod[l, :, j * D_MODEL:(j + 1) * D_MODEL]
            out.append((m[:bp].reshape(bp, 1, D_MODEL), m[bp:].reshape(1, bs, D_MODEL)))
        return out

    tabs_p = _rope_tables(jnp.arange(tp), tp)
    tabs_s = _rope_tables(jnp.full((1,), n_past), bs)
    ck4 = cache_k.reshape(N_AC, n_pool, PAGE_SIZE, QK_W)
    cv4 = cache_v.reshape(N_AC, n_pool, PAGE_SIZE, W_B)

    tiles_p = tp // ROW_TILE
    hp = x_prompt.reshape(mp, D_MODEL)
    hs = x_sample.reshape(bs, D_MODEL)
    kp_l, vp_l, cp_l, sp_l, ks_l, vs_l, cs_l, ss_l = ([] for _ in range(8))
    for l in range(DEPTH):
        (sh_p, sh_s), (sc_p, sc_s), (gt_p, gt_s) = mods(l)
        last = final_g if l == DEPTH - 1 else None
        if l % 2 == 0:
            a = l // 2
            lam_init = 0.8 - 0.6 * math.exp(-0.3 * l)
            lam = (jnp.exp(jnp.sum(lam_q1[a].astype(F32) * lam_k1[a].astype(F32)))
                   - jnp.exp(jnp.sum(lam_q2[a].astype(F32) * lam_k2[a].astype(F32))) + lam_init)
            lam = jnp.full((1, LANES), lam, F32)
            w_in = w_in_ac[a].astype(BF16)
            w_out = w_out_ac[a].astype(BF16)
            q, k, kb, v, vb, gb, u, ga = _inproj_ac(hp, norm_g[l], sc_p, sh_p, w_in, *tabs_p,
                                                    ROW_TILE, tiles_p)
            ob = _prompt_attention(q, kb, vb, gb, lam, subln_g[a], lam_init, bp, tp)
            y, cst = _prompt_conv(u, ga, conv_w[a], conv_b[a], ln_g[a], ln_b[a], bp, tp)
            hp = _outproj([ob, y], w_out, hp, gt_p, ROW_TILE, tiles_p, last)
            kp_l.append(k.reshape(bp, tp, H_B, 2 * HD_B))
            vp_l.append(v.reshape(bp, tp, H_B, DV_B))
            cp_l.append(cst)
            q, k, kb, v, vb, gb, u, ga = _inproj_ac(hs, norm_g[l], sc_s, sh_s, w_in, *tabs_s, bs, 1)
            ob = _decode_attention(q, k, v, gb, ck4, cv4, a, page_table, lam, subln_g[a], lam_init)
            y, cst = _decode_conv(state_conv, a, u, ga, conv_w[a], conv_b[a], ln_g[a], ln_b[a])
            hs = _outproj([ob, y], w_out, hs, gt_s, bs, 1, last)
            ks_l.append(k.reshape(bs, ts, H_B, 2 * HD_B))
            vs_l.append(v.reshape(bs, ts, H_B, DV_B))
            cs_l.append(cst)
        else:
            ci = l // 2
            w_in = w_in_c[ci].astype(BF16)
            w_out = w_out_c[ci].astype(BF16)
            qs, lg, iv, sg = _inproj_c(hp, norm_g[l], sc_p, sh_p, w_in, lb_all[l], ROW_TILE, tiles_p)
            o, st = _prompt_hgrn(qs, lg, iv, sg, gn_g[ci], bp, tp)
            hp = _outproj([o], w_out, hp, gt_p, ROW_TILE, tiles_p, last)
            sp_l.append(st)
            qs, lg, iv, sg = _inproj_c(hs, norm_g[l], sc_s, sh_s, w_in, lb_all[l], bs, 1)
            o, st = _decode_hgrn(qs, lg, iv, sg, gn_g[ci], state_hgrn, ci)
            hs = _outproj([o], w_out, hs, gt_s, bs, 1, last)
            ss_l.append(st)

    y_prompt = hp.reshape(bp, tp, D_MODEL)
    y_sample = hs.reshape(bs, ts, D_MODEL)
    return (y_prompt, y_sample,
            jnp.stack(kp_l), jnp.stack(vp_l), jnp.stack(cp_l), jnp.stack(sp_l),
            jnp.stack(ks_l), jnp.stack(vs_l), jnp.stack(cs_l), jnp.stack(ss_l))
```

```python
import functools
import math

import numpy as np
import jax
import jax.numpy as jnp
from jax import lax
from jax.experimental import pallas as pl
from jax.experimental.pallas import tpu as pltpu

F32 = jnp.float32
BF16 = jnp.bfloat16

D_MODEL = 1024
DEPTH = 4
PAGE_SIZE = 128
N_AC = (DEPTH + 1) // 2
N_C = DEPTH // 2

H_B = 4
HD_B = 64
DV_B = 2 * HD_B
W_B = H_B * DV_B
QK_W = H_B * 2 * HD_B
ROT_DIM = HD_B // 4
ROPE_THETA = 500000.0
NEG_INF = -1e30

W_A = D_MODEL // 2
CONV_W = 31

H_C = 8
DK_C = D_MODEL // H_C
DV_C = D_MODEL // H_C
W_C = H_C * DV_C

EPS = 1e-6

AC_IN = 2 * QK_W + 2 * W_B + 3 * W_A
C_IN = 2 * H_C * DK_C + 2 * W_C

LANES = 128
SUBLANES = 8
VMEM_LIMIT_BYTES = 56 * 1024 * 1024

ROW_TILE = 512
ATTN_TILE = 256
CONV_TILE = 512
CONV_HALO = 32
HGRN_CHUNK = 128
SAMPLE_STATE_TILE = 8


def _cparams(*sem):
    return pltpu.CompilerParams(dimension_semantics=sem, vmem_limit_bytes=VMEM_LIMIT_BYTES)


def _silu(x):
    return x * jax.nn.sigmoid(x)


def _dot(a, b):
    return jnp.dot(a, b, preferred_element_type=F32)


def _dot_nt(a, b):
    return lax.dot_general(a, b, (((1,), (1,)), ((), ())), preferred_element_type=F32)


def _dot_tn(a, b):
    return lax.dot_general(a, b, (((0,), (0,)), ((), ())), preferred_element_type=F32)


def _rms(x, g):
    return x * lax.rsqrt(jnp.mean(x * x, axis=-1, keepdims=True) + EPS) * g


def _ada_kernel(c_ref, w_ref, b_ref, o_ref):
    s = _silu(c_ref[...]).astype(BF16)
    o_ref[...] = _dot(s, w_ref[...].astype(BF16)) + b_ref[...]


def _ada_modulation(c_all, ada_w, ada_b):
    rows = c_all.shape[0]
    return pl.pallas_call(
        _ada_kernel,
        grid=(DEPTH, 3),
        in_specs=[
            pl.BlockSpec((rows, D_MODEL), lambda l, j: (0, 0)),
            pl.BlockSpec((None, D_MODEL, D_MODEL), lambda l, j: (l, 0, j)),
            pl.BlockSpec((None, 1, D_MODEL), lambda l, j: (l, 0, j)),
        ],
        out_specs=pl.BlockSpec((None, rows, D_MODEL), lambda l, j: (l, 0, j)),
        out_shape=jax.ShapeDtypeStruct((DEPTH, rows, 3 * D_MODEL), F32),
        compiler_params=_cparams("arbitrary", "arbitrary"),
        name="ada_modulation",
    )(c_all, ada_w, ada_b.reshape(DEPTH, 1, 3 * D_MODEL))


def _modulated(x_ref, ng_ref, sc_ref, sh_ref):
    h = _rms(x_ref[...], ng_ref[...])
    return (h * (1.0 + sc_ref[...]) + sh_ref[...]).astype(BF16)


def _inproj_ac_kernel(x_ref, ng_ref, sc_ref, sh_ref, w_ref, cos_ref, sa_ref, sb_ref,
                      q_ref, k_ref, kb_ref, v_ref, vb_ref, gb_ref, u_ref, ga_ref, h_scr):
    h_scr[...] = _modulated(x_ref, ng_ref, sc_ref, sh_ref)

    def proj(s):
        return _dot(h_scr[...], w_ref[:, s * QK_W:(s + 1) * QK_W])

    cos, sa, sb = cos_ref[...], sa_ref[...], sb_ref[...]

    def rope(z):
        return (z * cos + pltpu.roll(z, LANES - ROT_DIM // 2, 1) * sa
                + pltpu.roll(z, ROT_DIM // 2, 1) * sb)

    zq = proj(0)
    for g in range(H_B):
        sl = slice(g * LANES, (g + 1) * LANES)
        q_ref[:, sl] = (rope(zq[:, sl]) * (HD_B ** -0.5)).astype(BF16)
    tm = x_ref.shape[0]
    zk = proj(1)
    for g in range(H_B):
        sl = slice(g * LANES, (g + 1) * LANES)
        r = rope(zk[:, sl])
        k_ref[pl.ds(g, tm, stride=H_B), :] = r
        kb_ref[:, sl] = r.astype(BF16)
    zv = proj(2)
    for g in range(H_B):
        v_ref[pl.ds(g, tm, stride=H_B), :] = zv[:, g * LANES:(g + 1) * LANES]
    vb_ref[...] = zv.astype(BF16)
    gb_ref[...] = _silu(proj(3)).astype(BF16)
    a_val = proj(4)
    u_ref[...] = a_val * jax.nn.sigmoid(proj(5))
    ga_ref[...] = _silu(proj(6)).astype(BF16)


def _mod_specs(rows_per_mod, tm, tiles_per_mod):
    r = rows_per_mod
    return pl.BlockSpec((None, r, D_MODEL), lambda i: (i // tiles_per_mod, 0, 0))


def _inproj_ac(x, norm_g, scale, shift, w, cos_t, sa_t, sb_t, tm, tiles_per_mod):
    m = x.shape[0]
    tab_tiles = cos_t.shape[0] // tm
    mod_spec = _mod_specs(scale.shape[1], tm, tiles_per_mod)
    row512 = pl.BlockSpec((tm, QK_W), lambda i: (i, 0))
    heads = pl.BlockSpec((H_B * tm, LANES), lambda i: (i, 0))
    tab_spec = pl.BlockSpec((tm, LANES), lambda i: (i % tab_tiles, 0))
    out_shape = [
        jax.ShapeDtypeStruct((m, QK_W), BF16),
        jax.ShapeDtypeStruct((H_B * m, LANES), F32),
        jax.ShapeDtypeStruct((m, QK_W), BF16),
        jax.ShapeDtypeStruct((H_B * m, LANES), F32),
        jax.ShapeDtypeStruct((m, W_B), BF16),
        jax.ShapeDtypeStruct((m, W_B), BF16),
        jax.ShapeDtypeStruct((m, W_A), F32),
        jax.ShapeDtypeStruct((m, W_A), BF16),
    ]
    return pl.pallas_call(
        _inproj_ac_kernel,
        grid=(m // tm,),
        in_specs=[
            pl.BlockSpec((tm, D_MODEL), lambda i: (i, 0)),
            pl.BlockSpec((1, D_MODEL), lambda i: (0, 0)),
            mod_spec, mod_spec,
            pl.BlockSpec((D_MODEL, AC_IN), lambda i: (0, 0)),
            tab_spec, tab_spec, tab_spec,
        ],
        out_specs=[row512, heads, row512, heads, row512, row512, row512, row512],
        out_shape=out_shape,
        scratch_shapes=[pltpu.VMEM((tm, D_MODEL), BF16)],
        compiler_params=_cparams("arbitrary"),
        name="inproj_conv_attn",
    )(x, norm_g.reshape(1, D_MODEL), scale, shift, w, cos_t, sa_t, sb_t)


def _inproj_c_kernel(x_ref, ng_ref, sc_ref, sh_ref, w_ref, lb_ref, q_ref, lg_ref, i_ref, sg_ref, h_scr):
    h_scr[...] = _modulated(x_ref, ng_ref, sc_ref, sh_ref)

    def proj(s):
        return _dot(h_scr[...], w_ref[:, s * W_C:(s + 1) * W_C])

    q_ref[...] = _silu(proj(0))
    lb = lb_ref[...]
    lg_ref[...] = jnp.log(lb + (1.0 - lb) * jax.nn.sigmoid(proj(1)))
    i_ref[...] = proj(2)
    sg_ref[...] = _silu(proj(3)).astype(BF16)


def _inproj_c(x, norm_g, scale, shift, w, lb, tm, tiles_per_mod):
    m = x.shape[0]
    mod_spec = _mod_specs(scale.shape[1], tm, tiles_per_mod)
    row = pl.BlockSpec((tm, W_C), lambda i: (i, 0))
    out_shape = [
        jax.ShapeDtypeStruct((m, W_C), F32),
        jax.ShapeDtypeStruct((m, W_C), F32),
        jax.ShapeDtypeStruct((m, W_C), F32),
        jax.ShapeDtypeStruct((m, W_C), BF16),
    ]
    return pl.pallas_call(
        _inproj_c_kernel,
        grid=(m // tm,),
        in_specs=[
            pl.BlockSpec((tm, D_MODEL), lambda i: (i, 0)),
            pl.BlockSpec((1, D_MODEL), lambda i: (0, 0)),
            mod_spec, mod_spec,
            pl.BlockSpec((D_MODEL, C_IN), lambda i: (0, 0)),
            pl.BlockSpec((1, W_C), lambda i: (0, 0)),
        ],
        out_specs=[row] * 4,
        out_shape=out_shape,
        scratch_shapes=[pltpu.VMEM((tm, D_MODEL), BF16)],
        compiler_params=_cparams("arbitrary"),
        name="inproj_hgrn",
    )(x, norm_g.reshape(1, D_MODEL), scale, shift, w, lb.reshape(1, W_C))


def _outproj_kernel(*refs, n_act, final_norm):
    acts = refs[:n_act]
    w_ref, x_ref, gate_ref = refs[n_act:n_act + 3]
    o_ref = refs[-1]
    acc = None
    lo = 0
    for a_ref in acts:
        width = a_ref.shape[-1]
        part = _dot(a_ref[...], w_ref[lo:lo + width, :])
        acc = part if acc is None else acc + part
        lo += width
    y = x_ref[...] + gate_ref[...] * acc
    if final_norm:
        y = _rms(y, refs[n_act + 3][...])
    o_ref[...] = y


def _outproj(acts, w, x, gate, tm, tiles_per_mod, final_g=None):
    m = x.shape[0]
    mod_spec = _mod_specs(gate.shape[1], tm, tiles_per_mod)
    in_specs = [pl.BlockSpec((tm, a.shape[1]), lambda i: (i, 0)) for a in acts]
    in_specs += [
        pl.BlockSpec((D_MODEL, D_MODEL), lambda i: (0, 0)),
        pl.BlockSpec((tm, D_MODEL), lambda i: (i, 0)),
        mod_spec,
    ]
    args = list(acts) + [w, x, gate]
    if final_g is not None:
        in_specs.append(pl.BlockSpec((1, D_MODEL), lambda i: (0, 0)))
        args.append(final_g.reshape(1, D_MODEL))
    return pl.pallas_call(
        functools.partial(_outproj_kernel, n_act=len(acts), final_norm=final_g is not None),
        grid=(m // tm,),
        in_specs=in_specs,
        out_specs=pl.BlockSpec((tm, D_MODEL), lambda i: (i, 0)),
        out_shape=jax.ShapeDtypeStruct((m, D_MODEL), F32),
        compiler_params=_cparams("arbitrary"),
        name="outproj",
    )(*args)


def _attn_kernel(lam_ref, sg_ref, q_ref, k_ref, v_ref, gb_ref, o_ref, *, tile, lam_init):
    qi = pl.program_id(2)
    q = q_ref[...]
    lane = lax.broadcasted_iota(jnp.int32, q.shape, 1)
    zero = jnp.zeros_like(q)
    q2 = jnp.concatenate([jnp.where(lane < HD_B, q, zero), jnp.where(lane >= HD_B, q, zero)], axis=0)

    def update(carry, kt, vt, mask):
        m, l, acc = carry
        s = _dot_nt(kt, q2)
        if mask is not None:
            s = jnp.where(mask, s, NEG_INF)
        m_new = jnp.maximum(m, jnp.max(s, axis=0, keepdims=True))
        alpha = jnp.exp(m - m_new)
        p = jnp.exp(s - m_new)
        l = alpha * l + jnp.sum(p, axis=0, keepdims=True)
        acc = alpha * acc + _dot_tn(vt, p.astype(BF16))
        return m_new, l, acc

    def body(j, carry):
        start = pl.multiple_of(j * tile, tile)
        return update(carry, k_ref[pl.ds(start, tile), :], v_ref[pl.ds(start, tile), :], None)

    init = (jnp.full((1, 2 * tile), NEG_INF, F32), jnp.zeros((1, 2 * tile), F32),
            jnp.zeros((DV_B, 2 * tile), F32))
    carry = lax.fori_loop(0, qi, body, init)
    key = lax.broadcasted_iota(jnp.int32, (tile, 2 * tile), 0)
    qry = lax.broadcasted_iota(jnp.int32, (tile, 2 * tile), 1)
    causal = key <= jnp.where(qry >= tile, qry - tile, qry)
    start = pl.multiple_of(qi * tile, tile)
    _, l, acc = update(carry, k_ref[pl.ds(start, tile), :], v_ref[pl.ds(start, tile), :], causal)
    o = acc[:, :tile] / l[:, :tile] - lam_ref[:, 0:1] * (acc[:, tile:] / l[:, tile:])
    y = o * lax.rsqrt(jnp.mean(o * o, axis=0, keepdims=True) + EPS) * sg_ref[...]
    o_ref[...] = (y.T * (1.0 - lam_init) * gb_ref[...].astype(F32)).astype(BF16)


def _prompt_attention(q, kb, vb, gb, lam, subln_g, lam_init, batch, seq):
    tile = ATTN_TILE
    q3, k3, v3, g3 = (a.reshape(batch, seq, QK_W) for a in (q, kb, vb, gb))
    qspec = pl.BlockSpec((None, tile, LANES), lambda b, h, i: (b, i, h))
    kvspec = pl.BlockSpec((None, seq, LANES), lambda b, h, i: (b, 0, h))
    out = pl.pallas_call(
        functools.partial(_attn_kernel, tile=tile, lam_init=lam_init),
        grid=(batch, H_B, seq // tile),
        in_specs=[pl.BlockSpec((1, LANES), lambda b, h, i: (0, 0)),
                  pl.BlockSpec((DV_B, 1), lambda b, h, i: (0, 0)),
                  qspec, kvspec, kvspec, qspec],
        out_specs=qspec,
        out_shape=jax.ShapeDtypeStruct((batch, seq, W_B), BF16),
        compiler_params=_cparams("arbitrary", "arbitrary", "arbitrary"),
        name="prompt_diff_attention",
    )(lam, subln_g.reshape(DV_B, 1), q3, k3, v3, g3)
    return out.reshape(batch * seq, W_B)


def _conv_post(y, ln_g, ln_b, ga):
    yc = y - jnp.mean(y, axis=-1, keepdims=True)
    yn = yc * lax.rsqrt(jnp.mean(yc * yc, axis=-1, keepdims=True) + EPS) * ln_g + ln_b
    return _silu(yn) * ga


def _conv_kernel(u_ref, ga_ref, w_ref, cb_ref, lg_ref, lb_ref, y_ref, st_ref, ext_scr, *, tile, rows):
    t = pl.program_id(1)
    first = CONV_HALO - (CONV_W - 1)

    @pl.when(t == 0)
    def _():
        ext_scr[0:CONV_HALO, :] = jnp.zeros((CONV_HALO, W_A), F32)

    @pl.when(t > 0)
    def _():
        ext_scr[0:CONV_HALO, :] = ext_scr[tile:tile + CONV_HALO, :]

    ext_scr[CONV_HALO:CONV_HALO + tile, :] = u_ref[...]
    for r0 in range(0, tile, rows):
        acc = jnp.zeros((rows, W_A), F32) + cb_ref[...]
        for j in range(CONV_W):
            acc = acc + ext_scr[first + r0 + j:first + r0 + j + rows, :] * w_ref[j:j + 1, :]
        y_ref[r0:r0 + rows, :] = _conv_post(acc, lg_ref[...], lb_ref[...],
                                            ga_ref[r0:r0 + rows, :].astype(F32)).astype(BF16)

    @pl.when(t == pl.num_programs(1) - 1)
    def _():
        st_ref[...] = ext_scr[CONV_HALO + tile - (CONV_W - 1):CONV_HALO + tile, :]


def _prompt_conv(u, ga, conv_w, conv_b, ln_g, ln_b, batch, seq):
    tile = CONV_TILE
    u3, g3 = u.reshape(batch, seq, W_A), ga.reshape(batch, seq, W_A)
    tspec = pl.BlockSpec((None, tile, W_A), lambda b, t: (b, t, 0))
    vec = pl.BlockSpec((1, W_A), lambda b, t: (0, 0))
    y, st = pl.pallas_call(
        functools.partial(_conv_kernel, tile=tile, rows=64),
        grid=(batch, seq // tile),
        in_specs=[tspec, tspec, pl.BlockSpec((CONV_W, W_A), lambda b, t: (0, 0)), vec, vec, vec],
        out_specs=[tspec, pl.BlockSpec((None, CONV_W - 1, W_A), lambda b, t: (b, 0, 0))],
        out_shape=[jax.ShapeDtypeStruct((batch, seq, W_A), BF16),
                   jax.ShapeDtypeStruct((batch, CONV_W - 1, W_A), F32)],
        scratch_shapes=[pltpu.VMEM((CONV_HALO + tile, W_A), F32)],
        compiler_params=_cparams("arbitrary", "arbitrary"),
        name="prompt_conv",
    )(u3, g3, conv_w, conv_b.reshape(1, W_A), ln_g.reshape(1, W_A), ln_b.reshape(1, W_A))
    return y.reshape(batch * seq, W_A), st


def _hgrn_tables(chunk):
    levels = int(math.log2(chunk))
    t = np.arange(chunk)[:, None]
    u = np.arange(chunk)[None, :]
    masks = []
    for v in range(levels):
        m = 1 << v
        masks.append((t // (2 * m) == u // (2 * m)) & (t % (2 * m) >= m) & (u % (2 * m) < m))
    return jnp.asarray(u <= t, BF16), jnp.asarray(np.stack(masks), F32), levels


def _split3(x):
    hi = x.astype(BF16)
    r = x - hi.astype(F32)
    mid = r.astype(BF16)
    lo = (r - mid.astype(F32)).astype(BF16)
    return hi, mid, lo


def _hgrn_level_exponent(b, g, m, row):
    chunk = b.shape[0]
    second = (row & m) != 0
    if m == 1:
        return jnp.where(second, g, 0.0)
    if 2 * m < SUBLANES:
        b3 = b.reshape(chunk // SUBLANES, SUBLANES, LANES)
        sub = lax.broadcasted_iota(jnp.int32, b3.shape, 1)
        bm = b3[:, m - 1:m, :]
        for blk in range(1, SUBLANES // (2 * m)):
            bm = jnp.where(sub < blk * 2 * m, bm, b3[:, blk * 2 * m + m - 1:blk * 2 * m + m, :])
        bm = bm.reshape(chunk, LANES)
    else:
        b3 = b.reshape(chunk // (2 * m), 2 * m, LANES)
        bm = jnp.broadcast_to(b3[:, m - 1:m, :], b3.shape).reshape(chunk, LANES)
    return jnp.where(second, b - bm, bm - b)


def _hgrn_kernel(q_ref, lg_ref, i_ref, sg_ref, tri_ref, msk_ref, gn_ref, o_ref, s_ref, st_scr,
                 *, chunk, levels):
    c = pl.program_id(1)

    @pl.when(c == 0)
    def _():
        st_scr[...] = jnp.zeros_like(st_scr)

    tri = tri_ref[...]
    row = lax.broadcasted_iota(jnp.int32, (chunk, LANES), 0)
    for h in range(H_C):
        sl = slice(h * LANES, (h + 1) * LANES)
        g = lg_ref[:, sl]
        q = q_ref[:, sl]
        v = i_ref[:, sl]
        k = 1.0 - jnp.exp(g)
        b3 = _dot(tri, jnp.concatenate(_split3(g), axis=1))
        b = b3[:, :LANES] + b3[:, LANES:2 * LANES] + b3[:, 2 * LANES:]
        vb = v.astype(BF16)
        a = jnp.zeros((chunk, chunk), F32)
        for lv in range(levels):
            ex = jnp.exp(_hgrn_level_exponent(b, g, 1 << lv, row))
            a = a + msk_ref[lv] * _dot_nt((q * ex).astype(BF16), (k * ex).astype(BF16))
        st = st_scr[h]
        o = _dot(a.astype(BF16), vb) + jnp.sum(q * k, axis=1, keepdims=True) * v
        o = o + _dot_nt((q * jnp.exp(b)).astype(BF16), st.astype(BF16))
        b_last = b[chunk - 1:chunk, :]
        kd = (k * jnp.exp(b_last - b)).astype(BF16)
        st_scr[h] = st * jnp.exp(b_last) + _dot_tn(vb, kd)
        o_ref[:, sl] = (_rms(o, gn_ref[...]) * sg_ref[:, sl].astype(F32)).astype(BF16)

    @pl.when(c == pl.num_programs(1) - 1)
    def _():
        for h in range(H_C):
            s_ref[h] = st_scr[h].T


def _prompt_hgrn(qs, lg, iv, sg, gn_g, batch, seq):
    chunk = HGRN_CHUNK
    tri, masks, levels = _hgrn_tables(chunk)
    a3 = [a.reshape(batch, seq, W_C) for a in (qs, lg, iv, sg)]
    tspec = pl.BlockSpec((None, chunk, W_C), lambda b, c: (b, c, 0))
    o, s = pl.pallas_call(
        functools.partial(_hgrn_kernel, chunk=chunk, levels=levels),
        grid=(batch, seq // chunk),
        in_specs=[tspec, tspec, tspec, tspec,
                  pl.BlockSpec(tri.shape, lambda b, c: (0, 0)),
                  pl.BlockSpec(masks.shape, lambda b, c: (0, 0, 0)),
                  pl.BlockSpec((1, DV_C), lambda b, c: (0, 0))],
        out_specs=[tspec, pl.BlockSpec((None, H_C, DK_C, DV_C), lambda b, c: (b, 0, 0, 0))],
        out_shape=[jax.ShapeDtypeStruct((batch, seq, W_C), BF16),
                   jax.ShapeDtypeStruct((batch, H_C, DK_C, DV_C), F32)],
        scratch_shapes=[pltpu.VMEM((H_C, DV_C, DK_C), F32)],
        compiler_params=_cparams("arbitrary", "arbitrary"),
        name="prompt_hgrn",
    )(*a3, tri, masks, gn_g.reshape(1, DV_C))
    return o.reshape(batch * seq, W_C), s


def _decode_attn_kernel(pt_ref, lam_ref, sg_ref, q_ref, kn_ref, vn_ref, gb_ref, *refs, n_pages, lam_init):
    k_pages = refs[:n_pages]
    v_pages = refs[n_pages:2 * n_pages]
    o_ref = refs[2 * n_pages]
    n_rows = 2 * H_B
    page_rows = H_B * PAGE_SIZE
    row = lax.broadcasted_iota(jnp.int32, (n_rows, LANES), 0)
    lane = lax.broadcasted_iota(jnp.int32, (n_rows, LANES), 1)

    def head_rows(pieces):
        out = jnp.zeros((n_rows, LANES), F32)
        for h in range(H_B):
            out = jnp.where(row // 2 == h, jnp.broadcast_to(pieces[h], (n_rows, LANES)), out)
        return out

    def by_row(ref):
        return [ref[h:h + 1, :] for h in range(H_B)]

    q = q_ref[...].astype(F32)
    qm32 = jnp.where(lane // HD_B == row % 2,
                     head_rows([q[:, h * LANES:(h + 1) * LANES] for h in range(H_B)]), 0.0)
    qm = qm32.astype(BF16)
    s = jnp.concatenate([_dot_nt(qm, kp[...].astype(BF16)) for kp in k_pages], axis=1)
    key_head = lax.broadcasted_iota(jnp.int32, s.shape, 1) % H_B
    s = jnp.where(key_head == lax.broadcasted_iota(jnp.int32, s.shape, 0) // 2, s, NEG_INF)
    s_new = jnp.sum(qm32 * head_rows(by_row(kn_ref)), axis=-1, keepdims=True)
    m = jnp.maximum(jnp.max(s, axis=-1, keepdims=True), s_new)
    p = jnp.exp(s - m)
    p_new = jnp.exp(s_new - m)
    l = jnp.sum(p, axis=-1, keepdims=True) + p_new
    comp = lax.broadcasted_iota(jnp.int32, (n_rows, 1), 0) % 2
    wgt = jnp.where(comp == 0, 1.0, -lam_ref[:, 0:1]) / l
    pw = (p * wgt).astype(BF16)
    o8 = (p_new * wgt) * head_rows(by_row(vn_ref))
    for j, vp in enumerate(v_pages):
        o8 = o8 + _dot(pw[:, j * page_rows:(j + 1) * page_rows], vp[...].astype(BF16))
    for h in range(H_B):
        sl = slice(h * DV_B, (h + 1) * DV_B)
        o = o8[2 * h:2 * h + 1, :] + o8[2 * h + 1:2 * h + 2, :]
        y = _rms(o, sg_ref[...]) * (1.0 - lam_init) * gb_ref[:, sl].astype(F32)
        o_ref[:, sl] = y.astype(BF16)


def _decode_attention(q, k_new, v_new, gb, cache_k4, cache_v4, layer, page_table, lam, subln_g, lam_init):
    n_seq, n_pages = page_table.shape
    row = pl.BlockSpec((None, 1, QK_W), lambda b, pt: (b, 0, 0))
    heads = pl.BlockSpec((None, H_B, LANES), lambda b, pt: (b, 0, 0))
    vec = pl.BlockSpec((1, LANES), lambda b, pt: (0, 0))

    def page_spec(j):
        return pl.BlockSpec((None, None, H_B * PAGE_SIZE, LANES), lambda b, pt: (layer, pt[b, j], 0, 0))

    pages = [page_spec(j) for j in range(n_pages)]
    grid_spec = pltpu.PrefetchScalarGridSpec(
        num_scalar_prefetch=1,
        grid=(n_seq,),
        in_specs=[vec, vec, row, heads, heads, row] + pages + pages,
        out_specs=pl.BlockSpec((None, 1, W_B), lambda b, pt: (b, 0, 0)),
    )
    r3 = lambda a: a.reshape(n_seq, 1, QK_W)
    h3 = lambda a: a.reshape(n_seq, H_B, LANES)
    out = pl.pallas_call(
        functools.partial(_decode_attn_kernel, n_pages=n_pages, lam_init=lam_init),
        grid_spec=grid_spec,
        out_shape=jax.ShapeDtypeStruct((n_seq, 1, W_B), BF16),
        compiler_params=_cparams("arbitrary"),
        name="decode_diff_attention",
    )(page_table, lam, subln_g.reshape(1, DV_B), r3(q), h3(k_new), h3(v_new), r3(gb),
      *([cache_k4] * n_pages), *([cache_v4] * n_pages))
    return out.reshape(n_seq, W_B)


def _decode_conv_kernel(st_ref, u_ref, ga_ref, w_ref, cb_ref, lg_ref, lb_ref, y_ref, ns_ref):
    st = st_ref[...]
    u = u_ref[...]
    w = w_ref[...]
    y = jnp.sum(st * w[None, :CONV_W - 1, :], axis=1) + u * w[CONV_W - 1:CONV_W, :] + cb_ref[...]
    y_ref[...] = _conv_post(y, lg_ref[...], lb_ref[...], ga_ref[...].astype(F32)).astype(BF16)
    ns_ref[:, 0:CONV_W - 2, :] = st[:, 1:CONV_W - 1, :]
    ns_ref[:, CONV_W - 2:CONV_W - 1, :] = u[:, None, :]


def _decode_conv(state_all, layer, u, ga, conv_w, conv_b, ln_g, ln_b):
    n = u.shape[0]
    tile = 32
    vec = pl.BlockSpec((1, W_A), lambda i: (0, 0))
    sspec = pl.BlockSpec((tile, CONV_W - 1, W_A), lambda i: (i, 0, 0))
    rspec = pl.BlockSpec((tile, W_A), lambda i: (i, 0))
    return pl.pallas_call(
        _decode_conv_kernel,
        grid=(n // tile,),
        in_specs=[pl.BlockSpec((None, tile, CONV_W - 1, W_A), lambda i: (layer, i, 0, 0)),
                  rspec, rspec, pl.BlockSpec((CONV_W, W_A), lambda i: (0, 0)), vec, vec, vec],
        out_specs=[rspec, sspec],
        out_shape=[jax.ShapeDtypeStruct((n, W_A), BF16),
                   jax.ShapeDtypeStruct((n, CONV_W - 1, W_A), F32)],
        compiler_params=_cparams("arbitrary"),
        name="decode_conv",
    )(state_all, u, ga, conv_w, conv_b.reshape(1, W_A), ln_g.reshape(1, W_A), ln_b.reshape(1, W_A))


def _decode_hgrn_kernel(q_ref, lg_ref, i_ref, sg_ref, gn_ref, s_ref, o_ref, ns_ref, *, n_seq):
    eye = (lax.broadcasted_iota(jnp.int32, (DK_C, LANES), 0)
           == lax.broadcasted_iota(jnp.int32, (DK_C, LANES), 1))

    def column(r):
        return jnp.sum(jnp.where(eye, jnp.broadcast_to(r, (DK_C, LANES)), 0.0), axis=1, keepdims=True)

    def body(n, carry):
        for h in range(H_C):
            sl = slice(h * LANES, (h + 1) * LANES)
            g = lg_ref[n, :, sl]
            f = jnp.exp(g)
            s_new = column(f) * s_ref[n, h] + column(1.0 - f) * i_ref[n, :, sl]
            ns_ref[n, h] = s_new
            o = jnp.sum(column(q_ref[n, :, sl]) * s_new, axis=0, keepdims=True)
            o_ref[n, :, sl] = (_rms(o, gn_ref[...]) * sg_ref[n, :, sl].astype(F32)).astype(BF16)
        return carry

    lax.fori_loop(0, n_seq, body, 0)


def _decode_hgrn(qs, lg, iv, sg, gn_g, state_all, layer):
    n = qs.shape[0]
    tile = SAMPLE_STATE_TILE
    rspec = pl.BlockSpec((tile, 1, W_C), lambda i: (i, 0, 0))
    sspec = pl.BlockSpec((tile, H_C, DK_C, DV_C), lambda i: (i, 0, 0, 0))
    sin_spec = pl.BlockSpec((None, tile, H_C, DK_C, DV_C), lambda i: (layer, i, 0, 0, 0))
    r3 = lambda a: a.reshape(n, 1, W_C)
    o, ns = pl.pallas_call(
        functools.partial(_decode_hgrn_kernel, n_seq=tile),
        grid=(n // tile,),
        in_specs=[rspec, rspec, rspec, rspec, pl.BlockSpec((1, DV_C), lambda i: (0, 0)), sin_spec],
        out_specs=[rspec, sspec],
        out_shape=[jax.ShapeDtypeStruct((n, 1, W_C), BF16),
                   jax.ShapeDtypeStruct((n, H_C, DK_C, DV_C), F32)],
        compiler_params=_cparams("arbitrary"),
        name="decode_hgrn",
    )(r3(qs), r3(lg), r3(iv), r3(sg), gn_g.reshape(1, DV_C), state_all)
    return o.reshape(n, W_C), ns


def _rope_tables(pos, rows):
    half = ROT_DIM // 2
    inv_freq = ROPE_THETA ** (-jnp.arange(half, dtype=F32) / half)
    ang = pos.astype(F32)[:, None] * inv_freq[None, :]
    cos, sin = jnp.cos(ang), jnp.sin(ang)
    n = pos.shape[0]
    ones = jnp.ones((n, HD_B - ROT_DIM), F32)
    zeros = jnp.zeros((n, HD_B - ROT_DIM), F32)
    z8 = jnp.zeros((n, half), F32)
    cos_t = jnp.concatenate([cos, cos, ones], axis=1)
    sa_t = jnp.concatenate([-sin, z8, zeros], axis=1)
    sb_t = jnp.concatenate([z8, sin, zeros], axis=1)
    out = []
    for t in (cos_t, sa_t, sb_t):
        t = jnp.concatenate([t, t], axis=1)
        out.append(jnp.broadcast_to(t, (rows, LANES)) if n == 1 else t)
    return out


def kernel(x_prompt, x_sample, c_prompt, c_sample, cache_k, cache_v, page_table, state_conv, state_hgrn,
           norm_g, ada_w, ada_b, w_in_ac, w_out_ac, conv_w, conv_b, ln_g, ln_b, lam_q1, lam_k1, lam_q2,
           lam_k2, subln_g, w_in_c, w_out_c, gn_g, lb_logits, final_g):
    bp, tp = x_prompt.shape[:2]
    bs, ts = x_sample.shape[:2]
    assert ts == 1
    n_pool = cache_k.shape[1]
    n_past = page_table.shape[1] * PAGE_SIZE
    mp = bp * tp

    lb_sm = jax.nn.softmax(lb_logits.astype(F32), axis=0)
    lb_all = jnp.cumsum(lb_sm, axis=0) - lb_sm[0]

    mod = _ada_modulation(jnp.concatenate([c_prompt, c_sample], axis=0), ada_w, ada_b)

    def mods(l):
        out = []
        for j in range(3):
            m = mod[l, :, j * D_MODEL:(j + 1) * D_MODEL]
            out.append((m[:bp].reshape(bp, 1, D_MODEL), m[bp:].reshape(1, bs, D_MODEL)))
        return out

    tabs_p = _rope_tables(jnp.arange(tp), tp)
    tabs_s = _rope_tables(jnp.full((1,), n_past), bs)
    ck4 = cache_k.reshape(N_AC, n_pool, PAGE_SIZE * H_B, 2 * HD_B)
    cv4 = cache_v.reshape(N_AC, n_pool, PAGE_SIZE * H_B, DV_B)

    tiles_p = tp // ROW_TILE
    hp = x_prompt.reshape(mp, D_MODEL)
    hs = x_sample.reshape(bs, D_MODEL)
    kp_l, vp_l, cp_l, sp_l, ks_l, vs_l, cs_l, ss_l = ([] for _ in range(8))
    for l in range(DEPTH):
        (sh_p, sh_s), (sc_p, sc_s), (gt_p, gt_s) = mods(l)
        last = final_g if l == DEPTH - 1 else None
        if l % 2 == 0:
            a = l // 2
            lam_init = 0.8 - 0.6 * math.exp(-0.3 * l)
            lam = (jnp.exp(jnp.sum(lam_q1[a].astype(F32) * lam_k1[a].astype(F32)))
                   - jnp.exp(jnp.sum(lam_q2[a].astype(F32) * lam_k2[a].astype(F32))) + lam_init)
            lam = jnp.full((1, LANES), lam, F32)
            w_in = w_in_ac[a].astype(BF16)
            w_out = w_out_ac[a].astype(BF16)
            q, k, kb, v, vb, gb, u, ga = _inproj_ac(hp, norm_g[l], sc_p, sh_p, w_in, *tabs_p,
                                                    ROW_TILE, tiles_p)
            ob = _prompt_attention(q, kb, vb, gb, lam, subln_g[a], lam_init, bp, tp)
            y, cst = _prompt_conv(u, ga, conv_w[a], conv_b[a], ln_g[a], ln_b[a], bp, tp)
            hp = _outproj([ob, y], w_out, hp, gt_p, ROW_TILE, tiles_p, last)
            kp_l.append(k.reshape(bp, tp, H_B, 2 * HD_B))
            vp_l.append(v.reshape(bp, tp, H_B, DV_B))
            cp_l.append(cst)
            q, k, kb, v, vb, gb, u, ga = _inproj_ac(hs, norm_g[l], sc_s, sh_s, w_in, *tabs_s, bs, 1)
            ob = _decode_attention(q, k, v, gb, ck4, cv4, a, page_table, lam, subln_g[a], lam_init)
            y, cst = _decode_conv(state_conv, a, u, ga, conv_w[a], conv_b[a], ln_g[a], ln_b[a])
            hs = _outproj([ob, y], w_out, hs, gt_s, bs, 1, last)
            ks_l.append(k.reshape(bs, ts, H_B, 2 * HD_B))
            vs_l.append(v.reshape(bs, ts, H_B, DV_B))
            cs_l.append(cst)
        else:
            ci = l // 2
            w_in = w_in_c[ci].astype(BF16)
            w_out = w_out_c[ci].astype(BF16)
            qs, lg, iv, sg = _inproj_c(hp, norm_g[l], sc_p, sh_p, w_in, lb_all[l], ROW_TILE, tiles_p)
            o, st = _prompt_hgrn(qs, lg, iv, sg, gn_g[ci], bp, tp)
            hp = _outproj([o], w_out, hp, gt_p, ROW_TILE, tiles_p, last)
            sp_l.append(st)
            qs, lg, iv, sg = _inproj_c(hs, norm_g[l], sc_s, sh_s, w_in, lb_all[l], bs, 1)
            o, st = _decode_hgrn(qs, lg, iv, sg, gn_g[ci], state_hgrn, ci)
            hs = _outproj([o], w_out, hs, gt_s, bs, 1, last)
            ss_l.append(st)

    y_prompt = hp.reshape(bp, tp, D_MODEL)
    y_sample = hs.reshape(bs, ts, D_MODEL)
    return (y_prompt, y_sample,
            jnp.stack(kp_l), jnp.stack(vp_l), jnp.stack(cp_l), jnp.stack(sp_l),
            jnp.stack(ks_l), jnp.stack(vs_l), jnp.stack(cs_l), jnp.stack(ss_l))
```

```python
import functools
import math

import numpy as np
import jax
import jax.numpy as jnp
from jax import lax
from jax.experimental import pallas as pl
from jax.experimental.pallas import tpu as pltpu

F32 = jnp.float32
BF16 = jnp.bfloat16

D_MODEL = 1024
DEPTH = 4
PAGE_SIZE = 128
N_AC = (DEPTH + 1) // 2
N_C = DEPTH // 2

H_B = 4
HD_B = 64
DV_B = 2 * HD_B
W_B = H_B * DV_B
QK_W = H_B * 2 * HD_B
ROT_DIM = HD_B // 4
ROPE_THETA = 500000.0
NEG_INF = -1e30

W_A = D_MODEL // 2
CONV_W = 31

H_C = 8
DK_C = D_MODEL // H_C
DV_C = D_MODEL // H_C
W_C = H_C * DV_C

EPS = 1e-6
LOG2_E = math.log2(math.e)
Q_SCALE = HD_B ** -0.5 * LOG2_E

AC_IN = 2 * QK_W + 2 * W_B + 3 * W_A
C_IN = 2 * H_C * DK_C + 2 * W_C

LANES = 128
SUBLANES = 8
VMEM_LIMIT_BYTES = 56 * 1024 * 1024

ROW_TILE = 512
ATTN_TILE = 256
CONV_TILE = 512
CONV_HALO = 32
HGRN_CHUNK = 128
SAMPLE_STATE_TILE = 8


def _cparams(*sem):
    return pltpu.CompilerParams(dimension_semantics=sem, vmem_limit_bytes=VMEM_LIMIT_BYTES)


def _silu(x):
    return x * jax.nn.sigmoid(x)


def _dot(a, b):
    return jnp.dot(a, b, preferred_element_type=F32)


def _dot_nt(a, b):
    return lax.dot_general(a, b, (((1,), (1,)), ((), ())), preferred_element_type=F32)


def _dot_tn(a, b):
    return lax.dot_general(a, b, (((0,), (0,)), ((), ())), preferred_element_type=F32)


def _rms(x, g):
    return x * lax.rsqrt(jnp.mean(x * x, axis=-1, keepdims=True) + EPS) * g


def _ada_kernel(c_ref, w_ref, b_ref, o_ref):
    s = _silu(c_ref[...]).astype(BF16)
    o_ref[...] = _dot(s, w_ref[...].astype(BF16)) + b_ref[...]


def _ada_modulation(c_all, ada_w, ada_b):
    rows = c_all.shape[0]
    return pl.pallas_call(
        _ada_kernel,
        grid=(DEPTH, 3),
        in_specs=[
            pl.BlockSpec((rows, D_MODEL), lambda l, j: (0, 0)),
            pl.BlockSpec((None, D_MODEL, D_MODEL), lambda l, j: (l, 0, j)),
            pl.BlockSpec((None, 1, D_MODEL), lambda l, j: (l, 0, j)),
        ],
        out_specs=pl.BlockSpec((None, rows, D_MODEL), lambda l, j: (l, 0, j)),
        out_shape=jax.ShapeDtypeStruct((DEPTH, rows, 3 * D_MODEL), F32),
        compiler_params=_cparams("arbitrary", "arbitrary"),
        name="ada_modulation",
    )(c_all, ada_w, ada_b.reshape(DEPTH, 1, 3 * D_MODEL))


def _modulated(x_ref, ng_ref, sc_ref, sh_ref):
    h = _rms(x_ref[...], ng_ref[...])
    return (h * (1.0 + sc_ref[...]) + sh_ref[...]).astype(BF16)


def _inproj_ac_kernel(x_ref, ng_ref, sc_ref, sh_ref, w_ref, cos_ref, sa_ref, sb_ref, *rest):
    q_ref, k_ref, kb_ref, v_ref, vb_ref, gb_ref, u_ref, ga_ref, h_scr = rest[-9:]
    h_scr[...] = _modulated(x_ref, ng_ref, sc_ref, sh_ref)

    def proj(s):
        return _dot(h_scr[...], w_ref[:, s * QK_W:(s + 1) * QK_W])

    cos, sa, sb = cos_ref[...], sa_ref[...], sb_ref[...]

    def rope(z):
        return (z * cos + pltpu.roll(z, LANES - ROT_DIM // 2, 1) * sa
                + pltpu.roll(z, ROT_DIM // 2, 1) * sb)

    zq = proj(0)
    for g in range(H_B):
        sl = slice(g * LANES, (g + 1) * LANES)
        q_ref[:, sl] = (rope(zq[:, sl]) * Q_SCALE).astype(BF16)
    tm = x_ref.shape[0]
    zk = proj(1)
    for g in range(H_B):
        sl = slice(g * LANES, (g + 1) * LANES)
        r = rope(zk[:, sl])
        k_ref[pl.ds(g, tm, stride=H_B), :] = r
        kb_ref[:, sl] = r.astype(BF16)
    zv = proj(2)
    for g in range(H_B):
        v_ref[pl.ds(g, tm, stride=H_B), :] = zv[:, g * LANES:(g + 1) * LANES]
    vb_ref[...] = zv.astype(BF16)
    gb_ref[...] = _silu(proj(3)).astype(BF16)
    a_val = proj(4)
    u_ref[...] = a_val * jax.nn.sigmoid(proj(5))
    ga_ref[...] = _silu(proj(6)).astype(BF16)


def _mod_specs(rows_per_mod, tm, tiles_per_mod):
    r = rows_per_mod
    return pl.BlockSpec((None, r, D_MODEL), lambda i: (i // tiles_per_mod, 0, 0))


def _inproj_ac(x, norm_g, scale, shift, w, cos_t, sa_t, sb_t, tm, tiles_per_mod, layer, kv_prev):
    m = x.shape[0]
    tab_tiles = cos_t.shape[0] // tm
    mod_spec = _mod_specs(scale.shape[1], tm, tiles_per_mod)
    row512 = pl.BlockSpec((tm, QK_W), lambda i: (i, 0))
    heads = pl.BlockSpec((None, H_B * tm, LANES), lambda i: (layer, i, 0))
    tab_spec = pl.BlockSpec((tm, LANES), lambda i: (i % tab_tiles, 0))
    out_shape = [
        jax.ShapeDtypeStruct((m, QK_W), BF16),
        jax.ShapeDtypeStruct((N_AC, H_B * m, LANES), F32),
        jax.ShapeDtypeStruct((m, QK_W), BF16),
        jax.ShapeDtypeStruct((N_AC, H_B * m, LANES), F32),
        jax.ShapeDtypeStruct((m, W_B), BF16),
        jax.ShapeDtypeStruct((m, W_B), BF16),
        jax.ShapeDtypeStruct((m, W_A), F32),
        jax.ShapeDtypeStruct((m, W_A), BF16),
    ]
    in_specs = [
        pl.BlockSpec((tm, D_MODEL), lambda i: (i, 0)),
        pl.BlockSpec((1, D_MODEL), lambda i: (0, 0)),
        mod_spec, mod_spec,
        pl.BlockSpec((D_MODEL, AC_IN), lambda i: (0, 0)),
        tab_spec, tab_spec, tab_spec,
    ]
    args = [x, norm_g.reshape(1, D_MODEL), scale, shift, w, cos_t, sa_t, sb_t]
    aliases = {}
    if kv_prev is not None:
        aliases = {len(args): 1, len(args) + 1: 3}
        in_specs += [pl.BlockSpec(memory_space=pl.ANY)] * 2
        args += list(kv_prev)
    return pl.pallas_call(
        _inproj_ac_kernel,
        grid=(m // tm,),
        in_specs=in_specs,
        out_specs=[row512, heads, row512, heads, row512, row512, row512, row512],
        out_shape=out_shape,
        input_output_aliases=aliases,
        scratch_shapes=[pltpu.VMEM((tm, D_MODEL), BF16)],
        compiler_params=_cparams("arbitrary"),
        name="inproj_conv_attn",
    )(*args)


def _inproj_c_kernel(x_ref, ng_ref, sc_ref, sh_ref, w_ref, lb_ref, q_ref, lg_ref, i_ref, sg_ref, h_scr):
    h_scr[...] = _modulated(x_ref, ng_ref, sc_ref, sh_ref)

    def proj(s):
        return _dot(h_scr[...], w_ref[:, s * W_C:(s + 1) * W_C])

    q_ref[...] = _silu(proj(0))
    lb = lb_ref[...]
    lg_ref[...] = jnp.log(lb + (1.0 - lb) * jax.nn.sigmoid(proj(1))) * LOG2_E
    i_ref[...] = proj(2)
    sg_ref[...] = _silu(proj(3)).astype(BF16)


def _inproj_c(x, norm_g, scale, shift, w, lb, tm, tiles_per_mod):
    m = x.shape[0]
    mod_spec = _mod_specs(scale.shape[1], tm, tiles_per_mod)
    row = pl.BlockSpec((tm, W_C), lambda i: (i, 0))
    out_shape = [
        jax.ShapeDtypeStruct((m, W_C), F32),
        jax.ShapeDtypeStruct((m, W_C), F32),
        jax.ShapeDtypeStruct((m, W_C), F32),
        jax.ShapeDtypeStruct((m, W_C), BF16),
    ]
    return pl.pallas_call(
        _inproj_c_kernel,
        grid=(m // tm,),
        in_specs=[
            pl.BlockSpec((tm, D_MODEL), lambda i: (i, 0)),
            pl.BlockSpec((1, D_MODEL), lambda i: (0, 0)),
            mod_spec, mod_spec,
            pl.BlockSpec((D_MODEL, C_IN), lambda i: (0, 0)),
            pl.BlockSpec((1, W_C), lambda i: (0, 0)),
        ],
        out_specs=[row] * 4,
        out_shape=out_shape,
        scratch_shapes=[pltpu.VMEM((tm, D_MODEL), BF16)],
        compiler_params=_cparams("arbitrary"),
        name="inproj_hgrn",
    )(x, norm_g.reshape(1, D_MODEL), scale, shift, w, lb.reshape(1, W_C))


def _outproj_kernel(*refs, n_act, final_norm):
    acts = refs[:n_act]
    w_ref, x_ref, gate_ref = refs[n_act:n_act + 3]
    o_ref = refs[-1]
    acc = None
    lo = 0
    for a_ref in acts:
        width = a_ref.shape[-1]
        part = _dot(a_ref[...], w_ref[lo:lo + width, :])
        acc = part if acc is None else acc + part
        lo += width
    y = x_ref[...] + gate_ref[...] * acc
    if final_norm:
        y = _rms(y, refs[n_act + 3][...])
    o_ref[...] = y


def _outproj(acts, w, x, gate, tm, tiles_per_mod, final_g=None):
    m = x.shape[0]
    mod_spec = _mod_specs(gate.shape[1], tm, tiles_per_mod)
    in_specs = [pl.BlockSpec((tm, a.shape[1]), lambda i: (i, 0)) for a in acts]
    in_specs += [
        pl.BlockSpec((D_MODEL, D_MODEL), lambda i: (0, 0)),
        pl.BlockSpec((tm, D_MODEL), lambda i: (i, 0)),
        mod_spec,
    ]
    args = list(acts) + [w, x, gate]
    if final_g is not None:
        in_specs.append(pl.BlockSpec((1, D_MODEL), lambda i: (0, 0)))
        args.append(final_g.reshape(1, D_MODEL))
    return pl.pallas_call(
        functools.partial(_outproj_kernel, n_act=len(acts), final_norm=final_g is not None),
        grid=(m // tm,),
        in_specs=in_specs,
        out_specs=pl.BlockSpec((tm, D_MODEL), lambda i: (i, 0)),
        out_shape=jax.ShapeDtypeStruct((m, D_MODEL), F32),
        compiler_params=_cparams("arbitrary"),
        name="outproj",
    )(*args)


def _attn_kernel(lam_ref, sg_ref, q_ref, k_ref, v_ref, gb_ref, o_ref, q2_scr, m_scr, l_scr, acc_scr,
                 *, tile, lam_init):
    qi = pl.program_id(1)
    heads = [slice(h * LANES, (h + 1) * LANES) for h in range(H_B)]
    lane = lax.broadcasted_iota(jnp.int32, (tile, LANES), 1)
    for h, sl in enumerate(heads):
        q = q_ref[:, sl]
        zero = jnp.zeros_like(q)
        q2_scr[h] = jnp.concatenate([jnp.where(lane < HD_B, q, zero), jnp.where(lane >= HD_B, q, zero)], axis=0)
    m_scr[...] = jnp.full(m_scr.shape, NEG_INF, F32)
    l_scr[...] = jnp.zeros(l_scr.shape, F32)
    acc_scr[...] = jnp.zeros(acc_scr.shape, F32)

    def step(start, mask):
        s = [_dot_nt(k_ref[pl.ds(start, tile), sl], q2_scr[h]) for h, sl in enumerate(heads)]
        for h, sl in enumerate(heads):
            sh = s[h] if mask is None else jnp.where(mask, s[h], NEG_INF)
            m_old = m_scr[h]
            m_new = jnp.maximum(m_old, jnp.max(sh, axis=0, keepdims=True))
            alpha = jnp.exp2(m_old - m_new)
            p = jnp.exp2(sh - m_new)
            m_scr[h] = m_new
            l_scr[h] = alpha * l_scr[h] + jnp.sum(p, axis=0, keepdims=True)
            acc_scr[h] = alpha * acc_scr[h] + _dot_tn(v_ref[pl.ds(start, tile), sl], p.astype(BF16))

    def body(j, carry):
        step(pl.multiple_of(j * tile, tile), None)
        return carry

    lax.fori_loop(0, qi, body, 0)
    key = lax.broadcasted_iota(jnp.int32, (tile, 2 * tile), 0)
    qry = lax.broadcasted_iota(jnp.int32, (tile, 2 * tile), 1)
    step(pl.multiple_of(qi * tile, tile), key <= jnp.where(qry >= tile, qry - tile, qry))
    for h, sl in enumerate(heads):
        l, acc = l_scr[h], acc_scr[h]
        o = acc[:, :tile] / l[:, :tile] - lam_ref[:, 0:1] * (acc[:, tile:] / l[:, tile:])
        y = o * lax.rsqrt(jnp.mean(o * o, axis=0, keepdims=True) + EPS) * sg_ref[...]
        o_ref[:, sl] = (y.T * (1.0 - lam_init) * gb_ref[:, sl].astype(F32)).astype(BF16)


def _prompt_attention(q, kb, vb, gb, lam, subln_g, lam_init, batch, seq):
    tile = ATTN_TILE
    q3, k3, v3, g3 = (a.reshape(batch, seq, QK_W) for a in (q, kb, vb, gb))
    qspec = pl.BlockSpec((None, tile, QK_W), lambda b, i: (b, i, 0))
    kvspec = pl.BlockSpec((None, seq, QK_W), lambda b, i: (b, 0, 0))
    out = pl.pallas_call(
        functools.partial(_attn_kernel, tile=tile, lam_init=lam_init),
        grid=(batch, seq // tile),
        in_specs=[pl.BlockSpec((1, LANES), lambda b, i: (0, 0)),
                  pl.BlockSpec((DV_B, 1), lambda b, i: (0, 0)),
                  qspec, kvspec, kvspec, qspec],
        out_specs=qspec,
        out_shape=jax.ShapeDtypeStruct((batch, seq, W_B), BF16),
        scratch_shapes=[pltpu.VMEM((H_B, 2 * tile, LANES), BF16),
                        pltpu.VMEM((H_B, 1, 2 * tile), F32),
                        pltpu.VMEM((H_B, 1, 2 * tile), F32),
                        pltpu.VMEM((H_B, DV_B, 2 * tile), F32)],
        compiler_params=_cparams("arbitrary", "arbitrary"),
        name="prompt_diff_attention",
    )(lam, subln_g.reshape(DV_B, 1), q3, k3, v3, g3)
    return out.reshape(batch * seq, W_B)


def _conv_post(y, ln_g, ln_b, ga):
    yc = y - jnp.mean(y, axis=-1, keepdims=True)
    yn = yc * lax.rsqrt(jnp.mean(yc * yc, axis=-1, keepdims=True) + EPS) * ln_g + ln_b
    return _silu(yn) * ga


def _conv_kernel(u_ref, ga_ref, w_ref, cb_ref, lg_ref, lb_ref, y_ref, st_ref, ext_scr, sh_scr, *, tile, rows):
    t = pl.program_id(1)
    first = CONV_HALO - (CONV_W - 1)
    sh_rows = sh_scr.shape[1]

    @pl.when(t == 0)
    def _():
        ext_scr[0:CONV_HALO, :] = jnp.zeros((CONV_HALO, W_A), F32)

    @pl.when(t > 0)
    def _():
        ext_scr[0:CONV_HALO, :] = ext_scr[tile:tile + CONV_HALO, :]

    ext_scr[CONV_HALO:CONV_HALO + tile, :] = u_ref[...]
    for r in range(1, SUBLANES):
        sh_scr[r - 1] = ext_scr[r:r + sh_rows, :]
    for r0 in range(0, tile, rows):
        acc = jnp.zeros((rows, W_A), F32) + cb_ref[...]
        for j in range(CONV_W):
            off = first + j
            r, base = off % SUBLANES, off - off % SUBLANES + r0
            src = ext_scr[base:base + rows, :] if r == 0 else sh_scr[r - 1, base:base + rows, :]
            acc = acc + src * w_ref[j:j + 1, :]
        y_ref[r0:r0 + rows, :] = _conv_post(acc, lg_ref[...], lb_ref[...],
                                            ga_ref[r0:r0 + rows, :].astype(F32)).astype(BF16)

    @pl.when(t == pl.num_programs(1) - 1)
    def _():
        st_ref[...] = ext_scr[CONV_HALO + tile - (CONV_W - 1):CONV_HALO + tile, :]


def _prompt_conv(u, ga, conv_w, conv_b, ln_g, ln_b, batch, seq):
    tile = CONV_TILE
    u3, g3 = u.reshape(batch, seq, W_A), ga.reshape(batch, seq, W_A)
    tspec = pl.BlockSpec((None, tile, W_A), lambda b, t: (b, t, 0))
    vec = pl.BlockSpec((1, W_A), lambda b, t: (0, 0))
    y, st = pl.pallas_call(
        functools.partial(_conv_kernel, tile=tile, rows=64),
        grid=(batch, seq // tile),
        in_specs=[tspec, tspec, pl.BlockSpec((CONV_W, W_A), lambda b, t: (0, 0)), vec, vec, vec],
        out_specs=[tspec, pl.BlockSpec((None, CONV_W - 1, W_A), lambda b, t: (b, 0, 0))],
        out_shape=[jax.ShapeDtypeStruct((batch, seq, W_A), BF16),
                   jax.ShapeDtypeStruct((batch, CONV_W - 1, W_A), F32)],
        scratch_shapes=[pltpu.VMEM((CONV_HALO + tile, W_A), F32),
                        pltpu.VMEM((SUBLANES - 1, CONV_HALO + tile - SUBLANES, W_A), F32)],
        compiler_params=_cparams("arbitrary", "arbitrary"),
        name="prompt_conv",
    )(u3, g3, conv_w, conv_b.reshape(1, W_A), ln_g.reshape(1, W_A), ln_b.reshape(1, W_A))
    return y.reshape(batch * seq, W_A), st


def _hgrn_tables(chunk):
    levels = int(math.log2(chunk))
    t = np.arange(chunk)[:, None]
    u = np.arange(chunk)[None, :]
    masks = []
    for v in range(levels):
        m = 1 << v
        masks.append((t // (2 * m) == u // (2 * m)) & (t % (2 * m) >= m) & (u % (2 * m) < m))
    return jnp.asarray(u <= t, BF16), jnp.asarray(np.stack(masks), F32), levels


def _split3(x):
    hi = x.astype(BF16)
    r = x - hi.astype(F32)
    mid = r.astype(BF16)
    lo = (r - mid.astype(F32)).astype(BF16)
    return hi, mid, lo


def _hgrn_level_exponent(b, g2, m, row):
    chunk, width = b.shape
    if m == 1:
        return jnp.where((row & 1) != 0, g2, 0.0)
    if 2 * m < SUBLANES:
        b3 = b.reshape(chunk // SUBLANES, SUBLANES, width)
        sub = lax.broadcasted_iota(jnp.int32, b3.shape, 1)
        bm = b3[:, m - 1:m, :]
        for blk in range(1, SUBLANES // (2 * m)):
            bm = jnp.where(sub < blk * 2 * m, bm, b3[:, blk * 2 * m + m - 1:blk * 2 * m + m, :])
    else:
        b3 = b.reshape(chunk // (2 * m), 2 * m, width)
        bm = b3[:, m - 1:m, :]
    return -jnp.abs((b3 - bm).reshape(chunk, width))


def _hgrn_kernel(q_ref, lg_ref, i_ref, sg_ref, tri_ref, msk_ref, gn_ref, o_ref, s_ref, st_scr,
                 *, chunk, levels):
    c = pl.program_id(1)

    @pl.when(c == 0)
    def _():
        st_scr[...] = jnp.zeros_like(st_scr)

    heads = [slice(h * LANES, (h + 1) * LANES) for h in range(H_C)]
    g2 = lg_ref[...]
    q = q_ref[...]
    v = i_ref[...]
    k = 1.0 - jnp.exp2(g2)
    b3 = _dot(tri_ref[...], jnp.concatenate(_split3(g2), axis=1))
    b = b3[:, :W_C] + b3[:, W_C:2 * W_C] + b3[:, 2 * W_C:]
    row = lax.broadcasted_iota(jnp.int32, (chunk, W_C), 0)
    a = [jnp.zeros((chunk, chunk), F32)] * H_C
    for lv in range(levels):
        m = 1 << lv
        x = (jnp.where((row & m) != 0, q, k) * jnp.exp2(_hgrn_level_exponent(b, g2, m, row))).astype(BF16)
        msk = msk_ref[lv]
        a = [a[h] + msk * _dot_nt(x[:, sl], x[:, sl]) for h, sl in enumerate(heads)]
    vb = v.astype(BF16)
    qe = (q * jnp.exp2(b)).astype(BF16)
    b_last = b[chunk - 1:chunk, :]
    kd = (k * jnp.exp2(b_last - b)).astype(BF16)
    decay = jnp.exp2(b_last)
    qk = q * k
    for h, sl in enumerate(heads):
        st = st_scr[h]
        o = _dot(a[h].astype(BF16), vb[:, sl]) + jnp.sum(qk[:, sl], axis=1, keepdims=True) * v[:, sl]
        o = o + _dot_nt(qe[:, sl], st.astype(BF16))
        st_scr[h] = st * decay[:, sl] + _dot_tn(vb[:, sl], kd[:, sl])
        o_ref[:, sl] = (_rms(o, gn_ref[...]) * sg_ref[:, sl].astype(F32)).astype(BF16)

    @pl.when(c == pl.num_programs(1) - 1)
    def _():
        for h in range(H_C):
            s_ref[h] = st_scr[h].T


def _prompt_hgrn(qs, lg, iv, sg, gn_g, batch, seq):
    chunk = HGRN_CHUNK
    tri, masks, levels = _hgrn_tables(chunk)
    a3 = [a.reshape(batch, seq, W_C) for a in (qs, lg, iv, sg)]
    tspec = pl.BlockSpec((None, chunk, W_C), lambda b, c: (b, c, 0))
    o, s = pl.pallas_call(
        functools.partial(_hgrn_kernel, chunk=chunk, levels=levels),
        grid=(batch, seq // chunk),
        in_specs=[tspec, tspec, tspec, tspec,
                  pl.BlockSpec(tri.shape, lambda b, c: (0, 0)),
                  pl.BlockSpec(masks.shape, lambda b, c: (0, 0, 0)),
                  pl.BlockSpec((1, DV_C), lambda b, c: (0, 0))],
        out_specs=[tspec, pl.BlockSpec((None, H_C, DK_C, DV_C), lambda b, c: (b, 0, 0, 0))],
        out_shape=[jax.ShapeDtypeStruct((batch, seq, W_C), BF16),
                   jax.ShapeDtypeStruct((batch, H_C, DK_C, DV_C), F32)],
        scratch_shapes=[pltpu.VMEM((H_C, DV_C, DK_C), F32)],
        compiler_params=_cparams("arbitrary", "arbitrary"),
        name="prompt_hgrn",
    )(*a3, tri, masks, gn_g.reshape(1, DV_C))
    return o.reshape(batch * seq, W_C), s


def _decode_attn_kernel(pt_ref, lam_ref, sg_ref, q_ref, kn_ref, vn_ref, gb_ref, *refs, n_pages, lam_init):
    k_pages = refs[:n_pages]
    v_pages = refs[n_pages:2 * n_pages]
    o_ref = refs[2 * n_pages]
    n_rows = 2 * H_B
    page_rows = H_B * PAGE_SIZE
    row = lax.broadcasted_iota(jnp.int32, (n_rows, LANES), 0)
    lane = lax.broadcasted_iota(jnp.int32, (n_rows, LANES), 1)

    def head_rows(pieces):
        out = jnp.zeros((n_rows, LANES), F32)
        for h in range(H_B):
            out = jnp.where(row // 2 == h, jnp.broadcast_to(pieces[h], (n_rows, LANES)), out)
        return out

    def by_row(ref):
        return [ref[h:h + 1, :] for h in range(H_B)]

    q = q_ref[...].astype(F32)
    qm32 = jnp.where(lane // HD_B == row % 2,
                     head_rows([q[:, h * LANES:(h + 1) * LANES] for h in range(H_B)]), 0.0)
    qm = qm32.astype(BF16)
    s = jnp.concatenate([_dot_nt(qm, kp[...].astype(BF16)) for kp in k_pages], axis=1)
    key_head = lax.broadcasted_iota(jnp.int32, s.shape, 1) % H_B
    s = jnp.where(key_head == lax.broadcasted_iota(jnp.int32, s.shape, 0) // 2, s, NEG_INF)
    s_new = jnp.sum(qm32 * head_rows(by_row(kn_ref)), axis=-1, keepdims=True)
    m = jnp.maximum(jnp.max(s, axis=-1, keepdims=True), s_new)
    p = jnp.exp2(s - m)
    p_new = jnp.exp2(s_new - m)
    l = jnp.sum(p, axis=-1, keepdims=True) + p_new
    comp = lax.broadcasted_iota(jnp.int32, (n_rows, 1), 0) % 2
    wgt = jnp.where(comp == 0, 1.0, -lam_ref[:, 0:1]) / l
    pw = (p * wgt).astype(BF16)
    o8 = (p_new * wgt) * head_rows(by_row(vn_ref))
    for j, vp in enumerate(v_pages):
        o8 = o8 + _dot(pw[:, j * page_rows:(j + 1) * page_rows], vp[...].astype(BF16))
    for h in range(H_B):
        sl = slice(h * DV_B, (h + 1) * DV_B)
        o = o8[2 * h:2 * h + 1, :] + o8[2 * h + 1:2 * h + 2, :]
        y = _rms(o, sg_ref[...]) * (1.0 - lam_init) * gb_ref[:, sl].astype(F32)
        o_ref[:, sl] = y.astype(BF16)


def _decode_attention(q, k_new, v_new, gb, cache_k4, cache_v4, layer, page_table, lam, subln_g, lam_init):
    n_seq, n_pages = page_table.shape
    row = pl.BlockSpec((None, 1, QK_W), lambda b, pt: (b, 0, 0))
    heads = pl.BlockSpec((None, None, H_B, LANES), lambda b, pt: (layer, b, 0, 0))
    vec = pl.BlockSpec((1, LANES), lambda b, pt: (0, 0))

    def page_spec(j):
        return pl.BlockSpec((None, None, H_B * PAGE_SIZE, LANES), lambda b, pt: (layer, pt[b, j], 0, 0))

    pages = [page_spec(j) for j in range(n_pages)]
    grid_spec = pltpu.PrefetchScalarGridSpec(
        num_scalar_prefetch=1,
        grid=(n_seq,),
        in_specs=[vec, vec, row, heads, heads, row] + pages + pages,
        out_specs=pl.BlockSpec((None, 1, W_B), lambda b, pt: (b, 0, 0)),
    )
    r3 = lambda a: a.reshape(n_seq, 1, QK_W)
    h3 = lambda a: a.reshape(N_AC, n_seq, H_B, LANES)
    out = pl.pallas_call(
        functools.partial(_decode_attn_kernel, n_pages=n_pages, lam_init=lam_init),
        grid_spec=grid_spec,
        out_shape=jax.ShapeDtypeStruct((n_seq, 1, W_B), BF16),
        compiler_params=_cparams("arbitrary"),
        name="decode_diff_attention",
    )(page_table, lam, subln_g.reshape(1, DV_B), r3(q), h3(k_new), h3(v_new), r3(gb),
      *([cache_k4] * n_pages), *([cache_v4] * n_pages))
    return out.reshape(n_seq, W_B)


def _decode_conv_kernel(st_ref, u_ref, ga_ref, w_ref, cb_ref, lg_ref, lb_ref, y_ref, ns_ref):
    st = st_ref[...]
    u = u_ref[...]
    w = w_ref[...]
    y = jnp.sum(st * w[None, :CONV_W - 1, :], axis=1) + u * w[CONV_W - 1:CONV_W, :] + cb_ref[...]
    y_ref[...] = _conv_post(y, lg_ref[...], lb_ref[...], ga_ref[...].astype(F32)).astype(BF16)
    ns_ref[:, 0:CONV_W - 2, :] = st[:, 1:CONV_W - 1, :]
    ns_ref[:, CONV_W - 2:CONV_W - 1, :] = u[:, None, :]


def _decode_conv(state_all, layer, u, ga, conv_w, conv_b, ln_g, ln_b):
    n = u.shape[0]
    tile = 32
    vec = pl.BlockSpec((1, W_A), lambda i: (0, 0))
    sspec = pl.BlockSpec((tile, CONV_W - 1, W_A), lambda i: (i, 0, 0))
    rspec = pl.BlockSpec((tile, W_A), lambda i: (i, 0))
    return pl.pallas_call(
        _decode_conv_kernel,
        grid=(n // tile,),
        in_specs=[pl.BlockSpec((None, tile, CONV_W - 1, W_A), lambda i: (layer, i, 0, 0)),
                  rspec, rspec, pl.BlockSpec((CONV_W, W_A), lambda i: (0, 0)), vec, vec, vec],
        out_specs=[rspec, sspec],
        out_shape=[jax.ShapeDtypeStruct((n, W_A), BF16),
                   jax.ShapeDtypeStruct((n, CONV_W - 1, W_A), F32)],
        compiler_params=_cparams("arbitrary"),
        name="decode_conv",
    )(state_all, u, ga, conv_w, conv_b.reshape(1, W_A), ln_g.reshape(1, W_A), ln_b.reshape(1, W_A))


def _decode_hgrn_kernel(q_ref, lg_ref, i_ref, sg_ref, gn_ref, s_ref, *rest, n_seq):
    o_ref, ns_ref = rest[-2:]

    def body(n, carry):
        ft = jnp.exp2(lg_ref[n]).T
        kt = 1.0 - ft
        qt = q_ref[n].T
        v8 = i_ref[n]
        rows = []
        for h in range(H_C):
            s_new = ft[:, h:h + 1] * s_ref[n, h] + kt[:, h:h + 1] * v8[h:h + 1, :]
            ns_ref[n, h] = s_new
            rows.append(jnp.sum(qt[:, h:h + 1] * s_new, axis=0, keepdims=True))
        o = jnp.concatenate(rows, axis=0)
        o_ref[n] = (_rms(o, gn_ref[...]) * sg_ref[n].astype(F32)).astype(BF16)
        return carry

    lax.fori_loop(0, n_seq, body, 0)


def _decode_hgrn(qs, lg, iv, sg, gn_g, state_all, layer, new_prev):
    n = qs.shape[0]
    tile = SAMPLE_STATE_TILE
    hspec = pl.BlockSpec((tile, H_C, DV_C), lambda i: (i, 0, 0))
    sspec = pl.BlockSpec((None, tile, H_C, DK_C, DV_C), lambda i: (layer, i, 0, 0, 0))
    h3 = lambda a: a.reshape(n, H_C, DV_C)
    in_specs = [hspec, hspec, hspec, hspec, pl.BlockSpec((1, DV_C), lambda i: (0, 0)), sspec]
    args = [h3(qs), h3(lg), h3(iv), h3(sg), gn_g.reshape(1, DV_C), state_all]
    aliases = {}
    if new_prev is not None:
        aliases = {len(args): 1}
        in_specs.append(pl.BlockSpec(memory_space=pl.ANY))
        args.append(new_prev)
    o, ns = pl.pallas_call(
        functools.partial(_decode_hgrn_kernel, n_seq=tile),
        grid=(n // tile,),
        in_specs=in_specs,
        out_specs=[hspec, sspec],
        out_shape=[jax.ShapeDtypeStruct((n, H_C, DV_C), BF16),
                   jax.ShapeDtypeStruct((N_C, n, H_C, DK_C, DV_C), F32)],
        input_output_aliases=aliases,
        compiler_params=_cparams("arbitrary"),
        name="decode_hgrn",
    )(*args)
    return o.reshape(n, W_C), ns


def _rope_tables(pos, rows):
    half = ROT_DIM // 2
    inv_freq = ROPE_THETA ** (-jnp.arange(half, dtype=F32) / half)
    ang = pos.astype(F32)[:, None] * inv_freq[None, :]
    cos, sin = jnp.cos(ang), jnp.sin(ang)
    n = pos.shape[0]
    ones = jnp.ones((n, HD_B - ROT_DIM), F32)
    zeros = jnp.zeros((n, HD_B - ROT_DIM), F32)
    z8 = jnp.zeros((n, half), F32)
    cos_t = jnp.concatenate([cos, cos, ones], axis=1)
    sa_t = jnp.concatenate([-sin, z8, zeros], axis=1)
    sb_t = jnp.concatenate([z8, sin, zeros], axis=1)
    out = []
    for t in (cos_t, sa_t, sb_t):
        t = jnp.concatenate([t, t], axis=1)
        out.append(jnp.broadcast_to(t, (rows, LANES)) if n == 1 else t)
    return out


def kernel(x_prompt, x_sample, c_prompt, c_sample, cache_k, cache_v, page_table, state_conv, state_hgrn,
           norm_g, ada_w, ada_b, w_in_ac, w_out_ac, conv_w, conv_b, ln_g, ln_b, lam_q1, lam_k1, lam_q2,
           lam_k2, subln_g, w_in_c, w_out_c, gn_g, lb_logits, final_g):
    bp, tp = x_prompt.shape[:2]
    bs, ts = x_sample.shape[:2]
    assert ts == 1
    n_pool = cache_k.shape[1]
    n_past = page_table.shape[1] * PAGE_SIZE
    mp = bp * tp

    lb_sm = jax.nn.softmax(lb_logits.astype(F32), axis=0)
    lb_all = jnp.cumsum(lb_sm, axis=0) - lb_sm[0]

    mod = _ada_modulation(jnp.concatenate([c_prompt, c_sample], axis=0), ada_w, ada_b)

    def mods(l):
        out = []
        for j in range(3):
            m = mod[l, :, j * D_MODEL:(j + 1) * D_MODEL]
            out.append((m[:bp].reshape(bp, 1, D_MODEL), m[bp:].reshape(1, bs, D_MODEL)))
        return out

    tabs_p = _rope_tables(jnp.arange(tp), tp)
    tabs_s = _rope_tables(jnp.full((1,), n_past), bs)
    ck4 = cache_k.reshape(N_AC, n_pool, PAGE_SIZE * H_B, 2 * HD_B)
    cv4 = cache_v.reshape(N_AC, n_pool, PAGE_SIZE * H_B, DV_B)

    tiles_p = tp // ROW_TILE
    hp = x_prompt.reshape(mp, D_MODEL)
    hs = x_sample.reshape(bs, D_MODEL)
    cp_l, sp_l, cs_l = [], [], []
    kv_p = kv_s = ss = None
    for l in range(DEPTH):
        (sh_p, sh_s), (sc_p, sc_s), (gt_p, gt_s) = mods(l)
        last = final_g if l == DEPTH - 1 else None
        if l % 2 == 0:
            a = l // 2
            lam_init = 0.8 - 0.6 * math.exp(-0.3 * l)
            lam = (jnp.exp(jnp.sum(lam_q1[a].astype(F32) * lam_k1[a].astype(F32)))
                   - jnp.exp(jnp.sum(lam_q2[a].astype(F32) * lam_k2[a].astype(F32))) + lam_init)
            lam = jnp.full((1, LANES), lam, F32)
            w_in = w_in_ac[a].astype(BF16)
            w_out = w_out_ac[a].astype(BF16)
            q, k, kb, v, vb, gb, u, ga = _inproj_ac(hp, norm_g[l], sc_p, sh_p, w_in, *tabs_p,
                                                    ROW_TILE, tiles_p, a, kv_p)
            kv_p = (k, v)
            ob = _prompt_attention(q, kb, vb, gb, lam, subln_g[a], lam_init, bp, tp)
            y, cst = _prompt_conv(u, ga, conv_w[a], conv_b[a], ln_g[a], ln_b[a], bp, tp)
            hp = _outproj([ob, y], w_out, hp, gt_p, ROW_TILE, tiles_p, last)
            cp_l.append(cst)
            q, k, kb, v, vb, gb, u, ga = _inproj_ac(hs, norm_g[l], sc_s, sh_s, w_in, *tabs_s, bs, 1, a, kv_s)
            kv_s = (k, v)
            ob = _decode_attention(q, k, v, gb, ck4, cv4, a, page_table, lam, subln_g[a], lam_init)
            y, cst = _decode_conv(state_conv, a, u, ga, conv_w[a], conv_b[a], ln_g[a], ln_b[a])
            hs = _outproj([ob, y], w_out, hs, gt_s, bs, 1, last)
            cs_l.append(cst)
        else:
            ci = l // 2
            w_in = w_in_c[ci].astype(BF16)
            w_out = w_out_c[ci].astype(BF16)
            qs, lg, iv, sg = _inproj_c(hp, norm_g[l], sc_p, sh_p, w_in, lb_all[l], ROW_TILE, tiles_p)
            o, st = _prompt_hgrn(qs, lg, iv, sg, gn_g[ci], bp, tp)
            hp = _outproj([o], w_out, hp, gt_p, ROW_TILE, tiles_p, last)
            sp_l.append(st)
            qs, lg, iv, sg = _inproj_c(hs, norm_g[l], sc_s, sh_s, w_in, lb_all[l], bs, 1)
            o, ss = _decode_hgrn(qs, lg, iv, sg, gn_g[ci], state_hgrn, ci, ss)
            hs = _outproj([o], w_out, hs, gt_s, bs, 1, last)

    y_prompt = hp.reshape(bp, tp, D_MODEL)
    y_sample = hs.reshape(bs, ts, D_MODEL)
    return (y_prompt, y_sample,
            kv_p[0].reshape(N_AC, bp, tp, H_B, 2 * HD_B), kv_p[1].reshape(N_AC, bp, tp, H_B, DV_B),
            jnp.stack(cp_l), jnp.stack(sp_l),
            kv_s[0].reshape(N_AC, bs, ts, H_B, 2 * HD_B), kv_s[1].reshape(N_AC, bs, ts, H_B, DV_B),
            jnp.stack(cs_l), ss)
```

```python
import functools
import math

import numpy as np
import jax
import jax.numpy as jnp
from jax import lax
from jax.experimental import pallas as pl
from jax.experimental.pallas import tpu as pltpu

F32 = jnp.float32
BF16 = jnp.bfloat16

D_MODEL = 1024
DEPTH = 4
PAGE_SIZE = 128
N_AC = (DEPTH + 1) // 2
N_C = DEPTH // 2

H_B = 4
HD_B = 64
DV_B = 2 * HD_B
W_B = H_B * DV_B
QK_W = H_B * 2 * HD_B
ROT_DIM = HD_B // 4
ROPE_THETA = 500000.0
NEG_INF = -1e30

W_A = D_MODEL // 2
CONV_W = 31

H_C = 8
DK_C = D_MODEL // H_C
DV_C = D_MODEL // H_C
W_C = H_C * DV_C

EPS = 1e-6
LOG2_E = math.log2(math.e)
Q_SCALE = HD_B ** -0.5 * LOG2_E

AC_IN = 2 * QK_W + 2 * W_B + 3 * W_A
C_IN = 2 * H_C * DK_C + 2 * W_C

LANES = 128
SUBLANES = 8
VMEM_LIMIT_BYTES = 56 * 1024 * 1024

ROW_TILE = 512
ATTN_TILE = 256
CONV_TILE = 512
CONV_HALO = 32
HGRN_CHUNK = 128
SAMPLE_STATE_TILE = 8


def _cparams(*sem):
    return pltpu.CompilerParams(dimension_semantics=sem, vmem_limit_bytes=VMEM_LIMIT_BYTES)


def _silu(x):
    return x * jax.nn.sigmoid(x)


def _dot(a, b):
    return jnp.dot(a, b, preferred_element_type=F32)


def _dot_nt(a, b):
    return lax.dot_general(a, b, (((1,), (1,)), ((), ())), preferred_element_type=F32)


def _dot_tn(a, b):
    return lax.dot_general(a, b, (((0,), (0,)), ((), ())), preferred_element_type=F32)


def _rms(x, g):
    return x * lax.rsqrt(jnp.mean(x * x, axis=-1, keepdims=True) + EPS) * g


def _ada_kernel(c_ref, w_ref, b_ref, o_ref):
    s = _silu(c_ref[...]).astype(BF16)
    o_ref[...] = _dot(s, w_ref[...].astype(BF16)) + b_ref[...]


def _ada_modulation(c_all, ada_w, ada_b):
    rows = c_all.shape[0]
    return pl.pallas_call(
        _ada_kernel,
        grid=(DEPTH, 3),
        in_specs=[
            pl.BlockSpec((rows, D_MODEL), lambda l, j: (0, 0)),
            pl.BlockSpec((None, D_MODEL, D_MODEL), lambda l, j: (l, 0, j)),
            pl.BlockSpec((None, 1, D_MODEL), lambda l, j: (l, 0, j)),
        ],
        out_specs=pl.BlockSpec((None, rows, D_MODEL), lambda l, j: (l, 0, j)),
        out_shape=jax.ShapeDtypeStruct((DEPTH, rows, 3 * D_MODEL), F32),
        compiler_params=_cparams("arbitrary", "arbitrary"),
        name="ada_modulation",
    )(c_all, ada_w, ada_b.reshape(DEPTH, 1, 3 * D_MODEL))


def _modulated(x_ref, ng_ref, sc_ref, sh_ref):
    h = _rms(x_ref[...], ng_ref[...])
    return (h * (1.0 + sc_ref[...]) + sh_ref[...]).astype(BF16)


def _inproj_ac_kernel(x_ref, ng_ref, sc_ref, sh_ref, w_ref, cos_ref, sa_ref, sb_ref, *rest):
    q_ref, k_ref, kb_ref, v_ref, vb_ref, gb_ref, u_ref, ga_ref, h_scr = rest[-9:]
    h_scr[...] = _modulated(x_ref, ng_ref, sc_ref, sh_ref)

    def proj(s):
        return _dot(h_scr[...], w_ref[:, s * QK_W:(s + 1) * QK_W])

    cos, sa, sb = cos_ref[...], sa_ref[...], sb_ref[...]

    def rope(z):
        return (z * cos + pltpu.roll(z, LANES - ROT_DIM // 2, 1) * sa
                + pltpu.roll(z, ROT_DIM // 2, 1) * sb)

    zq = proj(0)
    for g in range(H_B):
        sl = slice(g * LANES, (g + 1) * LANES)
        q_ref[:, sl] = (rope(zq[:, sl]) * Q_SCALE).astype(BF16)
    tm = x_ref.shape[0]
    zk = proj(1)
    for g in range(H_B):
        sl = slice(g * LANES, (g + 1) * LANES)
        r = rope(zk[:, sl])
        k_ref[pl.ds(g, tm, stride=H_B), :] = r
        kb_ref[:, sl] = r.astype(BF16)
    zv = proj(2)
    for g in range(H_B):
        v_ref[pl.ds(g, tm, stride=H_B), :] = zv[:, g * LANES:(g + 1) * LANES]
    vb_ref[...] = zv.astype(BF16)
    gb_ref[...] = _silu(proj(3)).astype(BF16)
    a_val = proj(4)
    u_ref[...] = a_val * jax.nn.sigmoid(proj(5))
    ga_ref[...] = _silu(proj(6)).astype(BF16)


def _mod_specs(rows_per_mod, tm, tiles_per_mod):
    r = rows_per_mod
    return pl.BlockSpec((None, r, D_MODEL), lambda i: (i // tiles_per_mod, 0, 0))


def _inproj_ac(x, norm_g, scale, shift, w, cos_t, sa_t, sb_t, tm, tiles_per_mod, layer, kv_prev):
    m = x.shape[0]
    tab_tiles = cos_t.shape[0] // tm
    mod_spec = _mod_specs(scale.shape[1], tm, tiles_per_mod)
    row512 = pl.BlockSpec((tm, QK_W), lambda i: (i, 0))
    heads = pl.BlockSpec((None, H_B * tm, LANES), lambda i: (layer, i, 0))
    tab_spec = pl.BlockSpec((tm, LANES), lambda i: (i % tab_tiles, 0))
    out_shape = [
        jax.ShapeDtypeStruct((m, QK_W), BF16),
        jax.ShapeDtypeStruct((N_AC, H_B * m, LANES), F32),
        jax.ShapeDtypeStruct((m, QK_W), BF16),
        jax.ShapeDtypeStruct((N_AC, H_B * m, LANES), F32),
        jax.ShapeDtypeStruct((m, W_B), BF16),
        jax.ShapeDtypeStruct((m, W_B), BF16),
        jax.ShapeDtypeStruct((m, W_A), F32),
        jax.ShapeDtypeStruct((m, W_A), BF16),
    ]
    in_specs = [
        pl.BlockSpec((tm, D_MODEL), lambda i: (i, 0)),
        pl.BlockSpec((1, D_MODEL), lambda i: (0, 0)),
        mod_spec, mod_spec,
        pl.BlockSpec((D_MODEL, AC_IN), lambda i: (0, 0)),
        tab_spec, tab_spec, tab_spec,
    ]
    args = [x, norm_g.reshape(1, D_MODEL), scale, shift, w, cos_t, sa_t, sb_t]
    aliases = {}
    if kv_prev is not None:
        aliases = {len(args): 1, len(args) + 1: 3}
        in_specs += [pl.BlockSpec(memory_space=pl.ANY)] * 2
        args += list(kv_prev)
    return pl.pallas_call(
        _inproj_ac_kernel,
        grid=(m // tm,),
        in_specs=in_specs,
        out_specs=[row512, heads, row512, heads, row512, row512, row512, row512],
        out_shape=out_shape,
        input_output_aliases=aliases,
        scratch_shapes=[pltpu.VMEM((tm, D_MODEL), BF16)],
        compiler_params=_cparams("arbitrary"),
        name="inproj_conv_attn",
    )(*args)


def _inproj_c_kernel(x_ref, ng_ref, sc_ref, sh_ref, w_ref, lb_ref, q_ref, lg_ref, i_ref, sg_ref, h_scr):
    h_scr[...] = _modulated(x_ref, ng_ref, sc_ref, sh_ref)

    def proj(s):
        return _dot(h_scr[...], w_ref[:, s * W_C:(s + 1) * W_C])

    q_ref[...] = _silu(proj(0))
    lb = lb_ref[...]
    lg_ref[...] = jnp.log(lb + (1.0 - lb) * jax.nn.sigmoid(proj(1))) * LOG2_E
    i_ref[...] = proj(2)
    sg_ref[...] = _silu(proj(3)).astype(BF16)


def _inproj_c(x, norm_g, scale, shift, w, lb, tm, tiles_per_mod):
    m = x.shape[0]
    mod_spec = _mod_specs(scale.shape[1], tm, tiles_per_mod)
    row = pl.BlockSpec((tm, W_C), lambda i: (i, 0))
    out_shape = [
        jax.ShapeDtypeStruct((m, W_C), F32),
        jax.ShapeDtypeStruct((m, W_C), F32),
        jax.ShapeDtypeStruct((m, W_C), F32),
        jax.ShapeDtypeStruct((m, W_C), BF16),
    ]
    return pl.pallas_call(
        _inproj_c_kernel,
        grid=(m // tm,),
        in_specs=[
            pl.BlockSpec((tm, D_MODEL), lambda i: (i, 0)),
            pl.BlockSpec((1, D_MODEL), lambda i: (0, 0)),
            mod_spec, mod_spec,
            pl.BlockSpec((D_MODEL, C_IN), lambda i: (0, 0)),
            pl.BlockSpec((1, W_C), lambda i: (0, 0)),
        ],
        out_specs=[row] * 4,
        out_shape=out_shape,
        scratch_shapes=[pltpu.VMEM((tm, D_MODEL), BF16)],
        compiler_params=_cparams("arbitrary"),
        name="inproj_hgrn",
    )(x, norm_g.reshape(1, D_MODEL), scale, shift, w, lb.reshape(1, W_C))


def _outproj_kernel(*refs, n_act, final_norm):
    acts = refs[:n_act]
    w_ref, x_ref, gate_ref = refs[n_act:n_act + 3]
    o_ref = refs[-1]
    acc = None
    lo = 0
    for a_ref in acts:
        width = a_ref.shape[-1]
        part = _dot(a_ref[...], w_ref[lo:lo + width, :])
        acc = part if acc is None else acc + part
        lo += width
    y = x_ref[...] + gate_ref[...] * acc
    if final_norm:
        y = _rms(y, refs[n_act + 3][...])
    o_ref[...] = y


def _outproj(acts, w, x, gate, tm, tiles_per_mod, final_g=None):
    m = x.shape[0]
    mod_spec = _mod_specs(gate.shape[1], tm, tiles_per_mod)
    in_specs = [pl.BlockSpec((tm, a.shape[1]), lambda i: (i, 0)) for a in acts]
    in_specs += [
        pl.BlockSpec((D_MODEL, D_MODEL), lambda i: (0, 0)),
        pl.BlockSpec((tm, D_MODEL), lambda i: (i, 0)),
        mod_spec,
    ]
    args = list(acts) + [w, x, gate]
    if final_g is not None:
        in_specs.append(pl.BlockSpec((1, D_MODEL), lambda i: (0, 0)))
        args.append(final_g.reshape(1, D_MODEL))
    return pl.pallas_call(
        functools.partial(_outproj_kernel, n_act=len(acts), final_norm=final_g is not None),
        grid=(m // tm,),
        in_specs=in_specs,
        out_specs=pl.BlockSpec((tm, D_MODEL), lambda i: (i, 0)),
        out_shape=jax.ShapeDtypeStruct((m, D_MODEL), F32),
        compiler_params=_cparams("arbitrary"),
        name="outproj",
    )(*args)


def _attn_kernel(pt_ref, lam_ref, sgc_ref, sgr_ref, q_ref, k_ref, v_ref, gb_ref,
                 qs_ref, kn_ref, vn_ref, gbs_ref, *rest, tile, lam_init, n_dec, n_pages):
    pages = rest[:2 * n_dec * n_pages]
    o_ref, os_ref, q2_scr, m_scr, l_scr, acc_scr = rest[2 * n_dec * n_pages:]
    sg_ref = sgc_ref
    qi = pl.program_id(1)
    heads = [slice(h * LANES, (h + 1) * LANES) for h in range(H_B)]
    lane = lax.broadcasted_iota(jnp.int32, (tile, LANES), 1)
    for h, sl in enumerate(heads):
        q = q_ref[:, sl]
        zero = jnp.zeros_like(q)
        q2_scr[h] = jnp.concatenate([jnp.where(lane < HD_B, q, zero), jnp.where(lane >= HD_B, q, zero)], axis=0)
    m_scr[...] = jnp.full(m_scr.shape, NEG_INF, F32)
    l_scr[...] = jnp.zeros(l_scr.shape, F32)
    acc_scr[...] = jnp.zeros(acc_scr.shape, F32)

    def step(start, mask):
        s = [_dot_nt(k_ref[pl.ds(start, tile), sl], q2_scr[h]) for h, sl in enumerate(heads)]
        for h, sl in enumerate(heads):
            sh = s[h] if mask is None else jnp.where(mask, s[h], NEG_INF)
            m_old = m_scr[h]
            m_new = jnp.maximum(m_old, jnp.max(sh, axis=0, keepdims=True))
            alpha = jnp.exp2(m_old - m_new)
            p = jnp.exp2(sh - m_new)
            m_scr[h] = m_new
            l_scr[h] = alpha * l_scr[h] + jnp.sum(p, axis=0, keepdims=True)
            acc_scr[h] = alpha * acc_scr[h] + _dot_tn(v_ref[pl.ds(start, tile), sl], p.astype(BF16))

    def body(j, carry):
        step(pl.multiple_of(j * tile, tile), None)
        return carry

    lax.fori_loop(0, qi, body, 0)
    lam = lam_ref[:, 0:1]
    dec = []
    for r in range(n_dec):
        k_pages = pages[2 * r * n_pages:(2 * r + 1) * n_pages]
        v_pages = pages[(2 * r + 1) * n_pages:(2 * r + 2) * n_pages]
        dec.append(_decode_scores(qs_ref[r].astype(F32), kn_ref[r], k_pages) + (v_pages,))
    key = lax.broadcasted_iota(jnp.int32, (tile, 2 * tile), 0)
    qry = lax.broadcasted_iota(jnp.int32, (tile, 2 * tile), 1)
    step(pl.multiple_of(qi * tile, tile), key <= jnp.where(qry >= tile, qry - tile, qry))
    for r, (s, s_new, v_pages) in enumerate(dec):
        os_ref[r] = _decode_finish(s, s_new, lam, vn_ref[r], v_pages, sgr_ref[...],
                                   gbs_ref[r].astype(F32), lam_init).astype(BF16)
    for h, sl in enumerate(heads):
        l, acc = l_scr[h], acc_scr[h]
        o = acc[:, :tile] / l[:, :tile] - lam_ref[:, 0:1] * (acc[:, tile:] / l[:, tile:])
        y = o * lax.rsqrt(jnp.mean(o * o, axis=0, keepdims=True) + EPS) * sg_ref[...]
        o_ref[:, sl] = (y.T * (1.0 - lam_init) * gb_ref[:, sl].astype(F32)).astype(BF16)


def _attention(q, kb, vb, gb, batch, seq, qs, kn_all, vn_all, gbs, cache_k4, cache_v4, layer, page_table,
               lam, subln_g, lam_init):
    tile = ATTN_TILE
    nq = seq // tile
    n_seq, n_pages = page_table.shape
    n_dec = n_seq // (batch * nq)
    assert n_dec * batch * nq == n_seq
    q3, k3, v3, g3 = (a.reshape(batch, seq, QK_W) for a in (q, kb, vb, gb))
    qspec = pl.BlockSpec((None, tile, QK_W), lambda b, i, pt: (b, i, 0))
    kvspec = pl.BlockSpec((None, seq, QK_W), lambda b, i, pt: (b, 0, 0))
    rows = pl.BlockSpec((n_dec, 1, QK_W), lambda b, i, pt: (b * nq + i, 0, 0))
    new_kv = pl.BlockSpec((None, n_dec, H_B, LANES), lambda b, i, pt: (layer, b * nq + i, 0, 0))

    def page_spec(r, j):
        return pl.BlockSpec((None, None, H_B * PAGE_SIZE, LANES),
                            lambda b, i, pt: (layer, pt[(b * nq + i) * n_dec + r, j], 0, 0))

    page_specs, page_args = [], []
    for r in range(n_dec):
        for cache in (cache_k4, cache_v4):
            page_specs += [page_spec(r, j) for j in range(n_pages)]
            page_args += [cache] * n_pages
    grid_spec = pltpu.PrefetchScalarGridSpec(
        num_scalar_prefetch=1,
        grid=(batch, nq),
        in_specs=[pl.BlockSpec((1, LANES), lambda b, i, pt: (0, 0)),
                  pl.BlockSpec((DV_B, 1), lambda b, i, pt: (0, 0)),
                  pl.BlockSpec((1, DV_B), lambda b, i, pt: (0, 0)),
                  qspec, kvspec, kvspec, qspec, rows, new_kv, new_kv, rows] + page_specs,
        out_specs=[qspec, rows],
        scratch_shapes=[pltpu.VMEM((H_B, 2 * tile, LANES), BF16),
                        pltpu.VMEM((H_B, 1, 2 * tile), F32),
                        pltpu.VMEM((H_B, 1, 2 * tile), F32),
                        pltpu.VMEM((H_B, DV_B, 2 * tile), F32)],
    )
    r3 = lambda a: a.reshape(n_seq, 1, QK_W)
    h4 = lambda a: a.reshape(N_AC, n_seq, H_B, LANES)
    out, out_s = pl.pallas_call(
        functools.partial(_attn_kernel, tile=tile, lam_init=lam_init, n_dec=n_dec, n_pages=n_pages),
        grid_spec=grid_spec,
        out_shape=[jax.ShapeDtypeStruct((batch, seq, W_B), BF16),
                   jax.ShapeDtypeStruct((n_seq, 1, W_B), BF16)],
        compiler_params=_cparams("arbitrary", "arbitrary"),
        name="diff_attention",
    )(page_table, lam, subln_g.reshape(DV_B, 1), subln_g.reshape(1, DV_B), q3, k3, v3, g3,
      r3(qs), h4(kn_all), h4(vn_all), r3(gbs), *page_args)
    return out.reshape(batch * seq, W_B), out_s.reshape(n_seq, W_B)


def _conv_post(y, ln_g, ln_b, ga):
    yc = y - jnp.mean(y, axis=-1, keepdims=True)
    yn = yc * lax.rsqrt(jnp.mean(yc * yc, axis=-1, keepdims=True) + EPS) * ln_g + ln_b
    return _silu(yn) * ga


def _conv_kernel(u_ref, ga_ref, w_ref, cb_ref, lg_ref, lb_ref, y_ref, st_ref, ext_scr, sh_scr, *, tile, rows):
    t = pl.program_id(1)
    first = CONV_HALO - (CONV_W - 1)
    sh_rows = sh_scr.shape[1]

    @pl.when(t == 0)
    def _():
        ext_scr[0:CONV_HALO, :] = jnp.zeros((CONV_HALO, W_A), F32)

    @pl.when(t > 0)
    def _():
        ext_scr[0:CONV_HALO, :] = ext_scr[tile:tile + CONV_HALO, :]

    ext_scr[CONV_HALO:CONV_HALO + tile, :] = u_ref[...]
    for r in range(1, SUBLANES):
        sh_scr[r - 1] = ext_scr[r:r + sh_rows, :]
    for r0 in range(0, tile, rows):
        acc = jnp.zeros((rows, W_A), F32) + cb_ref[...]
        for j in range(CONV_W):
            off = first + j
            r, base = off % SUBLANES, off - off % SUBLANES + r0
            src = ext_scr[base:base + rows, :] if r == 0 else sh_scr[r - 1, base:base + rows, :]
            acc = acc + src * w_ref[j:j + 1, :]
        y_ref[r0:r0 + rows, :] = _conv_post(acc, lg_ref[...], lb_ref[...],
                                            ga_ref[r0:r0 + rows, :].astype(F32)).astype(BF16)

    @pl.when(t == pl.num_programs(1) - 1)
    def _():
        st_ref[...] = ext_scr[CONV_HALO + tile - (CONV_W - 1):CONV_HALO + tile, :]


def _prompt_conv(u, ga, conv_w, conv_b, ln_g, ln_b, batch, seq):
    tile = CONV_TILE
    u3, g3 = u.reshape(batch, seq, W_A), ga.reshape(batch, seq, W_A)
    tspec = pl.BlockSpec((None, tile, W_A), lambda b, t: (b, t, 0))
    vec = pl.BlockSpec((1, W_A), lambda b, t: (0, 0))
    y, st = pl.pallas_call(
        functools.partial(_conv_kernel, tile=tile, rows=64),
        grid=(batch, seq // tile),
        in_specs=[tspec, tspec, pl.BlockSpec((CONV_W, W_A), lambda b, t: (0, 0)), vec, vec, vec],
        out_specs=[tspec, pl.BlockSpec((None, CONV_W - 1, W_A), lambda b, t: (b, 0, 0))],
        out_shape=[jax.ShapeDtypeStruct((batch, seq, W_A), BF16),
                   jax.ShapeDtypeStruct((batch, CONV_W - 1, W_A), F32)],
        scratch_shapes=[pltpu.VMEM((CONV_HALO + tile, W_A), F32),
                        pltpu.VMEM((SUBLANES - 1, CONV_HALO + tile - SUBLANES, W_A), F32)],
        compiler_params=_cparams("arbitrary", "arbitrary"),
        name="prompt_conv",
    )(u3, g3, conv_w, conv_b.reshape(1, W_A), ln_g.reshape(1, W_A), ln_b.reshape(1, W_A))
    return y.reshape(batch * seq, W_A), st


def _hgrn_tables(chunk):
    levels = int(math.log2(chunk))
    t = np.arange(chunk)[:, None]
    u = np.arange(chunk)[None, :]
    masks = []
    for v in range(levels):
        m = 1 << v
        masks.append((t // (2 * m) == u // (2 * m)) & (t % (2 * m) >= m) & (u % (2 * m) < m))
    return jnp.asarray(u <= t, BF16), jnp.asarray(np.stack(masks), F32), levels


def _split3(x):
    hi = x.astype(BF16)
    r = x - hi.astype(F32)
    mid = r.astype(BF16)
    lo = (r - mid.astype(F32)).astype(BF16)
    return hi, mid, lo


def _hgrn_level_exponent(b, g2, m, row):
    chunk, width = b.shape
    if m == 1:
        return jnp.where((row & 1) != 0, g2, 0.0)
    if 2 * m < SUBLANES:
        b3 = b.reshape(chunk // SUBLANES, SUBLANES, width)
        sub = lax.broadcasted_iota(jnp.int32, b3.shape, 1)
        bm = b3[:, m - 1:m, :]
        for blk in range(1, SUBLANES // (2 * m)):
            bm = jnp.where(sub < blk * 2 * m, bm, b3[:, blk * 2 * m + m - 1:blk * 2 * m + m, :])
    else:
        b3 = b.reshape(chunk // (2 * m), 2 * m, width)
        bm = b3[:, m - 1:m, :]
    return -jnp.abs((b3 - bm).reshape(chunk, width))


def _hgrn_kernel(q_ref, lg_ref, i_ref, sg_ref, tri_ref, msk_ref, gn_ref, o_ref, s_ref, st_scr,
                 *, chunk, levels):
    c = pl.program_id(1)

    @pl.when(c == 0)
    def _():
        st_scr[...] = jnp.zeros_like(st_scr)

    heads = [slice(h * LANES, (h + 1) * LANES) for h in range(H_C)]
    g2 = lg_ref[...]
    q = q_ref[...]
    v = i_ref[...]
    k = 1.0 - jnp.exp2(g2)
    b3 = _dot(tri_ref[...], jnp.concatenate(_split3(g2), axis=1))
    b = b3[:, :W_C] + b3[:, W_C:2 * W_C] + b3[:, 2 * W_C:]
    row = lax.broadcasted_iota(jnp.int32, (chunk, W_C), 0)
    a = [jnp.zeros((chunk, chunk), F32)] * H_C
    for lv in range(levels):
        m = 1 << lv
        x = (jnp.where((row & m) != 0, q, k) * jnp.exp2(_hgrn_level_exponent(b, g2, m, row))).astype(BF16)
        msk = msk_ref[lv]
        a = [a[h] + msk * _dot_nt(x[:, sl], x[:, sl]) for h, sl in enumerate(heads)]
    vb = v.astype(BF16)
    qe = (q * jnp.exp2(b)).astype(BF16)
    b_last = b[chunk - 1:chunk, :]
    kd = (k * jnp.exp2(b_last - b)).astype(BF16)
    decay = jnp.exp2(b_last)
    qk = q * k
    for h, sl in enumerate(heads):
        st = st_scr[h]
        o = _dot(a[h].astype(BF16), vb[:, sl]) + jnp.sum(qk[:, sl], axis=1, keepdims=True) * v[:, sl]
        o = o + _dot_nt(qe[:, sl], st.astype(BF16))
        st_scr[h] = st * decay[:, sl] + _dot_tn(vb[:, sl], kd[:, sl])
        o_ref[:, sl] = (_rms(o, gn_ref[...]) * sg_ref[:, sl].astype(F32)).astype(BF16)

    @pl.when(c == pl.num_programs(1) - 1)
    def _():
        for h in range(H_C):
            s_ref[h] = st_scr[h].T


def _prompt_hgrn(qs, lg, iv, sg, gn_g, batch, seq):
    chunk = HGRN_CHUNK
    tri, masks, levels = _hgrn_tables(chunk)
    a3 = [a.reshape(batch, seq, W_C) for a in (qs, lg, iv, sg)]
    tspec = pl.BlockSpec((None, chunk, W_C), lambda b, c: (b, c, 0))
    o, s = pl.pallas_call(
        functools.partial(_hgrn_kernel, chunk=chunk, levels=levels),
        grid=(batch, seq // chunk),
        in_specs=[tspec, tspec, tspec, tspec,
                  pl.BlockSpec(tri.shape, lambda b, c: (0, 0)),
                  pl.BlockSpec(masks.shape, lambda b, c: (0, 0, 0)),
                  pl.BlockSpec((1, DV_C), lambda b, c: (0, 0))],
        out_specs=[tspec, pl.BlockSpec((None, H_C, DK_C, DV_C), lambda b, c: (b, 0, 0, 0))],
        out_shape=[jax.ShapeDtypeStruct((batch, seq, W_C), BF16),
                   jax.ShapeDtypeStruct((batch, H_C, DK_C, DV_C), F32)],
        scratch_shapes=[pltpu.VMEM((H_C, DV_C, DK_C), F32)],
        compiler_params=_cparams("arbitrary", "arbitrary"),
        name="prompt_hgrn",
    )(*a3, tri, masks, gn_g.reshape(1, DV_C))
    return o.reshape(batch * seq, W_C), s


_DEC_ROWS = 2 * H_B


def _decode_head_rows(pieces):
    row = lax.broadcasted_iota(jnp.int32, (_DEC_ROWS, LANES), 0)
    out = jnp.zeros((_DEC_ROWS, LANES), F32)
    for h in range(H_B):
        out = jnp.where(row // 2 == h, jnp.broadcast_to(pieces[h], (_DEC_ROWS, LANES)), out)
    return out


def _decode_scores(q, kn, k_pages):
    row = lax.broadcasted_iota(jnp.int32, (_DEC_ROWS, LANES), 0)
    lane = lax.broadcasted_iota(jnp.int32, (_DEC_ROWS, LANES), 1)
    qm32 = jnp.where(lane // HD_B == row % 2,
                     _decode_head_rows([q[:, h * LANES:(h + 1) * LANES] for h in range(H_B)]), 0.0)
    qm = qm32.astype(BF16)
    s = jnp.concatenate([_dot_nt(qm, kp[...].astype(BF16)) for kp in k_pages], axis=1)
    key_head = lax.broadcasted_iota(jnp.int32, s.shape, 1) % H_B
    s = jnp.where(key_head == lax.broadcasted_iota(jnp.int32, s.shape, 0) // 2, s, NEG_INF)
    s_new = jnp.sum(qm32 * _decode_head_rows([kn[h:h + 1, :] for h in range(H_B)]), axis=-1, keepdims=True)
    return s, s_new


def _decode_finish(s, s_new, lam, vn, v_pages, subln_row, gb, lam_init):
    page_rows = H_B * PAGE_SIZE
    m = jnp.maximum(jnp.max(s, axis=-1, keepdims=True), s_new)
    p = jnp.exp2(s - m)
    p_new = jnp.exp2(s_new - m)
    l = jnp.sum(p, axis=-1, keepdims=True) + p_new
    comp = lax.broadcasted_iota(jnp.int32, (_DEC_ROWS, 1), 0) % 2
    wgt = jnp.where(comp == 0, 1.0, -lam) / l
    pw = (p * wgt).astype(BF16)
    o8 = (p_new * wgt) * _decode_head_rows([vn[h:h + 1, :] for h in range(H_B)])
    for j, vp in enumerate(v_pages):
        o8 = o8 + _dot(pw[:, j * page_rows:(j + 1) * page_rows], vp[...].astype(BF16))
    out = []
    for h in range(H_B):
        o = o8[2 * h:2 * h + 1, :] + o8[2 * h + 1:2 * h + 2, :]
        out.append(_rms(o, subln_row) * (1.0 - lam_init) * gb[:, h * DV_B:(h + 1) * DV_B])
    return jnp.concatenate(out, axis=1)


def _decode_conv_kernel(st_ref, u_ref, ga_ref, w_ref, cb_ref, lg_ref, lb_ref, y_ref, ns_ref):
    st = st_ref[...]
    u = u_ref[...]
    w = w_ref[...]
    y = jnp.sum(st * w[None, :CONV_W - 1, :], axis=1) + u * w[CONV_W - 1:CONV_W, :] + cb_ref[...]
    y_ref[...] = _conv_post(y, lg_ref[...], lb_ref[...], ga_ref[...].astype(F32)).astype(BF16)
    ns_ref[:, 0:CONV_W - 2, :] = st[:, 1:CONV_W - 1, :]
    ns_ref[:, CONV_W - 2:CONV_W - 1, :] = u[:, None, :]


def _decode_conv(state_all, layer, u, ga, conv_w, conv_b, ln_g, ln_b):
    n = u.shape[0]
    tile = 32
    vec = pl.BlockSpec((1, W_A), lambda i: (0, 0))
    sspec = pl.BlockSpec((tile, CONV_W - 1, W_A), lambda i: (i, 0, 0))
    rspec = pl.BlockSpec((tile, W_A), lambda i: (i, 0))
    return pl.pallas_call(
        _decode_conv_kernel,
        grid=(n // tile,),
        in_specs=[pl.BlockSpec((None, tile, CONV_W - 1, W_A), lambda i: (layer, i, 0, 0)),
                  rspec, rspec, pl.BlockSpec((CONV_W, W_A), lambda i: (0, 0)), vec, vec, vec],
        out_specs=[rspec, sspec],
        out_shape=[jax.ShapeDtypeStruct((n, W_A), BF16),
                   jax.ShapeDtypeStruct((n, CONV_W - 1, W_A), F32)],
        compiler_params=_cparams("arbitrary"),
        name="decode_conv",
    )(state_all, u, ga, conv_w, conv_b.reshape(1, W_A), ln_g.reshape(1, W_A), ln_b.reshape(1, W_A))


def _decode_hgrn_kernel(q_ref, lg_ref, i_ref, sg_ref, gn_ref, s_ref, *rest, n_seq):
    o_ref, ns_ref = rest[-2:]

    def body(n, carry):
        ft = jnp.exp2(lg_ref[n]).T
        kt = 1.0 - ft
        qt = q_ref[n].T
        v8 = i_ref[n]
        rows = []
        for h in range(H_C):
            s_new = ft[:, h:h + 1] * s_ref[n, h] + kt[:, h:h + 1] * v8[h:h + 1, :]
            ns_ref[n, h] = s_new
            rows.append(jnp.sum(qt[:, h:h + 1] * s_new, axis=0, keepdims=True))
        o = jnp.concatenate(rows, axis=0)
        o_ref[n] = (_rms(o, gn_ref[...]) * sg_ref[n].astype(F32)).astype(BF16)
        return carry

    lax.fori_loop(0, n_seq, body, 0)


def _decode_hgrn(qs, lg, iv, sg, gn_g, state_all, layer, new_prev):
    n = qs.shape[0]
    tile = SAMPLE_STATE_TILE
    hspec = pl.BlockSpec((tile, H_C, DV_C), lambda i: (i, 0, 0))
    sspec = pl.BlockSpec((None, tile, H_C, DK_C, DV_C), lambda i: (layer, i, 0, 0, 0))
    h3 = lambda a: a.reshape(n, H_C, DV_C)
    in_specs = [hspec, hspec, hspec, hspec, pl.BlockSpec((1, DV_C), lambda i: (0, 0)), sspec]
    args = [h3(qs), h3(lg), h3(iv), h3(sg), gn_g.reshape(1, DV_C), state_all]
    aliases = {}
    if new_prev is not None:
        aliases = {len(args): 1}
        in_specs.append(pl.BlockSpec(memory_space=pl.ANY))
        args.append(new_prev)
    o, ns = pl.pallas_call(
        functools.partial(_decode_hgrn_kernel, n_seq=tile),
        grid=(n // tile,),
        in_specs=in_specs,
        out_specs=[hspec, sspec],
        out_shape=[jax.ShapeDtypeStruct((n, H_C, DV_C), BF16),
                   jax.ShapeDtypeStruct((N_C, n, H_C, DK_C, DV_C), F32)],
        input_output_aliases=aliases,
        compiler_params=_cparams("arbitrary"),
        name="decode_hgrn",
    )(*args)
    return o.reshape(n, W_C), ns


def _rope_tables(pos, rows):
    half = ROT_DIM // 2
    inv_freq = ROPE_THETA ** (-jnp.arange(half, dtype=F32) / half)
    ang = pos.astype(F32)[:, None] * inv_freq[None, :]
    cos, sin = jnp.cos(ang), jnp.sin(ang)
    n = pos.shape[0]
    ones = jnp.ones((n, HD_B - ROT_DIM), F32)
    zeros = jnp.zeros((n, HD_B - ROT_DIM), F32)
    z8 = jnp.zeros((n, half), F32)
    cos_t = jnp.concatenate([cos, cos, ones], axis=1)
    sa_t = jnp.concatenate([-sin, z8, zeros], axis=1)
    sb_t = jnp.concatenate([z8, sin, zeros], axis=1)
    out = []
    for t in (cos_t, sa_t, sb_t):
        t = jnp.concatenate([t, t], axis=1)
        out.append(jnp.broadcast_to(t, (rows, LANES)) if n == 1 else t)
    return out


def kernel(x_prompt, x_sample, c_prompt, c_sample, cache_k, cache_v, page_table, state_conv, state_hgrn,
           norm_g, ada_w, ada_b, w_in_ac, w_out_ac, conv_w, conv_b, ln_g, ln_b, lam_q1, lam_k1, lam_q2,
           lam_k2, subln_g, w_in_c, w_out_c, gn_g, lb_logits, final_g):
    bp, tp = x_prompt.shape[:2]
    bs, ts = x_sample.shape[:2]
    assert ts == 1
    n_pool = cache_k.shape[1]
    n_past = page_table.shape[1] * PAGE_SIZE
    mp = bp * tp

    lb_sm = jax.nn.softmax(lb_logits.astype(F32), axis=0)
    lb_all = jnp.cumsum(lb_sm, axis=0) - lb_sm[0]

    mod = _ada_modulation(jnp.concatenate([c_prompt, c_sample], axis=0), ada_w, ada_b)

    def mods(l):
        out = []
        for j in range(3):
            m = mod[l, :, j * D_MODEL:(j + 1) * D_MODEL]
            out.append((m[:bp].reshape(bp, 1, D_MODEL), m[bp:].reshape(1, bs, D_MODEL)))
        return out

    tabs_p = _rope_tables(jnp.arange(tp), tp)
    tabs_s = _rope_tables(jnp.full((1,), n_past), bs)
    ck4 = cache_k.reshape(N_AC, n_pool, PAGE_SIZE * H_B, 2 * HD_B)
    cv4 = cache_v.reshape(N_AC, n_pool, PAGE_SIZE * H_B, DV_B)

    tiles_p = tp // ROW_TILE
    hp = x_prompt.reshape(mp, D_MODEL)
    hs = x_sample.reshape(bs, D_MODEL)
    cp_l, sp_l, cs_l = [], [], []
    kv_p = kv_s = ss = None
    for l in range(DEPTH):
        (sh_p, sh_s), (sc_p, sc_s), (gt_p, gt_s) = mods(l)
        last = final_g if l == DEPTH - 1 else None
        if l % 2 == 0:
            a = l // 2
            lam_init = 0.8 - 0.6 * math.exp(-0.3 * l)
            lam = (jnp.exp(jnp.sum(lam_q1[a].astype(F32) * lam_k1[a].astype(F32)))
                   - jnp.exp(jnp.sum(lam_q2[a].astype(F32) * lam_k2[a].astype(F32))) + lam_init)
            lam = jnp.full((1, LANES), lam, F32)
            w_in = w_in_ac[a].astype(BF16)
            w_out = w_out_ac[a].astype(BF16)
            q, k, kb, v, vb, gb, u, ga = _inproj_ac(hp, norm_g[l], sc_p, sh_p, w_in, *tabs_p,
                                                    ROW_TILE, tiles_p, a, kv_p)
            kv_p = (k, v)
            qs, k, _, v, _, gbs, us, gas = _inproj_ac(hs, norm_g[l], sc_s, sh_s, w_in, *tabs_s, bs, 1, a, kv_s)
            kv_s = (k, v)
            ob, obs = _attention(q, kb, vb, gb, bp, tp, qs, k, v, gbs, ck4, cv4, a, page_table,
                                 lam, subln_g[a], lam_init)
            y, cst = _prompt_conv(u, ga, conv_w[a], conv_b[a], ln_g[a], ln_b[a], bp, tp)
            hp = _outproj([ob, y], w_out, hp, gt_p, ROW_TILE, tiles_p, last)
            cp_l.append(cst)
            y, cst = _decode_conv(state_conv, a, us, gas, conv_w[a], conv_b[a], ln_g[a], ln_b[a])
            hs = _outproj([obs, y], w_out, hs, gt_s, bs, 1, last)
            cs_l.append(cst)
        else:
            ci = l // 2
            w_in = w_in_c[ci].astype(BF16)
            w_out = w_out_c[ci].astype(BF16)
            qs, lg, iv, sg = _inproj_c(hp, norm_g[l], sc_p, sh_p, w_in, lb_all[l], ROW_TILE, tiles_p)
            o, st = _prompt_hgrn(qs, lg, iv, sg, gn_g[ci], bp, tp)
            hp = _outproj([o], w_out, hp, gt_p, ROW_TILE, tiles_p, last)
            sp_l.append(st)
            qs, lg, iv, sg = _inproj_c(hs, norm_g[l], sc_s, sh_s, w_in, lb_all[l], bs, 1)
            o, ss = _decode_hgrn(qs, lg, iv, sg, gn_g[ci], state_hgrn, ci, ss)
            hs = _outproj([o], w_out, hs, gt_s, bs, 1, last)

    y_prompt = hp.reshape(bp, tp, D_MODEL)
    y_sample = hs.reshape(bs, ts, D_MODEL)
    return (y_prompt, y_sample,
            kv_p[0].reshape(N_AC, bp, tp, H_B, 2 * HD_B), kv_p[1].reshape(N_AC, bp, tp, H_B, DV_B),
            jnp.stack(cp_l), jnp.stack(sp_l),
            kv_s[0].reshape(N_AC, bs, ts, H_B, 2 * HD_B), kv_s[1].reshape(N_AC, bs, ts, H_B, DV_B),
            jnp.stack(cs_l), ss)
```

```python
import functools
import math

import numpy as np
import jax
import jax.numpy as jnp
from jax import lax
from jax.experimental import pallas as pl
from jax.experimental.pallas import tpu as pltpu

F32 = jnp.float32
BF16 = jnp.bfloat16

D_MODEL = 1024
DEPTH = 4
PAGE_SIZE = 128
N_AC = (DEPTH + 1) // 2
N_C = DEPTH // 2

H_B = 4
HD_B = 64
DV_B = 2 * HD_B
W_B = H_B * DV_B
QK_W = H_B * 2 * HD_B
ROT_DIM = HD_B // 4
ROPE_THETA = 500000.0
NEG_INF = -1e30

W_A = D_MODEL // 2
CONV_W = 31

H_C = 8
DK_C = D_MODEL // H_C
DV_C = D_MODEL // H_C
W_C = H_C * DV_C

EPS = 1e-6
LOG2_E = math.log2(math.e)
Q_SCALE = HD_B ** -0.5 * LOG2_E

AC_IN = 2 * QK_W + 2 * W_B + 3 * W_A
C_IN = 2 * H_C * DK_C + 2 * W_C

LANES = 128
SUBLANES = 8
MXU_WIDTH = 256
VMEM_LIMIT_BYTES = 56 * 1024 * 1024

ROW_TILE = 512
ATTN_TILE = 256
CONV_TILE = 512
CONV_HALO = 32
HGRN_CHUNK = 128
SAMPLE_STATE_TILE = 8


def _cparams(*sem):
    return pltpu.CompilerParams(dimension_semantics=sem, vmem_limit_bytes=VMEM_LIMIT_BYTES)


def _silu(x):
    return x * jax.nn.sigmoid(x)


def _dot(a, b):
    return jnp.dot(a, b, preferred_element_type=F32)


def _dot_nt(a, b):
    return lax.dot_general(a, b, (((1,), (1,)), ((), ())), preferred_element_type=F32)


def _dot_tn(a, b):
    return lax.dot_general(a, b, (((0,), (0,)), ((), ())), preferred_element_type=F32)


def _rms(x, g):
    return x * lax.rsqrt(jnp.mean(x * x, axis=-1, keepdims=True) + EPS) * g


def _ada_kernel(c_ref, w_ref, b_ref, o_ref):
    s = _silu(c_ref[...]).astype(BF16)
    o_ref[...] = _dot(s, w_ref[...].astype(BF16)) + b_ref[...]


def _ada_modulation(c_all, ada_w, ada_b):
    rows = c_all.shape[0]
    return pl.pallas_call(
        _ada_kernel,
        grid=(DEPTH, 3),
        in_specs=[
            pl.BlockSpec((rows, D_MODEL), lambda l, j: (0, 0)),
            pl.BlockSpec((None, D_MODEL, D_MODEL), lambda l, j: (l, 0, j)),
            pl.BlockSpec((None, 1, D_MODEL), lambda l, j: (l, 0, j)),
        ],
        out_specs=pl.BlockSpec((None, rows, D_MODEL), lambda l, j: (l, 0, j)),
        out_shape=jax.ShapeDtypeStruct((DEPTH, rows, 3 * D_MODEL), F32),
        compiler_params=_cparams("arbitrary", "arbitrary"),
        name="ada_modulation",
    )(c_all, ada_w, ada_b.reshape(DEPTH, 1, 3 * D_MODEL))


def _modulated(x_ref, ng_ref, sc_ref, sh_ref):
    h = _rms(x_ref[...], ng_ref[...])
    return (h * (1.0 + sc_ref[...]) + sh_ref[...]).astype(BF16)


def _inproj_ac_kernel(x_ref, ng_ref, sc_ref, sh_ref, w_ref, cos_ref, sa_ref, sb_ref, *rest):
    q_ref, k_ref, kb_ref, v_ref, vb_ref, gb_ref, u_ref, ga_ref, h_scr = rest[-9:]
    h_scr[...] = _modulated(x_ref, ng_ref, sc_ref, sh_ref)

    def proj(s):
        return _dot(h_scr[...], w_ref[:, s * QK_W:(s + 1) * QK_W])

    cos, sa, sb = cos_ref[...], sa_ref[...], sb_ref[...]

    def rope(z):
        return (z * cos + pltpu.roll(z, LANES - ROT_DIM // 2, 1) * sa
                + pltpu.roll(z, ROT_DIM // 2, 1) * sb)

    zq = proj(0)
    for g in range(H_B):
        sl = slice(g * LANES, (g + 1) * LANES)
        q_ref[:, sl] = (rope(zq[:, sl]) * Q_SCALE).astype(BF16)
    tm = x_ref.shape[0]
    zk = proj(1)
    for g in range(H_B):
        sl = slice(g * LANES, (g + 1) * LANES)
        r = rope(zk[:, sl])
        k_ref[pl.ds(g, tm, stride=H_B), :] = r
        kb_ref[:, sl] = r.astype(BF16)
    zv = proj(2)
    for g in range(H_B):
        v_ref[pl.ds(g, tm, stride=H_B), :] = zv[:, g * LANES:(g + 1) * LANES]
    vb_ref[...] = zv.astype(BF16)
    gb_ref[...] = _silu(proj(3)).astype(BF16)
    a_val = proj(4)
    u_ref[...] = a_val * jax.nn.sigmoid(proj(5))
    ga_ref[...] = _silu(proj(6)).astype(BF16)


def _mod_specs(rows_per_mod, tm, tiles_per_mod):
    r = rows_per_mod
    return pl.BlockSpec((None, r, D_MODEL), lambda i: (i // tiles_per_mod, 0, 0))


def _inproj_ac(x, norm_g, scale, shift, w, cos_t, sa_t, sb_t, tm, tiles_per_mod, layer, kv_prev):
    m = x.shape[0]
    tab_tiles = cos_t.shape[0] // tm
    mod_spec = _mod_specs(scale.shape[1], tm, tiles_per_mod)
    row512 = pl.BlockSpec((tm, QK_W), lambda i: (i, 0))
    heads = pl.BlockSpec((None, H_B * tm, LANES), lambda i: (layer, i, 0))
    tab_spec = pl.BlockSpec((tm, LANES), lambda i: (i % tab_tiles, 0))
    out_shape = [
        jax.ShapeDtypeStruct((m, QK_W), BF16),
        jax.ShapeDtypeStruct((N_AC, H_B * m, LANES), F32),
        jax.ShapeDtypeStruct((m, QK_W), BF16),
        jax.ShapeDtypeStruct((N_AC, H_B * m, LANES), F32),
        jax.ShapeDtypeStruct((m, W_B), BF16),
        jax.ShapeDtypeStruct((m, W_B), BF16),
        jax.ShapeDtypeStruct((m, W_A), F32),
        jax.ShapeDtypeStruct((m, W_A), BF16),
    ]
    in_specs = [
        pl.BlockSpec((tm, D_MODEL), lambda i: (i, 0)),
        pl.BlockSpec((1, D_MODEL), lambda i: (0, 0)),
        mod_spec, mod_spec,
        pl.BlockSpec((D_MODEL, AC_IN), lambda i: (0, 0)),
        tab_spec, tab_spec, tab_spec,
    ]
    args = [x, norm_g.reshape(1, D_MODEL), scale, shift, w, cos_t, sa_t, sb_t]
    aliases = {}
    if kv_prev is not None:
        aliases = {len(args): 1, len(args) + 1: 3}
        in_specs += [pl.BlockSpec(memory_space=pl.ANY)] * 2
        args += list(kv_prev)
    return pl.pallas_call(
        _inproj_ac_kernel,
        grid=(m // tm,),
        in_specs=in_specs,
        out_specs=[row512, heads, row512, heads, row512, row512, row512, row512],
        out_shape=out_shape,
        input_output_aliases=aliases,
        scratch_shapes=[pltpu.VMEM((tm, D_MODEL), BF16)],
        compiler_params=_cparams("arbitrary"),
        name="inproj_conv_attn",
    )(*args)


def _inproj_c_kernel(x_ref, ng_ref, sc_ref, sh_ref, w_ref, lb_ref, q_ref, lg_ref, i_ref, sg_ref, h_scr):
    h_scr[...] = _modulated(x_ref, ng_ref, sc_ref, sh_ref)

    def proj(s):
        return _dot(h_scr[...], w_ref[:, s * W_C:(s + 1) * W_C])

    q_ref[...] = _silu(proj(0))
    lb = lb_ref[...]
    lg_ref[...] = jnp.log(lb + (1.0 - lb) * jax.nn.sigmoid(proj(1))) * LOG2_E
    i_ref[...] = proj(2)
    sg_ref[...] = _silu(proj(3)).astype(BF16)


def _inproj_c(x, norm_g, scale, shift, w, lb, tm, tiles_per_mod):
    m = x.shape[0]
    mod_spec = _mod_specs(scale.shape[1], tm, tiles_per_mod)
    row = pl.BlockSpec((tm, W_C), lambda i: (i, 0))
    out_shape = [
        jax.ShapeDtypeStruct((m, W_C), F32),
        jax.ShapeDtypeStruct((m, W_C), F32),
        jax.ShapeDtypeStruct((m, W_C), F32),
        jax.ShapeDtypeStruct((m, W_C), BF16),
    ]
    return pl.pallas_call(
        _inproj_c_kernel,
        grid=(m // tm,),
        in_specs=[
            pl.BlockSpec((tm, D_MODEL), lambda i: (i, 0)),
            pl.BlockSpec((1, D_MODEL), lambda i: (0, 0)),
            mod_spec, mod_spec,
            pl.BlockSpec((D_MODEL, C_IN), lambda i: (0, 0)),
            pl.BlockSpec((1, W_C), lambda i: (0, 0)),
        ],
        out_specs=[row] * 4,
        out_shape=out_shape,
        scratch_shapes=[pltpu.VMEM((tm, D_MODEL), BF16)],
        compiler_params=_cparams("arbitrary"),
        name="inproj_hgrn",
    )(x, norm_g.reshape(1, D_MODEL), scale, shift, w, lb.reshape(1, W_C))


def _outproj_kernel(*refs, n_act, final_norm):
    acts = refs[:n_act]
    w_ref, x_ref, gate_ref = refs[n_act:n_act + 3]
    o_ref = refs[-1]
    acc = None
    lo = 0
    for a_ref in acts:
        width = a_ref.shape[-1]
        part = _dot(a_ref[...], w_ref[lo:lo + width, :])
        acc = part if acc is None else acc + part
        lo += width
    y = x_ref[...] + gate_ref[...] * acc
    if final_norm:
        y = _rms(y, refs[n_act + 3][...])
    o_ref[...] = y


def _outproj(acts, w, x, gate, tm, tiles_per_mod, final_g=None):
    m = x.shape[0]
    mod_spec = _mod_specs(gate.shape[1], tm, tiles_per_mod)
    in_specs = [pl.BlockSpec((tm, a.shape[1]), lambda i: (i, 0)) for a in acts]
    in_specs += [
        pl.BlockSpec((D_MODEL, D_MODEL), lambda i: (0, 0)),
        pl.BlockSpec((tm, D_MODEL), lambda i: (i, 0)),
        mod_spec,
    ]
    args = list(acts) + [w, x, gate]
    if final_g is not None:
        in_specs.append(pl.BlockSpec((1, D_MODEL), lambda i: (0, 0)))
        args.append(final_g.reshape(1, D_MODEL))
    return pl.pallas_call(
        functools.partial(_outproj_kernel, n_act=len(acts), final_norm=final_g is not None),
        grid=(m // tm,),
        in_specs=in_specs,
        out_specs=pl.BlockSpec((tm, D_MODEL), lambda i: (i, 0)),
        out_shape=jax.ShapeDtypeStruct((m, D_MODEL), F32),
        compiler_params=_cparams("arbitrary"),
        name="outproj",
    )(*args)


def _attn_kernel(pt_ref, lam_ref, sgc_ref, sgr_ref, q_ref, k_ref, v_ref, gb_ref,
                 qs_ref, kn_ref, vn_ref, gbs_ref, *rest, tile, lam_init, n_dec, n_pages):
    pages = rest[:2 * n_dec * n_pages]
    o_ref, os_ref, q2_scr, m_scr, l_scr, acc_scr = rest[2 * n_dec * n_pages:]
    sg_ref = sgc_ref
    qi = pl.program_id(1)
    heads = [slice(h * LANES, (h + 1) * LANES) for h in range(H_B)]
    lane = lax.broadcasted_iota(jnp.int32, (tile, LANES), 1)
    for h, sl in enumerate(heads):
        q = q_ref[:, sl]
        zero = jnp.zeros_like(q)
        q2_scr[h] = jnp.concatenate([jnp.where(lane < HD_B, q, zero), jnp.where(lane >= HD_B, q, zero)], axis=0)
    m_scr[...] = jnp.full(m_scr.shape, NEG_INF, F32)
    l_scr[...] = jnp.zeros(l_scr.shape, F32)
    acc_scr[...] = jnp.zeros(acc_scr.shape, F32)

    def step(start, mask):
        s = [_dot_nt(k_ref[pl.ds(start, tile), sl], q2_scr[h]) for h, sl in enumerate(heads)]
        for h, sl in enumerate(heads):
            sh = s[h] if mask is None else jnp.where(mask, s[h], NEG_INF)
            m_old = m_scr[h]
            m_new = jnp.maximum(m_old, jnp.max(sh, axis=0, keepdims=True))
            alpha = jnp.exp2(m_old - m_new)
            p = jnp.exp2(sh - m_new)
            m_scr[h] = m_new
            l_scr[h] = alpha * l_scr[h] + jnp.sum(p, axis=0, keepdims=True)
            acc_scr[h] = alpha * acc_scr[h] + _dot_tn(v_ref[pl.ds(start, tile), sl], p.astype(BF16))

    def body(j, carry):
        step(pl.multiple_of(j * tile, tile), None)
        return carry

    lax.fori_loop(0, qi, body, 0)
    lam = lam_ref[:, 0:1]
    dec = []
    for r in range(n_dec):
        k_pages = pages[2 * r * n_pages:(2 * r + 1) * n_pages]
        v_pages = pages[(2 * r + 1) * n_pages:(2 * r + 2) * n_pages]
        dec.append(_decode_scores(qs_ref[r].astype(F32), kn_ref[r], k_pages) + (v_pages,))
    key = lax.broadcasted_iota(jnp.int32, (tile, 2 * tile), 0)
    qry = lax.broadcasted_iota(jnp.int32, (tile, 2 * tile), 1)
    step(pl.multiple_of(qi * tile, tile), key <= jnp.where(qry >= tile, qry - tile, qry))
    for r, (s, s_new, v_pages) in enumerate(dec):
        os_ref[r] = _decode_finish(s, s_new, lam, vn_ref[r], v_pages, sgr_ref[...],
                                   gbs_ref[r].astype(F32), lam_init).astype(BF16)
    for h, sl in enumerate(heads):
        l, acc = l_scr[h], acc_scr[h]
        o = acc[:, :tile] / l[:, :tile] - lam_ref[:, 0:1] * (acc[:, tile:] / l[:, tile:])
        y = o * lax.rsqrt(jnp.mean(o * o, axis=0, keepdims=True) + EPS) * sg_ref[...]
        o_ref[:, sl] = (y.T * (1.0 - lam_init) * gb_ref[:, sl].astype(F32)).astype(BF16)


def _attention(q, kb, vb, gb, batch, seq, qs, kn_all, vn_all, gbs, cache_k4, cache_v4, layer, page_table,
               lam, subln_g, lam_init):
    tile = ATTN_TILE
    nq = seq // tile
    n_seq, n_pages = page_table.shape
    n_dec = n_seq // (batch * nq)
    assert n_dec * batch * nq == n_seq
    q3, k3, v3, g3 = (a.reshape(batch, seq, QK_W) for a in (q, kb, vb, gb))
    qspec = pl.BlockSpec((None, tile, QK_W), lambda b, i, pt: (b, i, 0))
    kvspec = pl.BlockSpec((None, seq, QK_W), lambda b, i, pt: (b, 0, 0))
    rows = pl.BlockSpec((n_dec, 1, QK_W), lambda b, i, pt: (b * nq + i, 0, 0))
    new_kv = pl.BlockSpec((None, n_dec, H_B, LANES), lambda b, i, pt: (layer, b * nq + i, 0, 0))

    def page_spec(r, j):
        return pl.BlockSpec((None, None, H_B * PAGE_SIZE, LANES),
                            lambda b, i, pt: (layer, pt[(b * nq + i) * n_dec + r, j], 0, 0))

    page_specs, page_args = [], []
    for r in range(n_dec):
        for cache in (cache_k4, cache_v4):
            page_specs += [page_spec(r, j) for j in range(n_pages)]
            page_args += [cache] * n_pages
    grid_spec = pltpu.PrefetchScalarGridSpec(
        num_scalar_prefetch=1,
        grid=(batch, nq),
        in_specs=[pl.BlockSpec((1, LANES), lambda b, i, pt: (0, 0)),
                  pl.BlockSpec((DV_B, 1), lambda b, i, pt: (0, 0)),
                  pl.BlockSpec((1, DV_B), lambda b, i, pt: (0, 0)),
                  qspec, kvspec, kvspec, qspec, rows, new_kv, new_kv, rows] + page_specs,
        out_specs=[qspec, rows],
        scratch_shapes=[pltpu.VMEM((H_B, 2 * tile, LANES), BF16),
                        pltpu.VMEM((H_B, 1, 2 * tile), F32),
                        pltpu.VMEM((H_B, 1, 2 * tile), F32),
                        pltpu.VMEM((H_B, DV_B, 2 * tile), F32)],
    )
    r3 = lambda a: a.reshape(n_seq, 1, QK_W)
    h4 = lambda a: a.reshape(N_AC, n_seq, H_B, LANES)
    out, out_s = pl.pallas_call(
        functools.partial(_attn_kernel, tile=tile, lam_init=lam_init, n_dec=n_dec, n_pages=n_pages),
        grid_spec=grid_spec,
        out_shape=[jax.ShapeDtypeStruct((batch, seq, W_B), BF16),
                   jax.ShapeDtypeStruct((n_seq, 1, W_B), BF16)],
        compiler_params=_cparams("arbitrary", "arbitrary"),
        name="diff_attention",
    )(page_table, lam, subln_g.reshape(DV_B, 1), subln_g.reshape(1, DV_B), q3, k3, v3, g3,
      r3(qs), h4(kn_all), h4(vn_all), r3(gbs), *page_args)
    return out.reshape(batch * seq, W_B), out_s.reshape(n_seq, W_B)


def _conv_post(y, ln_g, ln_b, ga):
    yc = y - jnp.mean(y, axis=-1, keepdims=True)
    yn = yc * lax.rsqrt(jnp.mean(yc * yc, axis=-1, keepdims=True) + EPS) * ln_g + ln_b
    return _silu(yn) * ga


def _conv_kernel(u_ref, ga_ref, w_ref, cb_ref, lg_ref, lb_ref, y_ref, st_ref, ext_scr, sh_scr, *, tile, rows):
    t = pl.program_id(1)
    first = CONV_HALO - (CONV_W - 1)
    sh_rows = sh_scr.shape[1]

    @pl.when(t == 0)
    def _():
        ext_scr[0:CONV_HALO, :] = jnp.zeros((CONV_HALO, W_A), F32)

    @pl.when(t > 0)
    def _():
        ext_scr[0:CONV_HALO, :] = ext_scr[tile:tile + CONV_HALO, :]

    ext_scr[CONV_HALO:CONV_HALO + tile, :] = u_ref[...]
    for r in range(1, SUBLANES):
        sh_scr[r - 1] = ext_scr[r:r + sh_rows, :]
    for r0 in range(0, tile, rows):
        acc = jnp.zeros((rows, W_A), F32) + cb_ref[...]
        for j in range(CONV_W):
            off = first + j
            r, base = off % SUBLANES, off - off % SUBLANES + r0
            src = ext_scr[base:base + rows, :] if r == 0 else sh_scr[r - 1, base:base + rows, :]
            acc = acc + src * w_ref[j:j + 1, :]
        y_ref[r0:r0 + rows, :] = _conv_post(acc, lg_ref[...], lb_ref[...],
                                            ga_ref[r0:r0 + rows, :].astype(F32)).astype(BF16)

    @pl.when(t == pl.num_programs(1) - 1)
    def _():
        st_ref[...] = ext_scr[CONV_HALO + tile - (CONV_W - 1):CONV_HALO + tile, :]


def _prompt_conv(u, ga, conv_w, conv_b, ln_g, ln_b, batch, seq):
    tile = CONV_TILE
    u3, g3 = u.reshape(batch, seq, W_A), ga.reshape(batch, seq, W_A)
    tspec = pl.BlockSpec((None, tile, W_A), lambda b, t: (b, t, 0))
    vec = pl.BlockSpec((1, W_A), lambda b, t: (0, 0))
    y, st = pl.pallas_call(
        functools.partial(_conv_kernel, tile=tile, rows=64),
        grid=(batch, seq // tile),
        in_specs=[tspec, tspec, pl.BlockSpec((CONV_W, W_A), lambda b, t: (0, 0)), vec, vec, vec],
        out_specs=[tspec, pl.BlockSpec((None, CONV_W - 1, W_A), lambda b, t: (b, 0, 0))],
        out_shape=[jax.ShapeDtypeStruct((batch, seq, W_A), BF16),
                   jax.ShapeDtypeStruct((batch, CONV_W - 1, W_A), F32)],
        scratch_shapes=[pltpu.VMEM((CONV_HALO + tile, W_A), F32),
                        pltpu.VMEM((SUBLANES - 1, CONV_HALO + tile - SUBLANES, W_A), F32)],
        compiler_params=_cparams("arbitrary", "arbitrary"),
        name="prompt_conv",
    )(u3, g3, conv_w, conv_b.reshape(1, W_A), ln_g.reshape(1, W_A), ln_b.reshape(1, W_A))
    return y.reshape(batch * seq, W_A), st


def _hgrn_tables(chunk):
    levels = int(math.log2(chunk))
    t = np.arange(chunk)[:, None]
    u = np.arange(chunk)[None, :]
    masks = []
    for v in range(levels):
        m = 1 << v
        masks.append((t // (2 * m) == u // (2 * m)) & (t % (2 * m) >= m) & (u % (2 * m) < m))
    return jnp.asarray(u <= t, BF16), jnp.asarray(np.stack(masks), F32), levels


def _split3(x):
    hi = x.astype(BF16)
    r = x - hi.astype(F32)
    mid = r.astype(BF16)
    lo = (r - mid.astype(F32)).astype(BF16)
    return hi, mid, lo


def _hgrn_level_exponent(b, g2, m, row):
    chunk, width = b.shape
    if m == 1:
        return jnp.where((row & 1) != 0, g2, 0.0)
    if 2 * m < SUBLANES:
        b3 = b.reshape(chunk // SUBLANES, SUBLANES, width)
        sub = lax.broadcasted_iota(jnp.int32, b3.shape, 1)
        bm = b3[:, m - 1:m, :]
        for blk in range(1, SUBLANES // (2 * m)):
            bm = jnp.where(sub < blk * 2 * m, bm, b3[:, blk * 2 * m + m - 1:blk * 2 * m + m, :])
    else:
        b3 = b.reshape(chunk // (2 * m), 2 * m, width)
        bm = b3[:, m - 1:m, :]
    return -jnp.abs((b3 - bm).reshape(chunk, width))


def _hgrn_chunk_stages(q_ref, lg_ref, i_ref, sg_ref, tri_ref, msk_ref, gn_ref, o_ref, st_scr, chunk, levels):
    heads = [slice(h * LANES, (h + 1) * LANES) for h in range(H_C)]
    g2 = lg_ref[...]
    q = q_ref[...]
    v = i_ref[...]
    k = 1.0 - jnp.exp2(g2)
    b3 = _dot(tri_ref[...], jnp.concatenate(_split3(g2), axis=1))
    yield
    b = b3[:, :W_C] + b3[:, W_C:2 * W_C] + b3[:, 2 * W_C:]
    row = lax.broadcasted_iota(jnp.int32, (chunk, W_C), 0)
    a = [jnp.zeros((chunk, chunk), F32)] * H_C
    for lv in range(levels):
        m = 1 << lv
        x = (jnp.where((row & m) != 0, q, k) * jnp.exp2(_hgrn_level_exponent(b, g2, m, row))).astype(BF16)
        msk = msk_ref[lv]
        a = [a[h] + msk * _dot_nt(x[:, sl], x[:, sl]) for h, sl in enumerate(heads)]
        yield
    vb = v.astype(BF16)
    qe = (q * jnp.exp2(b)).astype(BF16)
    b_last = b[chunk - 1:chunk, :]
    kd = (k * jnp.exp2(b_last - b)).astype(BF16)
    decay = jnp.exp2(b_last)
    qk = q * k
    for h, sl in enumerate(heads):
        st = st_scr[h]
        o = _dot(a[h].astype(BF16), vb[:, sl]) + jnp.sum(qk[:, sl], axis=1, keepdims=True) * v[:, sl]
        o = o + _dot_nt(qe[:, sl], st.astype(BF16))
        st_scr[h] = st * decay[:, sl] + _dot_tn(vb[:, sl], kd[:, sl])
        o_ref[:, sl] = (_rms(o, gn_ref[...]) * sg_ref[:, sl].astype(F32)).astype(BF16)


def _inproj_hgrn_kernel(x_ref, ng_ref, sc_ref, sh_ref, w_ref, lb_ref, tri_ref, msk_ref, gn_ref,
                        o_ref, s_ref, h_scr, q_scr, lg_scr, iv_scr, sg_scr, st_scr,
                        *, chunk, levels, tiles_per_batch):
    i = pl.program_id(0)
    slot = i % 2
    prev = 1 - slot
    tm = x_ref.shape[0]
    assert tm // chunk == 4

    @pl.when(i == 0)
    def _():
        for scr in (q_scr, lg_scr, iv_scr, sg_scr):
            scr[1] = jnp.zeros(scr.shape[1:], scr.dtype)

    @pl.when((i + tiles_per_batch - 1) % tiles_per_batch == 0)
    def _():
        st_scr[...] = jnp.zeros_like(st_scr)

    h_scr[...] = _modulated(x_ref, ng_ref, sc_ref, sh_ref)
    lb = lb_ref[...]

    sections = [
        (q_scr, lambda z, cols: _silu(z)),
        (lg_scr, lambda z, cols: jnp.log(lb[:, cols] + (1.0 - lb[:, cols]) * jax.nn.sigmoid(z)) * LOG2_E),
        (iv_scr, lambda z, cols: z),
        (sg_scr, lambda z, cols: _silu(z).astype(BF16)),
    ]
    n_pieces = W_C // MXU_WIDTH
    for s, (scr, post) in enumerate(sections):
        rows = pl.ds(s * chunk, chunk)
        stages = _hgrn_chunk_stages(q_scr.at[prev, rows], lg_scr.at[prev, rows], iv_scr.at[prev, rows],
                                    sg_scr.at[prev, rows], tri_ref, msk_ref, gn_ref, o_ref.at[rows],
                                    st_scr, chunk, levels)
        for p in range(n_pieces):
            cols = slice(p * MXU_WIDTH, (p + 1) * MXU_WIDTH)
            z = _dot(h_scr[...], w_ref[:, s * W_C + p * MXU_WIDTH:s * W_C + (p + 1) * MXU_WIDTH])
            scr[slot, :, cols] = post(z, cols)
            next(stages, None)
            next(stages, None)
        for _ in stages:
            pass

    @pl.when(jnp.logical_and(i > 0, i % tiles_per_batch == 0))
    def _():
        for h in range(H_C):
            s_ref[h] = st_scr[h].T


def _inproj_hgrn(x, norm_g, scale, shift, w, lb, gn_g, batch, seq):
    tm, chunk = ROW_TILE, HGRN_CHUNK
    m = x.shape[0]
    nt = m // tm
    tiles_per_batch = seq // tm
    tri, masks, levels = _hgrn_tables(chunk)
    cur = lambda i: jnp.minimum(i, nt - 1)
    done = lambda i: jnp.maximum(i - 1, 0)
    mod_spec = pl.BlockSpec((None, 1, D_MODEL), lambda i: (cur(i) // tiles_per_batch, 0, 0))
    const = lambda shape: pl.BlockSpec(shape, lambda i: (0,) * len(shape))
    o, s = pl.pallas_call(
        functools.partial(_inproj_hgrn_kernel, chunk=chunk, levels=levels, tiles_per_batch=tiles_per_batch),
        grid=(nt + 1,),
        in_specs=[
            pl.BlockSpec((tm, D_MODEL), lambda i: (cur(i), 0)),
            const((1, D_MODEL)),
            mod_spec, mod_spec,
            const((D_MODEL, C_IN)),
            const((1, W_C)),
            const(tri.shape), const(masks.shape), const((1, DV_C)),
        ],
        out_specs=[pl.BlockSpec((tm, W_C), lambda i: (done(i), 0)),
                   pl.BlockSpec((None, H_C, DK_C, DV_C), lambda i: (done(i) // tiles_per_batch, 0, 0, 0))],
        out_shape=[jax.ShapeDtypeStruct((m, W_C), BF16),
                   jax.ShapeDtypeStruct((batch, H_C, DK_C, DV_C), F32)],
        scratch_shapes=[pltpu.VMEM((tm, D_MODEL), BF16),
                        pltpu.VMEM((2, tm, W_C), F32),
                        pltpu.VMEM((2, tm, W_C), F32),
                        pltpu.VMEM((2, tm, W_C), F32),
                        pltpu.VMEM((2, tm, W_C), BF16),
                        pltpu.VMEM((H_C, DV_C, DK_C), F32)],
        compiler_params=_cparams("arbitrary"),
        name="inproj_hgrn_recurrence",
    )(x, norm_g.reshape(1, D_MODEL), scale, shift, w, lb.reshape(1, W_C), tri, masks, gn_g.reshape(1, DV_C))
    return o, s


_DEC_ROWS = 2 * H_B


def _decode_head_rows(pieces):
    row = lax.broadcasted_iota(jnp.int32, (_DEC_ROWS, LANES), 0)
    out = jnp.zeros((_DEC_ROWS, LANES), F32)
    for h in range(H_B):
        out = jnp.where(row // 2 == h, jnp.broadcast_to(pieces[h], (_DEC_ROWS, LANES)), out)
    return out


def _decode_scores(q, kn, k_pages):
    row = lax.broadcasted_iota(jnp.int32, (_DEC_ROWS, LANES), 0)
    lane = lax.broadcasted_iota(jnp.int32, (_DEC_ROWS, LANES), 1)
    qm32 = jnp.where(lane // HD_B == row % 2,
                     _decode_head_rows([q[:, h * LANES:(h + 1) * LANES] for h in range(H_B)]), 0.0)
    qm = qm32.astype(BF16)
    s = jnp.concatenate([_dot_nt(qm, kp[...].astype(BF16)) for kp in k_pages], axis=1)
    key_head = lax.broadcasted_iota(jnp.int32, s.shape, 1) % H_B
    s = jnp.where(key_head == lax.broadcasted_iota(jnp.int32, s.shape, 0) // 2, s, NEG_INF)
    s_new = jnp.sum(qm32 * _decode_head_rows([kn[h:h + 1, :] for h in range(H_B)]), axis=-1, keepdims=True)
    return s, s_new


def _decode_finish(s, s_new, lam, vn, v_pages, subln_row, gb, lam_init):
    page_rows = H_B * PAGE_SIZE
    m = jnp.maximum(jnp.max(s, axis=-1, keepdims=True), s_new)
    p = jnp.exp2(s - m)
    p_new = jnp.exp2(s_new - m)
    l = jnp.sum(p, axis=-1, keepdims=True) + p_new
    comp = lax.broadcasted_iota(jnp.int32, (_DEC_ROWS, 1), 0) % 2
    wgt = jnp.where(comp == 0, 1.0, -lam) / l
    pw = (p * wgt).astype(BF16)
    o8 = (p_new * wgt) * _decode_head_rows([vn[h:h + 1, :] for h in range(H_B)])
    for j, vp in enumerate(v_pages):
        o8 = o8 + _dot(pw[:, j * page_rows:(j + 1) * page_rows], vp[...].astype(BF16))
    out = []
    for h in range(H_B):
        o = o8[2 * h:2 * h + 1, :] + o8[2 * h + 1:2 * h + 2, :]
        out.append(_rms(o, subln_row) * (1.0 - lam_init) * gb[:, h * DV_B:(h + 1) * DV_B])
    return jnp.concatenate(out, axis=1)


def _decode_conv_kernel(st_ref, u_ref, ga_ref, w_ref, cb_ref, lg_ref, lb_ref, y_ref, ns_ref):
    st = st_ref[...]
    u = u_ref[...]
    w = w_ref[...]
    y = jnp.sum(st * w[None, :CONV_W - 1, :], axis=1) + u * w[CONV_W - 1:CONV_W, :] + cb_ref[...]
    y_ref[...] = _conv_post(y, lg_ref[...], lb_ref[...], ga_ref[...].astype(F32)).astype(BF16)
    ns_ref[:, 0:CONV_W - 2, :] = st[:, 1:CONV_W - 1, :]
    ns_ref[:, CONV_W - 2:CONV_W - 1, :] = u[:, None, :]


def _decode_conv(state_all, layer, u, ga, conv_w, conv_b, ln_g, ln_b):
    n = u.shape[0]
    tile = 32
    vec = pl.BlockSpec((1, W_A), lambda i: (0, 0))
    sspec = pl.BlockSpec((tile, CONV_W - 1, W_A), lambda i: (i, 0, 0))
    rspec = pl.BlockSpec((tile, W_A), lambda i: (i, 0))
    return pl.pallas_call(
        _decode_conv_kernel,
        grid=(n // tile,),
        in_specs=[pl.BlockSpec((None, tile, CONV_W - 1, W_A), lambda i: (layer, i, 0, 0)),
                  rspec, rspec, pl.BlockSpec((CONV_W, W_A), lambda i: (0, 0)), vec, vec, vec],
        out_specs=[rspec, sspec],
        out_shape=[jax.ShapeDtypeStruct((n, W_A), BF16),
                   jax.ShapeDtypeStruct((n, CONV_W - 1, W_A), F32)],
        compiler_params=_cparams("arbitrary"),
        name="decode_conv",
    )(state_all, u, ga, conv_w, conv_b.reshape(1, W_A), ln_g.reshape(1, W_A), ln_b.reshape(1, W_A))


def _decode_hgrn_kernel(q_ref, lg_ref, i_ref, sg_ref, gn_ref, s_ref, *rest, n_seq):
    o_ref, ns_ref = rest[-2:]

    def body(n, carry):
        ft = jnp.exp2(lg_ref[n]).T
        kt = 1.0 - ft
        qt = q_ref[n].T
        v8 = i_ref[n]
        rows = []
        for h in range(H_C):
            s_new = ft[:, h:h + 1] * s_ref[n, h] + kt[:, h:h + 1] * v8[h:h + 1, :]
            ns_ref[n, h] = s_new
            rows.append(jnp.sum(qt[:, h:h + 1] * s_new, axis=0, keepdims=True))
        o = jnp.concatenate(rows, axis=0)
        o_ref[n] = (_rms(o, gn_ref[...]) * sg_ref[n].astype(F32)).astype(BF16)
        return carry

    lax.fori_loop(0, n_seq, body, 0)


def _decode_hgrn(qs, lg, iv, sg, gn_g, state_all, layer, new_prev):
    n = qs.shape[0]
    tile = SAMPLE_STATE_TILE
    hspec = pl.BlockSpec((tile, H_C, DV_C), lambda i: (i, 0, 0))
    sspec = pl.BlockSpec((None, tile, H_C, DK_C, DV_C), lambda i: (layer, i, 0, 0, 0))
    h3 = lambda a: a.reshape(n, H_C, DV_C)
    in_specs = [hspec, hspec, hspec, hspec, pl.BlockSpec((1, DV_C), lambda i: (0, 0)), sspec]
    args = [h3(qs), h3(lg), h3(iv), h3(sg), gn_g.reshape(1, DV_C), state_all]
    aliases = {}
    if new_prev is not None:
        aliases = {len(args): 1}
        in_specs.append(pl.BlockSpec(memory_space=pl.ANY))
        args.append(new_prev)
    o, ns = pl.pallas_call(
        functools.partial(_decode_hgrn_kernel, n_seq=tile),
        grid=(n // tile,),
        in_specs=in_specs,
        out_specs=[hspec, sspec],
        out_shape=[jax.ShapeDtypeStruct((n, H_C, DV_C), BF16),
                   jax.ShapeDtypeStruct((N_C, n, H_C, DK_C, DV_C), F32)],
        input_output_aliases=aliases,
        compiler_params=_cparams("arbitrary"),
        name="decode_hgrn",
    )(*args)
    return o.reshape(n, W_C), ns


def _rope_tables(pos, rows):
    half = ROT_DIM // 2
    inv_freq = ROPE_THETA ** (-jnp.arange(half, dtype=F32) / half)
    ang = pos.astype(F32)[:, None] * inv_freq[None, :]
    cos, sin = jnp.cos(ang), jnp.sin(ang)
    n = pos.shape[0]
    ones = jnp.ones((n, HD_B - ROT_DIM), F32)
    zeros = jnp.zeros((n, HD_B - ROT_DIM), F32)
    z8 = jnp.zeros((n, half), F32)
    cos_t = jnp.concatenate([cos, cos, ones], axis=1)
    sa_t = jnp.concatenate([-sin, z8, zeros], axis=1)
    sb_t = jnp.concatenate([z8, sin, zeros], axis=1)
    out = []
    for t in (cos_t, sa_t, sb_t):
        t = jnp.concatenate([t, t], axis=1)
        out.append(jnp.broadcast_to(t, (rows, LANES)) if n == 1 else t)
    return out


def kernel(x_prompt, x_sample, c_prompt, c_sample, cache_k, cache_v, page_table, state_conv, state_hgrn,
           norm_g, ada_w, ada_b, w_in_ac, w_out_ac, conv_w, conv_b, ln_g, ln_b, lam_q1, lam_k1, lam_q2,
           lam_k2, subln_g, w_in_c, w_out_c, gn_g, lb_logits, final_g):
    bp, tp = x_prompt.shape[:2]
    bs, ts = x_sample.shape[:2]
    assert ts == 1
    n_pool = cache_k.shape[1]
    n_past = page_table.shape[1] * PAGE_SIZE
    mp = bp * tp

    lb_sm = jax.nn.softmax(lb_logits.astype(F32), axis=0)
    lb_all = jnp.cumsum(lb_sm, axis=0) - lb_sm[0]

    mod = _ada_modulation(jnp.concatenate([c_prompt, c_sample], axis=0), ada_w, ada_b)

    def mods(l):
        out = []
        for j in range(3):
            m = mod[l, :, j * D_MODEL:(j + 1) * D_MODEL]
            out.append((m[:bp].reshape(bp, 1, D_MODEL), m[bp:].reshape(1, bs, D_MODEL)))
        return out

    tabs_p = _rope_tables(jnp.arange(tp), tp)
    tabs_s = _rope_tables(jnp.full((1,), n_past), bs)
    ck4 = cache_k.reshape(N_AC, n_pool, PAGE_SIZE * H_B, 2 * HD_B)
    cv4 = cache_v.reshape(N_AC, n_pool, PAGE_SIZE * H_B, DV_B)

    tiles_p = tp // ROW_TILE
    hp = x_prompt.reshape(mp, D_MODEL)
    hs = x_sample.reshape(bs, D_MODEL)
    cp_l, sp_l, cs_l = [], [], []
    kv_p = kv_s = ss = None
    for l in range(DEPTH):
        (sh_p, sh_s), (sc_p, sc_s), (gt_p, gt_s) = mods(l)
        last = final_g if l == DEPTH - 1 else None
        if l % 2 == 0:
            a = l // 2
            lam_init = 0.8 - 0.6 * math.exp(-0.3 * l)
            lam = (jnp.exp(jnp.sum(lam_q1[a].astype(F32) * lam_k1[a].astype(F32)))
                   - jnp.exp(jnp.sum(lam_q2[a].astype(F32) * lam_k2[a].astype(F32))) + lam_init)
            lam = jnp.full((1, LANES), lam, F32)
            w_in = w_in_ac[a].astype(BF16)
            w_out = w_out_ac[a].astype(BF16)
            q, k, kb, v, vb, gb, u, ga = _inproj_ac(hp, norm_g[l], sc_p, sh_p, w_in, *tabs_p,
                                                    ROW_TILE, tiles_p, a, kv_p)
            kv_p = (k, v)
            qs, k, _, v, _, gbs, us, gas = _inproj_ac(hs, norm_g[l], sc_s, sh_s, w_in, *tabs_s, bs, 1, a, kv_s)
            kv_s = (k, v)
            ob, obs = _attention(q, kb, vb, gb, bp, tp, qs, k, v, gbs, ck4, cv4, a, page_table,
                                 lam, subln_g[a], lam_init)
            y, cst = _prompt_conv(u, ga, conv_w[a], conv_b[a], ln_g[a], ln_b[a], bp, tp)
            hp = _outproj([ob, y], w_out, hp, gt_p, ROW_TILE, tiles_p, last)
            cp_l.append(cst)
            y, cst = _decode_conv(state_conv, a, us, gas, conv_w[a], conv_b[a], ln_g[a], ln_b[a])
            hs = _outproj([obs, y], w_out, hs, gt_s, bs, 1, last)
            cs_l.append(cst)
        else:
            ci = l // 2
            w_in = w_in_c[ci].astype(BF16)
            w_out = w_out_c[ci].astype(BF16)
            o, st = _inproj_hgrn(hp, norm_g[l], sc_p, sh_p, w_in, lb_all[l], gn_g[ci], bp, tp)
            hp = _outproj([o], w_out, hp, gt_p, ROW_TILE, tiles_p, last)
            sp_l.append(st)
            qs, lg, iv, sg = _inproj_c(hs, norm_g[l], sc_s, sh_s, w_in, lb_all[l], bs, 1)
            o, ss = _decode_hgrn(qs, lg, iv, sg, gn_g[ci], state_hgrn, ci, ss)
            hs = _outproj([o], w_out, hs, gt_s, bs, 1, last)

    y_prompt = hp.reshape(bp, tp, D_MODEL)
    y_sample = hs.reshape(bs, ts, D_MODEL)
    return (y_prompt, y_sample,
            kv_p[0].reshape(N_AC, bp, tp, H_B, 2 * HD_B), kv_p[1].reshape(N_AC, bp, tp, H_B, DV_B),
            jnp.stack(cp_l), jnp.stack(sp_l),
            kv_s[0].reshape(N_AC, bs, ts, H_B, 2 * HD_B), kv_s[1].reshape(N_AC, bs, ts, H_B, DV_B),
            jnp.stack(cs_l), ss)
```

```python
import functools
import math

import numpy as np
import jax
import jax.numpy as jnp
from jax import lax
from jax.experimental import pallas as pl
from jax.experimental.pallas import tpu as pltpu

F32 = jnp.float32
BF16 = jnp.bfloat16

D_MODEL = 1024
DEPTH = 4
PAGE_SIZE = 128
N_AC = (DEPTH + 1) // 2
N_C = DEPTH // 2

H_B = 4
HD_B = 64
DV_B = 2 * HD_B
W_B = H_B * DV_B
QK_W = H_B * 2 * HD_B
ROT_DIM = HD_B // 4
ROPE_THETA = 500000.0
NEG_INF = -1e30

W_A = D_MODEL // 2
CONV_W = 31

H_C = 8
DK_C = D_MODEL // H_C
DV_C = D_MODEL // H_C
W_C = H_C * DV_C

EPS = 1e-6
LOG2_E = math.log2(math.e)
Q_SCALE = HD_B ** -0.5 * LOG2_E

AC_IN = 2 * QK_W + 2 * W_B + 3 * W_A
C_IN = 2 * H_C * DK_C + 2 * W_C

LANES = 128
SUBLANES = 8
MXU_WIDTH = 256
VMEM_LIMIT_BYTES = 56 * 1024 * 1024

ROW_TILE = 512
ATTN_TILE = 256
CONV_TILE = 512
CONV_HALO = 32
HGRN_CHUNK = 128
SAMPLE_STATE_TILE = 8


def _cparams(*sem):
    return pltpu.CompilerParams(dimension_semantics=sem, vmem_limit_bytes=VMEM_LIMIT_BYTES)


def _silu(x):
    return x * jax.nn.sigmoid(x)


def _dot(a, b):
    return jnp.dot(a, b, preferred_element_type=F32)


def _dot_nt(a, b):
    return lax.dot_general(a, b, (((1,), (1,)), ((), ())), preferred_element_type=F32)


def _dot_tn(a, b):
    return lax.dot_general(a, b, (((0,), (0,)), ((), ())), preferred_element_type=F32)


def _rms(x, g):
    return x * lax.rsqrt(jnp.mean(x * x, axis=-1, keepdims=True) + EPS) * g


def _ada_kernel(c_ref, w_ref, b_ref, o_ref):
    s = _silu(c_ref[...]).astype(BF16)
    o_ref[...] = _dot(s, w_ref[...].astype(BF16)) + b_ref[...]


def _ada_modulation(c_all, ada_w, ada_b):
    rows = c_all.shape[0]
    return pl.pallas_call(
        _ada_kernel,
        grid=(DEPTH, 3),
        in_specs=[
            pl.BlockSpec((rows, D_MODEL), lambda l, j: (0, 0)),
            pl.BlockSpec((None, D_MODEL, D_MODEL), lambda l, j: (l, 0, j)),
            pl.BlockSpec((None, 1, D_MODEL), lambda l, j: (l, 0, j)),
        ],
        out_specs=pl.BlockSpec((None, rows, D_MODEL), lambda l, j: (l, 0, j)),
        out_shape=jax.ShapeDtypeStruct((DEPTH, rows, 3 * D_MODEL), F32),
        compiler_params=_cparams("arbitrary", "arbitrary"),
        name="ada_modulation",
    )(c_all, ada_w, ada_b.reshape(DEPTH, 1, 3 * D_MODEL))


def _modulated(x_ref, ng_ref, sc_ref, sh_ref):
    h = _rms(x_ref[...], ng_ref[...])
    return (h * (1.0 + sc_ref[...]) + sh_ref[...]).astype(BF16)


def _inproj_ac_kernel(x_ref, ng_ref, sc_ref, sh_ref, w_ref, cos_ref, sa_ref, sb_ref, *rest):
    q_ref, k_ref, kb_ref, v_ref, vb_ref, gb_ref, u_ref, ga_ref, h_scr = rest[-9:]
    h_scr[...] = _modulated(x_ref, ng_ref, sc_ref, sh_ref)

    def proj(s):
        return _dot(h_scr[...], w_ref[:, s * QK_W:(s + 1) * QK_W])

    cos, sa, sb = cos_ref[...], sa_ref[...], sb_ref[...]

    def rope(z):
        return (z * cos + pltpu.roll(z, LANES - ROT_DIM // 2, 1) * sa
                + pltpu.roll(z, ROT_DIM // 2, 1) * sb)

    zq = proj(0)
    for g in range(H_B):
        sl = slice(g * LANES, (g + 1) * LANES)
        q_ref[:, sl] = (rope(zq[:, sl]) * Q_SCALE).astype(BF16)
    tm = x_ref.shape[0]
    zk = proj(1)
    for g in range(H_B):
        sl = slice(g * LANES, (g + 1) * LANES)
        r = rope(zk[:, sl])
        k_ref[pl.ds(g, tm, stride=H_B), :] = r
        kb_ref[:, sl] = r.astype(BF16)
    zv = proj(2)
    for g in range(H_B):
        v_ref[pl.ds(g, tm, stride=H_B), :] = zv[:, g * LANES:(g + 1) * LANES]
    vb_ref[...] = zv.astype(BF16)
    gb_ref[...] = _silu(proj(3)).astype(BF16)
    a_val = proj(4)
    u_ref[...] = a_val * jax.nn.sigmoid(proj(5))
    ga_ref[...] = _silu(proj(6)).astype(BF16)


def _mod_specs(rows_per_mod, tm, tiles_per_mod):
    r = rows_per_mod
    return pl.BlockSpec((None, r, D_MODEL), lambda i: (i // tiles_per_mod, 0, 0))


def _inproj_ac(x, norm_g, scale, shift, w, cos_t, sa_t, sb_t, tm, tiles_per_mod, layer, kv_prev):
    m = x.shape[0]
    tab_tiles = cos_t.shape[0] // tm
    mod_spec = _mod_specs(scale.shape[1], tm, tiles_per_mod)
    row512 = pl.BlockSpec((tm, QK_W), lambda i: (i, 0))
    heads = pl.BlockSpec((None, H_B * tm, LANES), lambda i: (layer, i, 0))
    tab_spec = pl.BlockSpec((tm, LANES), lambda i: (i % tab_tiles, 0))
    out_shape = [
        jax.ShapeDtypeStruct((m, QK_W), BF16),
        jax.ShapeDtypeStruct((N_AC, H_B * m, LANES), F32),
        jax.ShapeDtypeStruct((m, QK_W), BF16),
        jax.ShapeDtypeStruct((N_AC, H_B * m, LANES), F32),
        jax.ShapeDtypeStruct((m, W_B), BF16),
        jax.ShapeDtypeStruct((m, W_B), BF16),
        jax.ShapeDtypeStruct((m, W_A), F32),
        jax.ShapeDtypeStruct((m, W_A), BF16),
    ]
    in_specs = [
        pl.BlockSpec((tm, D_MODEL), lambda i: (i, 0)),
        pl.BlockSpec((1, D_MODEL), lambda i: (0, 0)),
        mod_spec, mod_spec,
        pl.BlockSpec((D_MODEL, AC_IN), lambda i: (0, 0)),
        tab_spec, tab_spec, tab_spec,
    ]
    args = [x, norm_g.reshape(1, D_MODEL), scale, shift, w, cos_t, sa_t, sb_t]
    aliases = {}
    if kv_prev is not None:
        aliases = {len(args): 1, len(args) + 1: 3}
        in_specs += [pl.BlockSpec(memory_space=pl.ANY)] * 2
        args += list(kv_prev)
    return pl.pallas_call(
        _inproj_ac_kernel,
        grid=(m // tm,),
        in_specs=in_specs,
        out_specs=[row512, heads, row512, heads, row512, row512, row512, row512],
        out_shape=out_shape,
        input_output_aliases=aliases,
        scratch_shapes=[pltpu.VMEM((tm, D_MODEL), BF16)],
        compiler_params=_cparams("arbitrary"),
        name="inproj_conv_attn",
    )(*args)


def _inproj_c_kernel(x_ref, ng_ref, sc_ref, sh_ref, w_ref, lb_ref, q_ref, lg_ref, i_ref, sg_ref, h_scr):
    h_scr[...] = _modulated(x_ref, ng_ref, sc_ref, sh_ref)

    def proj(s):
        return _dot(h_scr[...], w_ref[:, s * W_C:(s + 1) * W_C])

    q_ref[...] = _silu(proj(0))
    lb = lb_ref[...]
    lg_ref[...] = jnp.log(lb + (1.0 - lb) * jax.nn.sigmoid(proj(1))) * LOG2_E
    i_ref[...] = proj(2)
    sg_ref[...] = _silu(proj(3)).astype(BF16)


def _inproj_c(x, norm_g, scale, shift, w, lb, tm, tiles_per_mod):
    m = x.shape[0]
    mod_spec = _mod_specs(scale.shape[1], tm, tiles_per_mod)
    row = pl.BlockSpec((tm, W_C), lambda i: (i, 0))
    out_shape = [
        jax.ShapeDtypeStruct((m, W_C), F32),
        jax.ShapeDtypeStruct((m, W_C), F32),
        jax.ShapeDtypeStruct((m, W_C), F32),
        jax.ShapeDtypeStruct((m, W_C), BF16),
    ]
    return pl.pallas_call(
        _inproj_c_kernel,
        grid=(m // tm,),
        in_specs=[
            pl.BlockSpec((tm, D_MODEL), lambda i: (i, 0)),
            pl.BlockSpec((1, D_MODEL), lambda i: (0, 0)),
            mod_spec, mod_spec,
            pl.BlockSpec((D_MODEL, C_IN), lambda i: (0, 0)),
            pl.BlockSpec((1, W_C), lambda i: (0, 0)),
        ],
        out_specs=[row] * 4,
        out_shape=out_shape,
        scratch_shapes=[pltpu.VMEM((tm, D_MODEL), BF16)],
        compiler_params=_cparams("arbitrary"),
        name="inproj_hgrn",
    )(x, norm_g.reshape(1, D_MODEL), scale, shift, w, lb.reshape(1, W_C))


def _outproj_kernel(*refs, n_act, final_norm):
    acts = refs[:n_act]
    w_ref, x_ref, gate_ref = refs[n_act:n_act + 3]
    o_ref = refs[-1]
    acc = None
    lo = 0
    for a_ref in acts:
        width = a_ref.shape[-1]
        part = _dot(a_ref[...], w_ref[lo:lo + width, :])
        acc = part if acc is None else acc + part
        lo += width
    y = x_ref[...] + gate_ref[...] * acc
    if final_norm:
        y = _rms(y, refs[n_act + 3][...])
    o_ref[...] = y


def _outproj(acts, w, x, gate, tm, tiles_per_mod, final_g=None):
    m = x.shape[0]
    mod_spec = _mod_specs(gate.shape[1], tm, tiles_per_mod)
    in_specs = [pl.BlockSpec((tm, a.shape[1]), lambda i: (i, 0)) for a in acts]
    in_specs += [
        pl.BlockSpec((D_MODEL, D_MODEL), lambda i: (0, 0)),
        pl.BlockSpec((tm, D_MODEL), lambda i: (i, 0)),
        mod_spec,
    ]
    args = list(acts) + [w, x, gate]
    if final_g is not None:
        in_specs.append(pl.BlockSpec((1, D_MODEL), lambda i: (0, 0)))
        args.append(final_g.reshape(1, D_MODEL))
    return pl.pallas_call(
        functools.partial(_outproj_kernel, n_act=len(acts), final_norm=final_g is not None),
        grid=(m // tm,),
        in_specs=in_specs,
        out_specs=pl.BlockSpec((tm, D_MODEL), lambda i: (i, 0)),
        out_shape=jax.ShapeDtypeStruct((m, D_MODEL), F32),
        compiler_params=_cparams("arbitrary"),
        name="outproj",
    )(*args)


def _attn_kernel(pt_ref, lam_ref, sgc_ref, sgr_ref, q_ref, k_ref, v_ref, gb_ref,
                 qs_ref, kn_ref, vn_ref, gbs_ref, *rest, tile, lam_init, n_dec, n_pages):
    pages = rest[:2 * n_dec * n_pages]
    o_ref, os_ref, q2_scr, s_a, s_b, m_scr, l_scr, acc_scr = rest[2 * n_dec * n_pages:]
    sg_ref = sgc_ref
    qi = pl.program_id(1)
    heads = [slice(h * LANES, (h + 1) * LANES) for h in range(H_B)]
    lane = lax.broadcasted_iota(jnp.int32, (tile, LANES), 1)
    for h, sl in enumerate(heads):
        q = q_ref[:, sl]
        zero = jnp.zeros_like(q)
        q2_scr[h] = jnp.concatenate([jnp.where(lane < HD_B, q, zero), jnp.where(lane >= HD_B, q, zero)], axis=0)
    m_scr[...] = jnp.full(m_scr.shape, NEG_INF, F32)
    l_scr[...] = jnp.zeros(l_scr.shape, F32)
    acc_scr[...] = jnp.zeros(acc_scr.shape, F32)

    def scores(t, s_scr):
        start = pl.multiple_of(t * tile, tile)
        for h, sl in enumerate(heads):
            s_scr[h] = _dot_nt(k_ref[pl.ds(start, tile), sl], q2_scr[h])

    def softmax_part(s_scr, mask):
        parts = []
        for h in range(H_B):
            sh = s_scr[h]
            if mask is not None:
                sh = jnp.where(mask, sh, NEG_INF)
            m_old = m_scr[h]
            m_new = jnp.maximum(m_old, jnp.max(sh, axis=0, keepdims=True))
            alpha = jnp.exp2(m_old - m_new)
            p = jnp.exp2(sh - m_new)
            m_scr[h] = m_new
            l_scr[h] = alpha * l_scr[h] + jnp.sum(p, axis=0, keepdims=True)
            parts.append((alpha, p.astype(BF16)))
        return parts

    def value_part(t, parts):
        start = pl.multiple_of(t * tile, tile)
        for h, sl in enumerate(heads):
            alpha, p = parts[h]
            acc_scr[h] = alpha * acc_scr[h] + _dot_tn(v_ref[pl.ds(start, tile), sl], p)

    def consume(s_scr, t, mask):
        value_part(t, softmax_part(s_scr, mask))

    odd = qi % 2

    @pl.when(odd == 1)
    def _():
        scores(0, s_b)
        scores(1, s_a)
        consume(s_b, 0, None)

    @pl.when(odd == 0)
    def _():
        scores(0, s_a)

    def body(jj, carry):
        t = odd + 2 * jj
        scores(t + 1, s_b)
        consume(s_a, t, None)
        scores(t + 2, s_a)
        consume(s_b, t + 1, None)
        return carry

    lax.fori_loop(0, qi // 2, body, 0)
    key = lax.broadcasted_iota(jnp.int32, (tile, 2 * tile), 0)
    qry = lax.broadcasted_iota(jnp.int32, (tile, 2 * tile), 1)
    diag = softmax_part(s_a, key <= jnp.where(qry >= tile, qry - tile, qry))
    lam = lam_ref[:, 0:1]
    dec = []
    for r in range(n_dec):
        k_pages = pages[2 * r * n_pages:(2 * r + 1) * n_pages]
        v_pages = pages[(2 * r + 1) * n_pages:(2 * r + 2) * n_pages]
        dec.append(_decode_scores(qs_ref[r].astype(F32), kn_ref[r], k_pages) + (v_pages,))
    value_part(qi, diag)
    for r, (s, s_new, v_pages) in enumerate(dec):
        os_ref[r] = _decode_finish(s, s_new, lam, vn_ref[r], v_pages, sgr_ref[...],
                                   gbs_ref[r].astype(F32), lam_init).astype(BF16)
    for h, sl in enumerate(heads):
        l, acc = l_scr[h], acc_scr[h]
        o = acc[:, :tile] / l[:, :tile] - lam_ref[:, 0:1] * (acc[:, tile:] / l[:, tile:])
        y = o * lax.rsqrt(jnp.mean(o * o, axis=0, keepdims=True) + EPS) * sg_ref[...]
        o_ref[:, sl] = (y.T * (1.0 - lam_init) * gb_ref[:, sl].astype(F32)).astype(BF16)


def _attention(q, kb, vb, gb, batch, seq, qs, kn_all, vn_all, gbs, cache_k4, cache_v4, layer, page_table,
               lam, subln_g, lam_init):
    tile = ATTN_TILE
    nq = seq // tile
    n_seq, n_pages = page_table.shape
    n_dec = n_seq // (batch * nq)
    assert n_dec * batch * nq == n_seq
    q3, k3, v3, g3 = (a.reshape(batch, seq, QK_W) for a in (q, kb, vb, gb))
    qspec = pl.BlockSpec((None, tile, QK_W), lambda b, i, pt: (b, i, 0))
    kvspec = pl.BlockSpec((None, seq, QK_W), lambda b, i, pt: (b, 0, 0))
    rows = pl.BlockSpec((n_dec, 1, QK_W), lambda b, i, pt: (b * nq + i, 0, 0))
    new_kv = pl.BlockSpec((None, n_dec, H_B, LANES), lambda b, i, pt: (layer, b * nq + i, 0, 0))

    def page_spec(r, j):
        return pl.BlockSpec((None, None, H_B * PAGE_SIZE, LANES),
                            lambda b, i, pt: (layer, pt[(b * nq + i) * n_dec + r, j], 0, 0))

    page_specs, page_args = [], []
    for r in range(n_dec):
        for cache in (cache_k4, cache_v4):
            page_specs += [page_spec(r, j) for j in range(n_pages)]
            page_args += [cache] * n_pages
    grid_spec = pltpu.PrefetchScalarGridSpec(
        num_scalar_prefetch=1,
        grid=(batch, nq),
        in_specs=[pl.BlockSpec((1, LANES), lambda b, i, pt: (0, 0)),
                  pl.BlockSpec((DV_B, 1), lambda b, i, pt: (0, 0)),
                  pl.BlockSpec((1, DV_B), lambda b, i, pt: (0, 0)),
                  qspec, kvspec, kvspec, qspec, rows, new_kv, new_kv, rows] + page_specs,
        out_specs=[qspec, rows],
        scratch_shapes=[pltpu.VMEM((H_B, 2 * tile, LANES), BF16),
                        pltpu.VMEM((H_B, tile, 2 * tile), F32),
                        pltpu.VMEM((H_B, tile, 2 * tile), F32),
                        pltpu.VMEM((H_B, 1, 2 * tile), F32),
                        pltpu.VMEM((H_B, 1, 2 * tile), F32),
                        pltpu.VMEM((H_B, DV_B, 2 * tile), F32)],
    )
    r3 = lambda a: a.reshape(n_seq, 1, QK_W)
    h4 = lambda a: a.reshape(N_AC, n_seq, H_B, LANES)
    out, out_s = pl.pallas_call(
        functools.partial(_attn_kernel, tile=tile, lam_init=lam_init, n_dec=n_dec, n_pages=n_pages),
        grid_spec=grid_spec,
        out_shape=[jax.ShapeDtypeStruct((batch, seq, W_B), BF16),
                   jax.ShapeDtypeStruct((n_seq, 1, W_B), BF16)],
        compiler_params=_cparams("arbitrary", "arbitrary"),
        name="diff_attention",
    )(page_table, lam, subln_g.reshape(DV_B, 1), subln_g.reshape(1, DV_B), q3, k3, v3, g3,
      r3(qs), h4(kn_all), h4(vn_all), r3(gbs), *page_args)
    return out.reshape(batch * seq, W_B), out_s.reshape(n_seq, W_B)


def _conv_post(y, ln_g, ln_b, ga):
    yc = y - jnp.mean(y, axis=-1, keepdims=True)
    yn = yc * lax.rsqrt(jnp.mean(yc * yc, axis=-1, keepdims=True) + EPS) * ln_g + ln_b
    return _silu(yn) * ga


def _conv_kernel(u_ref, ga_ref, w_ref, cb_ref, lg_ref, lb_ref, y_ref, st_ref, ext_scr, sh_scr, *, tile, rows):
    t = pl.program_id(1)
    first = CONV_HALO - (CONV_W - 1)
    sh_rows = sh_scr.shape[1]

    @pl.when(t == 0)
    def _():
        ext_scr[0:CONV_HALO, :] = jnp.zeros((CONV_HALO, W_A), F32)

    @pl.when(t > 0)
    def _():
        ext_scr[0:CONV_HALO, :] = ext_scr[tile:tile + CONV_HALO, :]

    ext_scr[CONV_HALO:CONV_HALO + tile, :] = u_ref[...]
    for r in range(1, SUBLANES):
        sh_scr[r - 1] = ext_scr[r:r + sh_rows, :]
    for r0 in range(0, tile, rows):
        acc = jnp.zeros((rows, W_A), F32) + cb_ref[...]
        for j in range(CONV_W):
            off = first + j
            r, base = off % SUBLANES, off - off % SUBLANES + r0
            src = ext_scr[base:base + rows, :] if r == 0 else sh_scr[r - 1, base:base + rows, :]
            acc = acc + src * w_ref[j:j + 1, :]
        y_ref[r0:r0 + rows, :] = _conv_post(acc, lg_ref[...], lb_ref[...],
                                            ga_ref[r0:r0 + rows, :].astype(F32)).astype(BF16)

    @pl.when(t == pl.num_programs(1) - 1)
    def _():
        st_ref[...] = ext_scr[CONV_HALO + tile - (CONV_W - 1):CONV_HALO + tile, :]


def _prompt_conv(u, ga, conv_w, conv_b, ln_g, ln_b, batch, seq):
    tile = CONV_TILE
    u3, g3 = u.reshape(batch, seq, W_A), ga.reshape(batch, seq, W_A)
    tspec = pl.BlockSpec((None, tile, W_A), lambda b, t: (b, t, 0))
    vec = pl.BlockSpec((1, W_A), lambda b, t: (0, 0))
    y, st = pl.pallas_call(
        functools.partial(_conv_kernel, tile=tile, rows=64),
        grid=(batch, seq // tile),
        in_specs=[tspec, tspec, pl.BlockSpec((CONV_W, W_A), lambda b, t: (0, 0)), vec, vec, vec],
        out_specs=[tspec, pl.BlockSpec((None, CONV_W - 1, W_A), lambda b, t: (b, 0, 0))],
        out_shape=[jax.ShapeDtypeStruct((batch, seq, W_A), BF16),
                   jax.ShapeDtypeStruct((batch, CONV_W - 1, W_A), F32)],
        scratch_shapes=[pltpu.VMEM((CONV_HALO + tile, W_A), F32),
                        pltpu.VMEM((SUBLANES - 1, CONV_HALO + tile - SUBLANES, W_A), F32)],
        compiler_params=_cparams("arbitrary", "arbitrary"),
        name="prompt_conv",
    )(u3, g3, conv_w, conv_b.reshape(1, W_A), ln_g.reshape(1, W_A), ln_b.reshape(1, W_A))
    return y.reshape(batch * seq, W_A), st


def _hgrn_tables(chunk):
    levels = int(math.log2(chunk))
    t = np.arange(chunk)[:, None]
    u = np.arange(chunk)[None, :]
    masks = []
    for v in range(levels):
        m = 1 << v
        masks.append((t // (2 * m) == u // (2 * m)) & (t % (2 * m) >= m) & (u % (2 * m) < m))
    return jnp.asarray(u <= t, BF16), jnp.asarray(np.stack(masks), F32), levels


def _split3(x):
    hi = x.astype(BF16)
    r = x - hi.astype(F32)
    mid = r.astype(BF16)
    lo = (r - mid.astype(F32)).astype(BF16)
    return hi, mid, lo


def _hgrn_level_exponent(b, g2, m, row):
    chunk, width = b.shape
    if m == 1:
        return jnp.where((row & 1) != 0, g2, 0.0)
    if 2 * m < SUBLANES:
        b3 = b.reshape(chunk // SUBLANES, SUBLANES, width)
        sub = lax.broadcasted_iota(jnp.int32, b3.shape, 1)
        bm = b3[:, m - 1:m, :]
        for blk in range(1, SUBLANES // (2 * m)):
            bm = jnp.where(sub < blk * 2 * m, bm, b3[:, blk * 2 * m + m - 1:blk * 2 * m + m, :])
    else:
        b3 = b.reshape(chunk // (2 * m), 2 * m, width)
        bm = b3[:, m - 1:m, :]
    return -jnp.abs((b3 - bm).reshape(chunk, width))


def _hgrn_chunk_stages(q_ref, lg_ref, i_ref, sg_ref, tri_ref, msk_ref, gn_ref, o_ref, st_scr, chunk, levels):
    heads = [slice(h * LANES, (h + 1) * LANES) for h in range(H_C)]
    g2 = lg_ref[...]
    q = q_ref[...]
    v = i_ref[...]
    k = 1.0 - jnp.exp2(g2)
    b3 = _dot(tri_ref[...], jnp.concatenate(_split3(g2), axis=1))
    yield
    b = b3[:, :W_C] + b3[:, W_C:2 * W_C] + b3[:, 2 * W_C:]
    row = lax.broadcasted_iota(jnp.int32, (chunk, W_C), 0)
    a = [jnp.zeros((chunk, chunk), F32)] * H_C
    for lv in range(levels):
        m = 1 << lv
        x = (jnp.where((row & m) != 0, q, k) * jnp.exp2(_hgrn_level_exponent(b, g2, m, row))).astype(BF16)
        msk = msk_ref[lv]
        a = [a[h] + msk * _dot_nt(x[:, sl], x[:, sl]) for h, sl in enumerate(heads)]
        yield
    vb = v.astype(BF16)
    qe = (q * jnp.exp2(b)).astype(BF16)
    b_last = b[chunk - 1:chunk, :]
    kd = (k * jnp.exp2(b_last - b)).astype(BF16)
    decay = jnp.exp2(b_last)
    qk = q * k
    for h, sl in enumerate(heads):
        st = st_scr[h]
        o = _dot(a[h].astype(BF16), vb[:, sl]) + jnp.sum(qk[:, sl], axis=1, keepdims=True) * v[:, sl]
        o = o + _dot_nt(qe[:, sl], st.astype(BF16))
        st_scr[h] = st * decay[:, sl] + _dot_tn(vb[:, sl], kd[:, sl])
        o_ref[:, sl] = (_rms(o, gn_ref[...]) * sg_ref[:, sl].astype(F32)).astype(BF16)


def _inproj_hgrn_kernel(x_ref, ng_ref, sc_ref, sh_ref, w_ref, lb_ref, tri_ref, msk_ref, gn_ref,
                        o_ref, s_ref, h_scr, q_scr, lg_scr, iv_scr, sg_scr, st_scr,
                        *, chunk, levels, tiles_per_batch):
    i = pl.program_id(0)
    slot = i % 2
    prev = 1 - slot
    tm = x_ref.shape[0]
    assert tm // chunk == 4

    @pl.when(i == 0)
    def _():
        for scr in (q_scr, lg_scr, iv_scr, sg_scr):
            scr[1] = jnp.zeros(scr.shape[1:], scr.dtype)

    @pl.when((i + tiles_per_batch - 1) % tiles_per_batch == 0)
    def _():
        st_scr[...] = jnp.zeros_like(st_scr)

    h_scr[...] = _modulated(x_ref, ng_ref, sc_ref, sh_ref)
    lb = lb_ref[...]

    sections = [
        (q_scr, lambda z, cols: _silu(z)),
        (lg_scr, lambda z, cols: jnp.log(lb[:, cols] + (1.0 - lb[:, cols]) * jax.nn.sigmoid(z)) * LOG2_E),
        (iv_scr, lambda z, cols: z),
        (sg_scr, lambda z, cols: _silu(z).astype(BF16)),
    ]
    n_pieces = W_C // MXU_WIDTH
    for s, (scr, post) in enumerate(sections):
        rows = pl.ds(s * chunk, chunk)
        stages = _hgrn_chunk_stages(q_scr.at[prev, rows], lg_scr.at[prev, rows], iv_scr.at[prev, rows],
                                    sg_scr.at[prev, rows], tri_ref, msk_ref, gn_ref, o_ref.at[rows],
                                    st_scr, chunk, levels)
        for p in range(n_pieces):
            cols = slice(p * MXU_WIDTH, (p + 1) * MXU_WIDTH)
            z = _dot(h_scr[...], w_ref[:, s * W_C + p * MXU_WIDTH:s * W_C + (p + 1) * MXU_WIDTH])
            scr[slot, :, cols] = post(z, cols)
            next(stages, None)
            next(stages, None)
        for _ in stages:
            pass

    @pl.when(jnp.logical_and(i > 0, i % tiles_per_batch == 0))
    def _():
        for h in range(H_C):
            s_ref[h] = st_scr[h].T


def _inproj_hgrn(x, norm_g, scale, shift, w, lb, gn_g, batch, seq):
    tm, chunk = ROW_TILE, HGRN_CHUNK
    m = x.shape[0]
    nt = m // tm
    tiles_per_batch = seq // tm
    tri, masks, levels = _hgrn_tables(chunk)
    cur = lambda i: jnp.minimum(i, nt - 1)
    done = lambda i: jnp.maximum(i - 1, 0)
    mod_spec = pl.BlockSpec((None, 1, D_MODEL), lambda i: (cur(i) // tiles_per_batch, 0, 0))
    const = lambda shape: pl.BlockSpec(shape, lambda i: (0,) * len(shape))
    o, s = pl.pallas_call(
        functools.partial(_inproj_hgrn_kernel, chunk=chunk, levels=levels, tiles_per_batch=tiles_per_batch),
        grid=(nt + 1,),
        in_specs=[
            pl.BlockSpec((tm, D_MODEL), lambda i: (cur(i), 0)),
            const((1, D_MODEL)),
            mod_spec, mod_spec,
            const((D_MODEL, C_IN)),
            const((1, W_C)),
            const(tri.shape), const(masks.shape), const((1, DV_C)),
        ],
        out_specs=[pl.BlockSpec((tm, W_C), lambda i: (done(i), 0)),
                   pl.BlockSpec((None, H_C, DK_C, DV_C), lambda i: (done(i) // tiles_per_batch, 0, 0, 0))],
        out_shape=[jax.ShapeDtypeStruct((m, W_C), BF16),
                   jax.ShapeDtypeStruct((batch, H_C, DK_C, DV_C), F32)],
        scratch_shapes=[pltpu.VMEM((tm, D_MODEL), BF16),
                        pltpu.VMEM((2, tm, W_C), F32),
                        pltpu.VMEM((2, tm, W_C), F32),
                        pltpu.VMEM((2, tm, W_C), F32),
                        pltpu.VMEM((2, tm, W_C), BF16),
                        pltpu.VMEM((H_C, DV_C, DK_C), F32)],
        compiler_params=_cparams("arbitrary"),
        name="inproj_hgrn_recurrence",
    )(x, norm_g.reshape(1, D_MODEL), scale, shift, w, lb.reshape(1, W_C), tri, masks, gn_g.reshape(1, DV_C))
    return o, s


_DEC_ROWS = 2 * H_B


def _decode_head_rows(pieces):
    row = lax.broadcasted_iota(jnp.int32, (_DEC_ROWS, LANES), 0)
    out = jnp.zeros((_DEC_ROWS, LANES), F32)
    for h in range(H_B):
        out = jnp.where(row // 2 == h, jnp.broadcast_to(pieces[h], (_DEC_ROWS, LANES)), out)
    return out


def _decode_scores(q, kn, k_pages):
    row = lax.broadcasted_iota(jnp.int32, (_DEC_ROWS, LANES), 0)
    lane = lax.broadcasted_iota(jnp.int32, (_DEC_ROWS, LANES), 1)
    qm32 = jnp.where(lane // HD_B == row % 2,
                     _decode_head_rows([q[:, h * LANES:(h + 1) * LANES] for h in range(H_B)]), 0.0)
    qm = qm32.astype(BF16)
    s = jnp.concatenate([_dot_nt(qm, kp[...].astype(BF16)) for kp in k_pages], axis=1)
    key_head = lax.broadcasted_iota(jnp.int32, s.shape, 1) % H_B
    s = jnp.where(key_head == lax.broadcasted_iota(jnp.int32, s.shape, 0) // 2, s, NEG_INF)
    s_new = jnp.sum(qm32 * _decode_head_rows([kn[h:h + 1, :] for h in range(H_B)]), axis=-1, keepdims=True)
    return s, s_new


def _decode_finish(s, s_new, lam, vn, v_pages, subln_row, gb, lam_init):
    page_rows = H_B * PAGE_SIZE
    m = jnp.maximum(jnp.max(s, axis=-1, keepdims=True), s_new)
    p = jnp.exp2(s - m)
    p_new = jnp.exp2(s_new - m)
    l = jnp.sum(p, axis=-1, keepdims=True) + p_new
    comp = lax.broadcasted_iota(jnp.int32, (_DEC_ROWS, 1), 0) % 2
    wgt = jnp.where(comp == 0, 1.0, -lam) / l
    pw = (p * wgt).astype(BF16)
    o8 = (p_new * wgt) * _decode_head_rows([vn[h:h + 1, :] for h in range(H_B)])
    for j, vp in enumerate(v_pages):
        o8 = o8 + _dot(pw[:, j * page_rows:(j + 1) * page_rows], vp[...].astype(BF16))
    out = []
    for h in range(H_B):
        o = o8[2 * h:2 * h + 1, :] + o8[2 * h + 1:2 * h + 2, :]
        out.append(_rms(o, subln_row) * (1.0 - lam_init) * gb[:, h * DV_B:(h + 1) * DV_B])
    return jnp.concatenate(out, axis=1)


def _decode_conv_kernel(st_ref, u_ref, ga_ref, w_ref, cb_ref, lg_ref, lb_ref, y_ref, ns_ref):
    st = st_ref[...]
    u = u_ref[...]
    w = w_ref[...]
    y = jnp.sum(st * w[None, :CONV_W - 1, :], axis=1) + u * w[CONV_W - 1:CONV_W, :] + cb_ref[...]
    y_ref[...] = _conv_post(y, lg_ref[...], lb_ref[...], ga_ref[...].astype(F32)).astype(BF16)
    ns_ref[:, 0:CONV_W - 2, :] = st[:, 1:CONV_W - 1, :]
    ns_ref[:, CONV_W - 2:CONV_W - 1, :] = u[:, None, :]


def _decode_conv(state_all, layer, u, ga, conv_w, conv_b, ln_g, ln_b):
    n = u.shape[0]
    tile = 32
    vec = pl.BlockSpec((1, W_A), lambda i: (0, 0))
    sspec = pl.BlockSpec((tile, CONV_W - 1, W_A), lambda i: (i, 0, 0))
    rspec = pl.BlockSpec((tile, W_A), lambda i: (i, 0))
    return pl.pallas_call(
        _decode_conv_kernel,
        grid=(n // tile,),
        in_specs=[pl.BlockSpec((None, tile, CONV_W - 1, W_A), lambda i: (layer, i, 0, 0)),
                  rspec, rspec, pl.BlockSpec((CONV_W, W_A), lambda i: (0, 0)), vec, vec, vec],
        out_specs=[rspec, sspec],
        out_shape=[jax.ShapeDtypeStruct((n, W_A), BF16),
                   jax.ShapeDtypeStruct((n, CONV_W - 1, W_A), F32)],
        compiler_params=_cparams("arbitrary"),
        name="decode_conv",
    )(state_all, u, ga, conv_w, conv_b.reshape(1, W_A), ln_g.reshape(1, W_A), ln_b.reshape(1, W_A))


def _decode_hgrn_kernel(q_ref, lg_ref, i_ref, sg_ref, gn_ref, s_ref, *rest, n_seq):
    o_ref, ns_ref = rest[-2:]

    def body(n, carry):
        ft = jnp.exp2(lg_ref[n]).T
        kt = 1.0 - ft
        qt = q_ref[n].T
        v8 = i_ref[n]
        rows = []
        for h in range(H_C):
            s_new = ft[:, h:h + 1] * s_ref[n, h] + kt[:, h:h + 1] * v8[h:h + 1, :]
            ns_ref[n, h] = s_new
            rows.append(jnp.sum(qt[:, h:h + 1] * s_new, axis=0, keepdims=True))
        o = jnp.concatenate(rows, axis=0)
        o_ref[n] = (_rms(o, gn_ref[...]) * sg_ref[n].astype(F32)).astype(BF16)
        return carry

    lax.fori_loop(0, n_seq, body, 0)


def _decode_hgrn(qs, lg, iv, sg, gn_g, state_all, layer, new_prev):
    n = qs.shape[0]
    tile = SAMPLE_STATE_TILE
    hspec = pl.BlockSpec((tile, H_C, DV_C), lambda i: (i, 0, 0))
    sspec = pl.BlockSpec((None, tile, H_C, DK_C, DV_C), lambda i: (layer, i, 0, 0, 0))
    h3 = lambda a: a.reshape(n, H_C, DV_C)
    in_specs = [hspec, hspec, hspec, hspec, pl.BlockSpec((1, DV_C), lambda i: (0, 0)), sspec]
    args = [h3(qs), h3(lg), h3(iv), h3(sg), gn_g.reshape(1, DV_C), state_all]
    aliases = {}
    if new_prev is not None:
        aliases = {len(args): 1}
        in_specs.append(pl.BlockSpec(memory_space=pl.ANY))
        args.append(new_prev)
    o, ns = pl.pallas_call(
        functools.partial(_decode_hgrn_kernel, n_seq=tile),
        grid=(n // tile,),
        in_specs=in_specs,
        out_specs=[hspec, sspec],
        out_shape=[jax.ShapeDtypeStruct((n, H_C, DV_C), BF16),
                   jax.ShapeDtypeStruct((N_C, n, H_C, DK_C, DV_C), F32)],
        input_output_aliases=aliases,
        compiler_params=_cparams("arbitrary"),
        name="decode_hgrn",
    )(*args)
    return o.reshape(n, W_C), ns


def _rope_tables(pos, rows):
    half = ROT_DIM // 2
    inv_freq = ROPE_THETA ** (-jnp.arange(half, dtype=F32) / half)
    ang = pos.astype(F32)[:, None] * inv_freq[None, :]
    cos, sin = jnp.cos(ang), jnp.sin(ang)
    n = pos.shape[0]
    ones = jnp.ones((n, HD_B - ROT_DIM), F32)
    zeros = jnp.zeros((n, HD_B - ROT_DIM), F32)
    z8 = jnp.zeros((n, half), F32)
    cos_t = jnp.concatenate([cos, cos, ones], axis=1)
    sa_t = jnp.concatenate([-sin, z8, zeros], axis=1)
    sb_t = jnp.concatenate([z8, sin, zeros], axis=1)
    out = []
    for t in (cos_t, sa_t, sb_t):
        t = jnp.concatenate([t, t], axis=1)
        out.append(jnp.broadcast_to(t, (rows, LANES)) if n == 1 else t)
    return out


def kernel(x_prompt, x_sample, c_prompt, c_sample, cache_k, cache_v, page_table, state_conv, state_hgrn,
           norm_g, ada_w, ada_b, w_in_ac, w_out_ac, conv_w, conv_b, ln_g, ln_b, lam_q1, lam_k1, lam_q2,
           lam_k2, subln_g, w_in_c, w_out_c, gn_g, lb_logits, final_g):
    bp, tp = x_prompt.shape[:2]
    bs, ts = x_sample.shape[:2]
    assert ts == 1
    n_pool = cache_k.shape[1]
    n_past = page_table.shape[1] * PAGE_SIZE
    mp = bp * tp

    lb_sm = jax.nn.softmax(lb_logits.astype(F32), axis=0)
    lb_all = jnp.cumsum(lb_sm, axis=0) - lb_sm[0]

    mod = _ada_modulation(jnp.concatenate([c_prompt, c_sample], axis=0), ada_w, ada_b)

    def mods(l):
        out = []
        for j in range(3):
            m = mod[l, :, j * D_MODEL:(j + 1) * D_MODEL]
            out.append((m[:bp].reshape(bp, 1, D_MODEL), m[bp:].reshape(1, bs, D_MODEL)))
        return out

    tabs_p = _rope_tables(jnp.arange(tp), tp)
    tabs_s = _rope_tables(jnp.full((1,), n_past), bs)
    ck4 = cache_k.reshape(N_AC, n_pool, PAGE_SIZE * H_B, 2 * HD_B)
    cv4 = cache_v.reshape(N_AC, n_pool, PAGE_SIZE * H_B, DV_B)

    tiles_p = tp // ROW_TILE
    hp = x_prompt.reshape(mp, D_MODEL)
    hs = x_sample.reshape(bs, D_MODEL)
    cp_l, sp_l, cs_l = [], [], []
    kv_p = kv_s = ss = None
    for l in range(DEPTH):
        (sh_p, sh_s), (sc_p, sc_s), (gt_p, gt_s) = mods(l)
        last = final_g if l == DEPTH - 1 else None
        if l % 2 == 0:
            a = l // 2
            lam_init = 0.8 - 0.6 * math.exp(-0.3 * l)
            lam = (jnp.exp(jnp.sum(lam_q1[a].astype(F32) * lam_k1[a].astype(F32)))
                   - jnp.exp(jnp.sum(lam_q2[a].astype(F32) * lam_k2[a].astype(F32))) + lam_init)
            lam = jnp.full((1, LANES), lam, F32)
            w_in = w_in_ac[a].astype(BF16)
            w_out = w_out_ac[a].astype(BF16)
            q, k, kb, v, vb, gb, u, ga = _inproj_ac(hp, norm_g[l], sc_p, sh_p, w_in, *tabs_p,
                                                    ROW_TILE, tiles_p, a, kv_p)
            kv_p = (k, v)
            qs, k, _, v, _, gbs, us, gas = _inproj_ac(hs, norm_g[l], sc_s, sh_s, w_in, *tabs_s, bs, 1, a, kv_s)
            kv_s = (k, v)
            ob, obs = _attention(q, kb, vb, gb, bp, tp, qs, k, v, gbs, ck4, cv4, a, page_table,
                                 lam, subln_g[a], lam_init)
            y, cst = _prompt_conv(u, ga, conv_w[a], conv_b[a], ln_g[a], ln_b[a], bp, tp)
            hp = _outproj([ob, y], w_out, hp, gt_p, ROW_TILE, tiles_p, last)
            cp_l.append(cst)
            y, cst = _decode_conv(state_conv, a, us, gas, conv_w[a], conv_b[a], ln_g[a], ln_b[a])
            hs = _outproj([obs, y], w_out, hs, gt_s, bs, 1, last)
            cs_l.append(cst)
        else:
            ci = l // 2
            w_in = w_in_c[ci].astype(BF16)
            w_out = w_out_c[ci].astype(BF16)
            o, st = _inproj_hgrn(hp, norm_g[l], sc_p, sh_p, w_in, lb_all[l], gn_g[ci], bp, tp)
            hp = _outproj([o], w_out, hp, gt_p, ROW_TILE, tiles_p, last)
            sp_l.append(st)
            qs, lg, iv, sg = _inproj_c(hs, norm_g[l], sc_s, sh_s, w_in, lb_all[l], bs, 1)
            o, ss = _decode_hgrn(qs, lg, iv, sg, gn_g[ci], state_hgrn, ci, ss)
            hs = _outproj([o], w_out, hs, gt_s, bs, 1, last)

    y_prompt = hp.reshape(bp, tp, D_MODEL)
    y_sample = hs.reshape(bs, ts, D_MODEL)
    return (y_prompt, y_sample,
            kv_p[0].reshape(N_AC, bp, tp, H_B, 2 * HD_B), kv_p[1].reshape(N_AC, bp, tp, H_B, DV_B),
            jnp.stack(cp_l), jnp.stack(sp_l),
            kv_s[0].reshape(N_AC, bs, ts, H_B, 2 * HD_B), kv_s[1].reshape(N_AC, bs, ts, H_B, DV_B),
            jnp.stack(cs_l), ss)
```

```python
import functools
import math

import numpy as np
import jax
import jax.numpy as jnp
from jax import lax
from jax.experimental import pallas as pl
from jax.experimental.pallas import tpu as pltpu

F32 = jnp.float32
BF16 = jnp.bfloat16

D_MODEL = 1024
DEPTH = 4
PAGE_SIZE = 128
N_AC = (DEPTH + 1) // 2
N_C = DEPTH // 2

H_B = 4
HD_B = 64
DV_B = 2 * HD_B
W_B = H_B * DV_B
QK_W = H_B * 2 * HD_B
ROT_DIM = HD_B // 4
ROPE_THETA = 500000.0
NEG_INF = -1e30

W_A = D_MODEL // 2
CONV_W = 31

H_C = 8
DK_C = D_MODEL // H_C
DV_C = D_MODEL // H_C
W_C = H_C * DV_C

EPS = 1e-6
LOG2_E = math.log2(math.e)
Q_SCALE = HD_B ** -0.5 * LOG2_E

AC_IN = 2 * QK_W + 2 * W_B + 3 * W_A
C_IN = 2 * H_C * DK_C + 2 * W_C

LANES = 128
SUBLANES = 8
MXU_WIDTH = 256
VMEM_LIMIT_BYTES = 56 * 1024 * 1024

ROW_TILE = 512
ATTN_TILE = 256
CONV_TILE = 512
CONV_HALO = 32
HGRN_CHUNK = 128
SAMPLE_STATE_TILE = 8


def _cparams(*sem):
    return pltpu.CompilerParams(dimension_semantics=sem, vmem_limit_bytes=VMEM_LIMIT_BYTES)


def _silu(x):
    return x * jax.nn.sigmoid(x)


def _dot(a, b):
    return jnp.dot(a, b, preferred_element_type=F32)


def _dot_nt(a, b):
    return lax.dot_general(a, b, (((1,), (1,)), ((), ())), preferred_element_type=F32)


def _dot_tn(a, b):
    return lax.dot_general(a, b, (((0,), (0,)), ((), ())), preferred_element_type=F32)


def _rms(x, g):
    return x * lax.rsqrt(jnp.mean(x * x, axis=-1, keepdims=True) + EPS) * g


def _ada_kernel(c_ref, w_ref, b_ref, o_ref):
    s = _silu(c_ref[...]).astype(BF16)
    o_ref[...] = _dot(s, w_ref[...].astype(BF16)) + b_ref[...]


def _ada_modulation(c_all, ada_w, ada_b):
    rows = c_all.shape[0]
    return pl.pallas_call(
        _ada_kernel,
        grid=(DEPTH, 3),
        in_specs=[
            pl.BlockSpec((rows, D_MODEL), lambda l, j: (0, 0)),
            pl.BlockSpec((None, D_MODEL, D_MODEL), lambda l, j: (l, 0, j)),
            pl.BlockSpec((None, 1, D_MODEL), lambda l, j: (l, 0, j)),
        ],
        out_specs=pl.BlockSpec((None, rows, D_MODEL), lambda l, j: (l, 0, j)),
        out_shape=jax.ShapeDtypeStruct((DEPTH, rows, 3 * D_MODEL), F32),
        compiler_params=_cparams("arbitrary", "arbitrary"),
        name="ada_modulation",
    )(c_all, ada_w, ada_b.reshape(DEPTH, 1, 3 * D_MODEL))


def _modulated(x_ref, ng_ref, sc_ref, sh_ref):
    h = _rms(x_ref[...], ng_ref[...])
    return (h * (1.0 + sc_ref[...]) + sh_ref[...]).astype(BF16)


def _inproj_ac_kernel(x_ref, ng_ref, sc_ref, sh_ref, w_ref, cos_ref, sa_ref, sb_ref, *rest):
    q_ref, k_ref, kb_ref, v_ref, vb_ref, gb_ref, u_ref, ga_ref, h_scr = rest[-9:]
    h_scr[...] = _modulated(x_ref, ng_ref, sc_ref, sh_ref)

    def proj(s):
        return _dot(h_scr[...], w_ref[:, s * QK_W:(s + 1) * QK_W])

    cos, sa, sb = cos_ref[...], sa_ref[...], sb_ref[...]

    def rope(z):
        return (z * cos + pltpu.roll(z, LANES - ROT_DIM // 2, 1) * sa
                + pltpu.roll(z, ROT_DIM // 2, 1) * sb)

    zq = proj(0)
    for g in range(H_B):
        sl = slice(g * LANES, (g + 1) * LANES)
        q_ref[:, sl] = (rope(zq[:, sl]) * Q_SCALE).astype(BF16)
    tm = x_ref.shape[0]
    zk = proj(1)
    for g in range(H_B):
        sl = slice(g * LANES, (g + 1) * LANES)
        r = rope(zk[:, sl])
        k_ref[pl.ds(g, tm, stride=H_B), :] = r
        kb_ref[:, sl] = r.astype(BF16)
    zv = proj(2)
    for g in range(H_B):
        v_ref[pl.ds(g, tm, stride=H_B), :] = zv[:, g * LANES:(g + 1) * LANES]
    vb_ref[...] = zv.astype(BF16)
    gb_ref[...] = _silu(proj(3)).astype(BF16)
    a_val = proj(4)
    u_ref[...] = a_val * jax.nn.sigmoid(proj(5))
    ga_ref[...] = _silu(proj(6)).astype(BF16)


def _mod_specs(rows_per_mod, tm, tiles_per_mod):
    r = rows_per_mod
    return pl.BlockSpec((None, r, D_MODEL), lambda i: (i // tiles_per_mod, 0, 0))


def _inproj_ac(x, norm_g, scale, shift, w, cos_t, sa_t, sb_t, tm, tiles_per_mod, layer, kv_prev):
    m = x.shape[0]
    tab_tiles = cos_t.shape[0] // tm
    mod_spec = _mod_specs(scale.shape[1], tm, tiles_per_mod)
    row512 = pl.BlockSpec((tm, QK_W), lambda i: (i, 0))
    heads = pl.BlockSpec((None, H_B * tm, LANES), lambda i: (layer, i, 0))
    tab_spec = pl.BlockSpec((tm, LANES), lambda i: (i % tab_tiles, 0))
    out_shape = [
        jax.ShapeDtypeStruct((m, QK_W), BF16),
        jax.ShapeDtypeStruct((N_AC, H_B * m, LANES), F32),
        jax.ShapeDtypeStruct((m, QK_W), BF16),
        jax.ShapeDtypeStruct((N_AC, H_B * m, LANES), F32),
        jax.ShapeDtypeStruct((m, W_B), BF16),
        jax.ShapeDtypeStruct((m, W_B), BF16),
        jax.ShapeDtypeStruct((m, W_A), F32),
        jax.ShapeDtypeStruct((m, W_A), BF16),
    ]
    in_specs = [
        pl.BlockSpec((tm, D_MODEL), lambda i: (i, 0)),
        pl.BlockSpec((1, D_MODEL), lambda i: (0, 0)),
        mod_spec, mod_spec,
        pl.BlockSpec((D_MODEL, AC_IN), lambda i: (0, 0)),
        tab_spec, tab_spec, tab_spec,
    ]
    args = [x, norm_g.reshape(1, D_MODEL), scale, shift, w, cos_t, sa_t, sb_t]
    aliases = {}
    if kv_prev is not None:
        aliases = {len(args): 1, len(args) + 1: 3}
        in_specs += [pl.BlockSpec(memory_space=pl.ANY)] * 2
        args += list(kv_prev)
    return pl.pallas_call(
        _inproj_ac_kernel,
        grid=(m // tm,),
        in_specs=in_specs,
        out_specs=[row512, heads, row512, heads, row512, row512, row512, row512],
        out_shape=out_shape,
        input_output_aliases=aliases,
        scratch_shapes=[pltpu.VMEM((tm, D_MODEL), BF16)],
        compiler_params=_cparams("arbitrary"),
        name="inproj_conv_attn",
    )(*args)


def _inproj_c_kernel(x_ref, ng_ref, sc_ref, sh_ref, w_ref, lb_ref, q_ref, lg_ref, i_ref, sg_ref, h_scr):
    h_scr[...] = _modulated(x_ref, ng_ref, sc_ref, sh_ref)

    def proj(s):
        return _dot(h_scr[...], w_ref[:, s * W_C:(s + 1) * W_C])

    q_ref[...] = _silu(proj(0))
    lb = lb_ref[...]
    lg_ref[...] = jnp.log(lb + (1.0 - lb) * jax.nn.sigmoid(proj(1))) * LOG2_E
    i_ref[...] = proj(2)
    sg_ref[...] = _silu(proj(3)).astype(BF16)


def _inproj_c(x, norm_g, scale, shift, w, lb, tm, tiles_per_mod):
    m = x.shape[0]
    mod_spec = _mod_specs(scale.shape[1], tm, tiles_per_mod)
    row = pl.BlockSpec((tm, W_C), lambda i: (i, 0))
    out_shape = [
        jax.ShapeDtypeStruct((m, W_C), F32),
        jax.ShapeDtypeStruct((m, W_C), F32),
        jax.ShapeDtypeStruct((m, W_C), F32),
        jax.ShapeDtypeStruct((m, W_C), BF16),
    ]
    return pl.pallas_call(
        _inproj_c_kernel,
        grid=(m // tm,),
        in_specs=[
            pl.BlockSpec((tm, D_MODEL), lambda i: (i, 0)),
            pl.BlockSpec((1, D_MODEL), lambda i: (0, 0)),
            mod_spec, mod_spec,
            pl.BlockSpec((D_MODEL, C_IN), lambda i: (0, 0)),
            pl.BlockSpec((1, W_C), lambda i: (0, 0)),
        ],
        out_specs=[row] * 4,
        out_shape=out_shape,
        scratch_shapes=[pltpu.VMEM((tm, D_MODEL), BF16)],
        compiler_params=_cparams("arbitrary"),
        name="inproj_hgrn",
    )(x, norm_g.reshape(1, D_MODEL), scale, shift, w, lb.reshape(1, W_C))


def _outproj_kernel(*refs, n_act, final_norm):
    acts = refs[:n_act]
    w_ref, x_ref, gate_ref = refs[n_act:n_act + 3]
    o_ref = refs[-1]
    acc = None
    lo = 0
    for a_ref in acts:
        width = a_ref.shape[-1]
        part = _dot(a_ref[...], w_ref[lo:lo + width, :])
        acc = part if acc is None else acc + part
        lo += width
    y = x_ref[...] + gate_ref[...] * acc
    if final_norm:
        y = _rms(y, refs[n_act + 3][...])
    o_ref[...] = y


def _outproj(acts, w, x, gate, tm, tiles_per_mod, final_g=None):
    m = x.shape[0]
    mod_spec = _mod_specs(gate.shape[1], tm, tiles_per_mod)
    in_specs = [pl.BlockSpec((tm, a.shape[1]), lambda i: (i, 0)) for a in acts]
    in_specs += [
        pl.BlockSpec((D_MODEL, D_MODEL), lambda i: (0, 0)),
        pl.BlockSpec((tm, D_MODEL), lambda i: (i, 0)),
        mod_spec,
    ]
    args = list(acts) + [w, x, gate]
    if final_g is not None:
        in_specs.append(pl.BlockSpec((1, D_MODEL), lambda i: (0, 0)))
        args.append(final_g.reshape(1, D_MODEL))
    return pl.pallas_call(
        functools.partial(_outproj_kernel, n_act=len(acts), final_norm=final_g is not None),
        grid=(m // tm,),
        in_specs=in_specs,
        out_specs=pl.BlockSpec((tm, D_MODEL), lambda i: (i, 0)),
        out_shape=jax.ShapeDtypeStruct((m, D_MODEL), F32),
        compiler_params=_cparams("arbitrary"),
        name="outproj",
    )(*args)


def _attn_kernel(pt_ref, lam_ref, sgc_ref, sgr_ref, q_ref, k_ref, v_ref, gb_ref,
                 qs_ref, kn_ref, vn_ref, gbs_ref, *rest, tile, lam_init, n_dec, n_pages):
    pages = rest[:2 * n_dec * n_pages]
    o_ref, os_ref, q2_scr, s_a, s_b, m_scr, l_scr, acc_scr = rest[2 * n_dec * n_pages:]
    sg_ref = sgc_ref
    qi = pl.program_id(1)
    heads = [slice(h * LANES, (h + 1) * LANES) for h in range(H_B)]
    lane = lax.broadcasted_iota(jnp.int32, (tile, LANES), 1)
    for h, sl in enumerate(heads):
        q = q_ref[:, sl]
        zero = jnp.zeros_like(q)
        q2_scr[h] = jnp.concatenate([jnp.where(lane < HD_B, q, zero), jnp.where(lane >= HD_B, q, zero)], axis=0)
    m_scr[...] = jnp.full(m_scr.shape, NEG_INF, F32)
    l_scr[...] = jnp.zeros(l_scr.shape, F32)
    acc_scr[...] = jnp.zeros(acc_scr.shape, F32)

    def scores(t, s_scr):
        start = pl.multiple_of(t * tile, tile)
        for h, sl in enumerate(heads):
            s_scr[h] = _dot_nt(k_ref[pl.ds(start, tile), sl], q2_scr[h])

    def softmax_part(s_scr, mask):
        parts = []
        for h in range(H_B):
            sh = s_scr[h]
            if mask is not None:
                sh = jnp.where(mask, sh, NEG_INF)
            m_old = m_scr[h]
            m_new = jnp.maximum(m_old, jnp.max(sh, axis=0, keepdims=True))
            alpha = jnp.exp2(m_old - m_new)
            p = jnp.exp2(sh - m_new)
            m_scr[h] = m_new
            l_scr[h] = alpha * l_scr[h] + jnp.sum(p, axis=0, keepdims=True)
            parts.append((alpha, p.astype(BF16)))
        return parts

    def value_part(t, parts):
        start = pl.multiple_of(t * tile, tile)
        for h, sl in enumerate(heads):
            alpha, p = parts[h]
            acc_scr[h] = alpha * acc_scr[h] + _dot_tn(v_ref[pl.ds(start, tile), sl], p)

    def consume(s_scr, t, mask):
        value_part(t, softmax_part(s_scr, mask))

    odd = qi % 2

    @pl.when(odd == 1)
    def _():
        scores(0, s_b)
        scores(1, s_a)
        consume(s_b, 0, None)

    @pl.when(odd == 0)
    def _():
        scores(0, s_a)

    def body(jj, carry):
        t = odd + 2 * jj
        scores(t + 1, s_b)
        consume(s_a, t, None)
        scores(t + 2, s_a)
        consume(s_b, t + 1, None)
        return carry

    lax.fori_loop(0, qi // 2, body, 0)
    key = lax.broadcasted_iota(jnp.int32, (tile, 2 * tile), 0)
    qry = lax.broadcasted_iota(jnp.int32, (tile, 2 * tile), 1)
    diag = softmax_part(s_a, key <= jnp.where(qry >= tile, qry - tile, qry))
    lam = lam_ref[:, 0:1]
    dec = []
    for r in range(n_dec):
        k_pages = pages[2 * r * n_pages:(2 * r + 1) * n_pages]
        v_pages = pages[(2 * r + 1) * n_pages:(2 * r + 2) * n_pages]
        dec.append(_decode_scores(qs_ref[r].astype(F32), kn_ref[r], k_pages) + (v_pages,))
    value_part(qi, diag)
    for r, (s, s_new, v_pages) in enumerate(dec):
        os_ref[r] = _decode_finish(s, s_new, lam, vn_ref[r], v_pages, sgr_ref[...],
                                   gbs_ref[r].astype(F32), lam_init).astype(BF16)
    for h, sl in enumerate(heads):
        l, acc = l_scr[h], acc_scr[h]
        o = acc[:, :tile] / l[:, :tile] - lam_ref[:, 0:1] * (acc[:, tile:] / l[:, tile:])
        y = o * lax.rsqrt(jnp.mean(o * o, axis=0, keepdims=True) + EPS) * sg_ref[...]
        o_ref[:, sl] = (y.T * (1.0 - lam_init) * gb_ref[:, sl].astype(F32)).astype(BF16)


def _attention(q, kb, vb, gb, batch, seq, qs, kn_all, vn_all, gbs, cache_k4, cache_v4, layer, page_table,
               lam, subln_g, lam_init):
    tile = ATTN_TILE
    nq = seq // tile
    n_seq, n_pages = page_table.shape
    n_dec = n_seq // (batch * nq)
    assert n_dec * batch * nq == n_seq
    q3, k3, v3, g3 = (a.reshape(batch, seq, QK_W) for a in (q, kb, vb, gb))
    qspec = pl.BlockSpec((None, tile, QK_W), lambda b, i, pt: (b, i, 0))
    kvspec = pl.BlockSpec((None, seq, QK_W), lambda b, i, pt: (b, 0, 0))
    rows = pl.BlockSpec((n_dec, 1, QK_W), lambda b, i, pt: (b * nq + i, 0, 0))
    new_kv = pl.BlockSpec((None, n_dec, H_B, LANES), lambda b, i, pt: (layer, b * nq + i, 0, 0))

    def page_spec(r, j):
        return pl.BlockSpec((None, None, H_B * PAGE_SIZE, LANES),
                            lambda b, i, pt: (layer, pt[(b * nq + i) * n_dec + r, j], 0, 0))

    page_specs, page_args = [], []
    for r in range(n_dec):
        for cache in (cache_k4, cache_v4):
            page_specs += [page_spec(r, j) for j in range(n_pages)]
            page_args += [cache] * n_pages
    grid_spec = pltpu.PrefetchScalarGridSpec(
        num_scalar_prefetch=1,
        grid=(batch, nq),
        in_specs=[pl.BlockSpec((1, LANES), lambda b, i, pt: (0, 0)),
                  pl.BlockSpec((DV_B, 1), lambda b, i, pt: (0, 0)),
                  pl.BlockSpec((1, DV_B), lambda b, i, pt: (0, 0)),
                  qspec, kvspec, kvspec, qspec, rows, new_kv, new_kv, rows] + page_specs,
        out_specs=[qspec, rows],
        scratch_shapes=[pltpu.VMEM((H_B, 2 * tile, LANES), BF16),
                        pltpu.VMEM((H_B, tile, 2 * tile), F32),
                        pltpu.VMEM((H_B, tile, 2 * tile), F32),
                        pltpu.VMEM((H_B, 1, 2 * tile), F32),
                        pltpu.VMEM((H_B, 1, 2 * tile), F32),
                        pltpu.VMEM((H_B, DV_B, 2 * tile), F32)],
    )
    r3 = lambda a: a.reshape(n_seq, 1, QK_W)
    h4 = lambda a: a.reshape(N_AC, n_seq, H_B, LANES)
    out, out_s = pl.pallas_call(
        functools.partial(_attn_kernel, tile=tile, lam_init=lam_init, n_dec=n_dec, n_pages=n_pages),
        grid_spec=grid_spec,
        out_shape=[jax.ShapeDtypeStruct((batch, seq, W_B), BF16),
                   jax.ShapeDtypeStruct((n_seq, 1, W_B), BF16)],
        compiler_params=_cparams("arbitrary", "arbitrary"),
        name="diff_attention",
    )(page_table, lam, subln_g.reshape(DV_B, 1), subln_g.reshape(1, DV_B), q3, k3, v3, g3,
      r3(qs), h4(kn_all), h4(vn_all), r3(gbs), *page_args)
    return out.reshape(batch * seq, W_B), out_s.reshape(n_seq, W_B)


def _conv_post(y, ln_g, ln_b, ga):
    yc = y - jnp.mean(y, axis=-1, keepdims=True)
    yn = yc * lax.rsqrt(jnp.mean(yc * yc, axis=-1, keepdims=True) + EPS) * ln_g + ln_b
    return _silu(yn) * ga


def _conv_kernel(u_ref, ga_ref, w_ref, cb_ref, lg_ref, lb_ref, y_ref, st_ref, ext_scr, sh_scr, *, tile, rows):
    t = pl.program_id(1)
    first = CONV_HALO - (CONV_W - 1)
    sh_rows = sh_scr.shape[1]

    @pl.when(t == 0)
    def _():
        ext_scr[0:CONV_HALO, :] = jnp.zeros((CONV_HALO, W_A), F32)

    @pl.when(t > 0)
    def _():
        ext_scr[0:CONV_HALO, :] = ext_scr[tile:tile + CONV_HALO, :]

    ext_scr[CONV_HALO:CONV_HALO + tile, :] = u_ref[...]
    for r in range(1, SUBLANES):
        sh_scr[r - 1] = ext_scr[r:r + sh_rows, :]
    for r0 in range(0, tile, rows):
        acc = jnp.zeros((rows, W_A), F32) + cb_ref[...]
        for j in range(CONV_W):
            off = first + j
            r, base = off % SUBLANES, off - off % SUBLANES + r0
            src = ext_scr[base:base + rows, :] if r == 0 else sh_scr[r - 1, base:base + rows, :]
            acc = acc + src * w_ref[j:j + 1, :]
        y_ref[r0:r0 + rows, :] = _conv_post(acc, lg_ref[...], lb_ref[...],
                                            ga_ref[r0:r0 + rows, :].astype(F32)).astype(BF16)

    @pl.when(t == pl.num_programs(1) - 1)
    def _():
        st_ref[...] = ext_scr[CONV_HALO + tile - (CONV_W - 1):CONV_HALO + tile, :]


def _prompt_conv(u, ga, conv_w, conv_b, ln_g, ln_b, batch, seq):
    tile = CONV_TILE
    u3, g3 = u.reshape(batch, seq, W_A), ga.reshape(batch, seq, W_A)
    tspec = pl.BlockSpec((None, tile, W_A), lambda b, t: (b, t, 0))
    vec = pl.BlockSpec((1, W_A), lambda b, t: (0, 0))
    y, st = pl.pallas_call(
        functools.partial(_conv_kernel, tile=tile, rows=64),
        grid=(batch, seq // tile),
        in_specs=[tspec, tspec, pl.BlockSpec((CONV_W, W_A), lambda b, t: (0, 0)), vec, vec, vec],
        out_specs=[tspec, pl.BlockSpec((None, CONV_W - 1, W_A), lambda b, t: (b, 0, 0))],
        out_shape=[jax.ShapeDtypeStruct((batch, seq, W_A), BF16),
                   jax.ShapeDtypeStruct((batch, CONV_W - 1, W_A), F32)],
        scratch_shapes=[pltpu.VMEM((CONV_HALO + tile, W_A), F32),
                        pltpu.VMEM((SUBLANES - 1, CONV_HALO + tile - SUBLANES, W_A), F32)],
        compiler_params=_cparams("arbitrary", "arbitrary"),
        name="prompt_conv",
    )(u3, g3, conv_w, conv_b.reshape(1, W_A), ln_g.reshape(1, W_A), ln_b.reshape(1, W_A))
    return y.reshape(batch * seq, W_A), st


def _hgrn_tables(chunk):
    levels = int(math.log2(chunk))
    t = np.arange(chunk)[:, None]
    u = np.arange(chunk)[None, :]
    masks = []
    for v in range(levels):
        m = 1 << v
        masks.append((t // (2 * m) == u // (2 * m)) & (t % (2 * m) >= m) & (u % (2 * m) < m))
    return jnp.asarray(u <= t, BF16), jnp.asarray(np.stack(masks), BF16), levels


def _split3(x):
    hi = x.astype(BF16)
    r = x - hi.astype(F32)
    mid = r.astype(BF16)
    lo = (r - mid.astype(F32)).astype(BF16)
    return hi, mid, lo


def _hgrn_level_exponent(b, g2, m, row):
    chunk, width = b.shape
    if m == 1:
        return jnp.where((row & 1) != 0, g2, 0.0)
    if 2 * m < SUBLANES:
        b3 = b.reshape(chunk // SUBLANES, SUBLANES, width)
        sub = lax.broadcasted_iota(jnp.int32, b3.shape, 1)
        bm = b3[:, m - 1:m, :]
        for blk in range(1, SUBLANES // (2 * m)):
            bm = jnp.where(sub < blk * 2 * m, bm, b3[:, blk * 2 * m + m - 1:blk * 2 * m + m, :])
    else:
        b3 = b.reshape(chunk // (2 * m), 2 * m, width)
        bm = b3[:, m - 1:m, :]
    d = lax.bitcast_convert_type((b3 - bm).reshape(chunk, width), jnp.uint32)
    return lax.bitcast_convert_type(d | jnp.uint32(0x80000000), F32)


def _hgrn_chunk_stages(q_ref, lg_ref, i_ref, sg_ref, tri_ref, msk_ref, gn_ref, o_ref, st_scr, chunk, levels):
    heads = [slice(h * LANES, (h + 1) * LANES) for h in range(H_C)]
    g2 = lg_ref[...]
    q = q_ref[...]
    v = i_ref[...]
    k = 1.0 - jnp.exp2(g2)
    b3 = _dot(tri_ref[...], jnp.concatenate(_split3(g2), axis=1))
    yield
    b = b3[:, :W_C] + b3[:, W_C:2 * W_C] + b3[:, 2 * W_C:]
    row = lax.broadcasted_iota(jnp.int32, (chunk, W_C), 0)
    qb, kb = q.astype(BF16), k.astype(BF16)
    a = [jnp.zeros((chunk, chunk), BF16)] * H_C
    for lv in range(levels):
        m = 1 << lv
        x = jnp.where((row & m) != 0, qb, kb) * jnp.exp2(_hgrn_level_exponent(b, g2, m, row).astype(BF16))
        msk = msk_ref[lv]
        a = [a[h] + msk * _dot_nt(x[:, sl], x[:, sl]).astype(BF16) for h, sl in enumerate(heads)]
        yield
    vb = v.astype(BF16)
    qe = (q * jnp.exp2(b)).astype(BF16)
    b_last = b[chunk - 1:chunk, :]
    kd = (k * jnp.exp2(b_last - b)).astype(BF16)
    decay = jnp.exp2(b_last)
    qk = q * k
    for h, sl in enumerate(heads):
        st = st_scr[h]
        o = _dot(a[h], vb[:, sl]) + jnp.sum(qk[:, sl], axis=1, keepdims=True) * v[:, sl]
        o = o + _dot_nt(qe[:, sl], st.astype(BF16))
        st_scr[h] = st * decay[:, sl] + _dot_tn(vb[:, sl], kd[:, sl])
        o_ref[:, sl] = (_rms(o, gn_ref[...]) * sg_ref[:, sl].astype(F32)).astype(BF16)


def _inproj_hgrn_kernel(x_ref, ng_ref, sc_ref, sh_ref, w_ref, lb_ref, tri_ref, msk_ref, gn_ref,
                        o_ref, s_ref, h_scr, q_scr, lg_scr, iv_scr, sg_scr, st_scr,
                        *, chunk, levels, tiles_per_batch):
    i = pl.program_id(0)
    slot = i % 2
    prev = 1 - slot
    tm = x_ref.shape[0]
    assert tm // chunk == 4

    @pl.when(i == 0)
    def _():
        for scr in (q_scr, lg_scr, iv_scr, sg_scr):
            scr[1] = jnp.zeros(scr.shape[1:], scr.dtype)

    @pl.when((i + tiles_per_batch - 1) % tiles_per_batch == 0)
    def _():
        st_scr[...] = jnp.zeros_like(st_scr)

    h_scr[...] = _modulated(x_ref, ng_ref, sc_ref, sh_ref)
    lb = lb_ref[...]

    sections = [
        (q_scr, lambda z, cols: _silu(z)),
        (lg_scr, lambda z, cols: jnp.log(lb[:, cols] + (1.0 - lb[:, cols]) * jax.nn.sigmoid(z)) * LOG2_E),
        (iv_scr, lambda z, cols: z),
        (sg_scr, lambda z, cols: _silu(z).astype(BF16)),
    ]
    n_pieces = W_C // MXU_WIDTH
    for s, (scr, post) in enumerate(sections):
        rows = pl.ds(s * chunk, chunk)
        stages = _hgrn_chunk_stages(q_scr.at[prev, rows], lg_scr.at[prev, rows], iv_scr.at[prev, rows],
                                    sg_scr.at[prev, rows], tri_ref, msk_ref, gn_ref, o_ref.at[rows],
                                    st_scr, chunk, levels)
        for p in range(n_pieces):
            cols = slice(p * MXU_WIDTH, (p + 1) * MXU_WIDTH)
            z = _dot(h_scr[...], w_ref[:, s * W_C + p * MXU_WIDTH:s * W_C + (p + 1) * MXU_WIDTH])
            scr[slot, :, cols] = post(z, cols)
            next(stages, None)
            next(stages, None)
        for _ in stages:
            pass

    @pl.when(jnp.logical_and(i > 0, i % tiles_per_batch == 0))
    def _():
        for h in range(H_C):
            s_ref[h] = st_scr[h].T


def _inproj_hgrn(x, norm_g, scale, shift, w, lb, gn_g, batch, seq):
    tm, chunk = ROW_TILE, HGRN_CHUNK
    m = x.shape[0]
    nt = m // tm
    tiles_per_batch = seq // tm
    tri, masks, levels = _hgrn_tables(chunk)
    cur = lambda i: jnp.minimum(i, nt - 1)
    done = lambda i: jnp.maximum(i - 1, 0)
    mod_spec = pl.BlockSpec((None, 1, D_MODEL), lambda i: (cur(i) // tiles_per_batch, 0, 0))
    const = lambda shape: pl.BlockSpec(shape, lambda i: (0,) * len(shape))
    o, s = pl.pallas_call(
        functools.partial(_inproj_hgrn_kernel, chunk=chunk, levels=levels, tiles_per_batch=tiles_per_batch),
        grid=(nt + 1,),
        in_specs=[
            pl.BlockSpec((tm, D_MODEL), lambda i: (cur(i), 0)),
            const((1, D_MODEL)),
            mod_spec, mod_spec,
            const((D_MODEL, C_IN)),
            const((1, W_C)),
            const(tri.shape), const(masks.shape), const((1, DV_C)),
        ],
        out_specs=[pl.BlockSpec((tm, W_C), lambda i: (done(i), 0)),
                   pl.BlockSpec((None, H_C, DK_C, DV_C), lambda i: (done(i) // tiles_per_batch, 0, 0, 0))],
        out_shape=[jax.ShapeDtypeStruct((m, W_C), BF16),
                   jax.ShapeDtypeStruct((batch, H_C, DK_C, DV_C), F32)],
        scratch_shapes=[pltpu.VMEM((tm, D_MODEL), BF16),
                        pltpu.VMEM((2, tm, W_C), F32),
                        pltpu.VMEM((2, tm, W_C), F32),
                        pltpu.VMEM((2, tm, W_C), F32),
                        pltpu.VMEM((2, tm, W_C), BF16),
                        pltpu.VMEM((H_C, DV_C, DK_C), F32)],
        compiler_params=_cparams("arbitrary"),
        name="inproj_hgrn_recurrence",
    )(x, norm_g.reshape(1, D_MODEL), scale, shift, w, lb.reshape(1, W_C), tri, masks, gn_g.reshape(1, DV_C))
    return o, s


_DEC_ROWS = 2 * H_B


def _decode_head_rows(pieces):
    row = lax.broadcasted_iota(jnp.int32, (_DEC_ROWS, LANES), 0)
    out = jnp.zeros((_DEC_ROWS, LANES), F32)
    for h in range(H_B):
        out = jnp.where(row // 2 == h, jnp.broadcast_to(pieces[h], (_DEC_ROWS, LANES)), out)
    return out


def _decode_scores(q, kn, k_pages):
    row = lax.broadcasted_iota(jnp.int32, (_DEC_ROWS, LANES), 0)
    lane = lax.broadcasted_iota(jnp.int32, (_DEC_ROWS, LANES), 1)
    qm32 = jnp.where(lane // HD_B == row % 2,
                     _decode_head_rows([q[:, h * LANES:(h + 1) * LANES] for h in range(H_B)]), 0.0)
    qm = qm32.astype(BF16)
    s = jnp.concatenate([_dot_nt(qm, kp[...].astype(BF16)) for kp in k_pages], axis=1)
    key_head = lax.broadcasted_iota(jnp.int32, s.shape, 1) % H_B
    s = jnp.where(key_head == lax.broadcasted_iota(jnp.int32, s.shape, 0) // 2, s, NEG_INF)
    s_new = jnp.sum(qm32 * _decode_head_rows([kn[h:h + 1, :] for h in range(H_B)]), axis=-1, keepdims=True)
    return s, s_new


def _decode_finish(s, s_new, lam, vn, v_pages, subln_row, gb, lam_init):
    page_rows = H_B * PAGE_SIZE
    m = jnp.maximum(jnp.max(s, axis=-1, keepdims=True), s_new)
    p = jnp.exp2(s - m)
    p_new = jnp.exp2(s_new - m)
    l = jnp.sum(p, axis=-1, keepdims=True) + p_new
    comp = lax.broadcasted_iota(jnp.int32, (_DEC_ROWS, 1), 0) % 2
    wgt = jnp.where(comp == 0, 1.0, -lam) / l
    pw = (p * wgt).astype(BF16)
    o8 = (p_new * wgt) * _decode_head_rows([vn[h:h + 1, :] for h in range(H_B)])
    for j, vp in enumerate(v_pages):
        o8 = o8 + _dot(pw[:, j * page_rows:(j + 1) * page_rows], vp[...].astype(BF16))
    out = []
    for h in range(H_B):
        o = o8[2 * h:2 * h + 1, :] + o8[2 * h + 1:2 * h + 2, :]
        out.append(_rms(o, subln_row) * (1.0 - lam_init) * gb[:, h * DV_B:(h + 1) * DV_B])
    return jnp.concatenate(out, axis=1)


def _decode_conv_kernel(st_ref, u_ref, ga_ref, w_ref, cb_ref, lg_ref, lb_ref, y_ref, ns_ref):
    st = st_ref[...]
    u = u_ref[...]
    w = w_ref[...]
    y = jnp.sum(st * w[None, :CONV_W - 1, :], axis=1) + u * w[CONV_W - 1:CONV_W, :] + cb_ref[...]
    y_ref[...] = _conv_post(y, lg_ref[...], lb_ref[...], ga_ref[...].astype(F32)).astype(BF16)
    ns_ref[:, 0:CONV_W - 2, :] = st[:, 1:CONV_W - 1, :]
    ns_ref[:, CONV_W - 2:CONV_W - 1, :] = u[:, None, :]


def _decode_conv(state_all, layer, u, ga, conv_w, conv_b, ln_g, ln_b):
    n = u.shape[0]
    tile = 32
    vec = pl.BlockSpec((1, W_A), lambda i: (0, 0))
    sspec = pl.BlockSpec((tile, CONV_W - 1, W_A), lambda i: (i, 0, 0))
    rspec = pl.BlockSpec((tile, W_A), lambda i: (i, 0))
    return pl.pallas_call(
        _decode_conv_kernel,
        grid=(n // tile,),
        in_specs=[pl.BlockSpec((None, tile, CONV_W - 1, W_A), lambda i: (layer, i, 0, 0)),
                  rspec, rspec, pl.BlockSpec((CONV_W, W_A), lambda i: (0, 0)), vec, vec, vec],
        out_specs=[rspec, sspec],
        out_shape=[jax.ShapeDtypeStruct((n, W_A), BF16),
                   jax.ShapeDtypeStruct((n, CONV_W - 1, W_A), F32)],
        compiler_params=_cparams("arbitrary"),
        name="decode_conv",
    )(state_all, u, ga, conv_w, conv_b.reshape(1, W_A), ln_g.reshape(1, W_A), ln_b.reshape(1, W_A))


def _decode_spread_table():
    rows = np.arange(3 * 3 * H_C)
    cols = np.arange(3 * W_C)
    same_vec = (rows[:, None] // (3 * H_C)) == (cols[None, :] // W_C)
    same_head = (rows[:, None] % H_C) == ((cols[None, :] % W_C) // DV_C)
    return jnp.asarray(same_vec & same_head, BF16)


def _decode_hgrn_kernel(q_ref, lg_ref, i_ref, sg_ref, gn_ref, e_ref, s_ref, *rest, n_seq):
    o_ref, ns_ref = rest[-2:]

    def split_rows(x):
        return [p.astype(F32) for p in _split3(x)]

    def body(n, carry):
        f8 = jnp.exp2(lg_ref[n])
        parts = jnp.concatenate(split_rows(f8) + split_rows(1.0 - f8) + split_rows(q_ref[n]), axis=0)
        spread = _dot(parts.T.astype(BF16), e_ref[...])
        v8 = i_ref[n]
        rows = []
        for h in range(H_C):
            f_b, k_b, q_b =(spread[:, j * W_C + h * DV_C:j * W_C + (h + 1) * DV_C] for j in range(3))
            s_new = f_b * s_ref[n, h] + k_b * v8[h:h + 1, :]
            ns_ref[n, h] = s_new
            rows.append(jnp.sum(q_b * s_new, axis=0, keepdims=True))
        o = jnp.concatenate(rows, axis=0)
        o_ref[n] = (_rms(o, gn_ref[...]) * sg_ref[n].astype(F32)).astype(BF16)
        return carry

    lax.fori_loop(0, n_seq, body, 0)


def _decode_hgrn(qs, lg, iv, sg, gn_g, state_all, layer, new_prev):
    n = qs.shape[0]
    tile = SAMPLE_STATE_TILE
    hspec = pl.BlockSpec((tile, H_C, DV_C), lambda i: (i, 0, 0))
    sspec = pl.BlockSpec((None, tile, H_C, DK_C, DV_C), lambda i: (layer, i, 0, 0, 0))
    h3 = lambda a: a.reshape(n, H_C, DV_C)
    table = _decode_spread_table()
    in_specs = [hspec, hspec, hspec, hspec, pl.BlockSpec((1, DV_C), lambda i: (0, 0)),
                pl.BlockSpec(table.shape, lambda i: (0, 0)), sspec]
    args = [h3(qs), h3(lg), h3(iv), h3(sg), gn_g.reshape(1, DV_C), table, state_all]
    aliases = {}
    if new_prev is not None:
        aliases = {len(args): 1}
        in_specs.append(pl.BlockSpec(memory_space=pl.ANY))
        args.append(new_prev)
    o, ns = pl.pallas_call(
        functools.partial(_decode_hgrn_kernel, n_seq=tile),
        grid=(n // tile,),
        in_specs=in_specs,
        out_specs=[hspec, sspec],
        out_shape=[jax.ShapeDtypeStruct((n, H_C, DV_C), BF16),
                   jax.ShapeDtypeStruct((N_C, n, H_C, DK_C, DV_C), F32)],
        input_output_aliases=aliases,
        compiler_params=_cparams("arbitrary"),
        name="decode_hgrn",
    )(*args)
    return o.reshape(n, W_C), ns


def _rope_tables(pos, rows):
    half = ROT_DIM // 2
    inv_freq = ROPE_THETA ** (-jnp.arange(half, dtype=F32) / half)
    ang = pos.astype(F32)[:, None] * inv_freq[None, :]
    cos, sin = jnp.cos(ang), jnp.sin(ang)
    n = pos.shape[0]
    ones = jnp.ones((n, HD_B - ROT_DIM), F32)
    zeros = jnp.zeros((n, HD_B - ROT_DIM), F32)
    z8 = jnp.zeros((n, half), F32)
    cos_t = jnp.concatenate([cos, cos, ones], axis=1)
    sa_t = jnp.concatenate([-sin, z8, zeros], axis=1)
    sb_t = jnp.concatenate([z8, sin, zeros], axis=1)
    out = []
    for t in (cos_t, sa_t, sb_t):
        t = jnp.concatenate([t, t], axis=1)
        out.append(jnp.broadcast_to(t, (rows, LANES)) if n == 1 else t)
    return out


def kernel(x_prompt, x_sample, c_prompt, c_sample, cache_k, cache_v, page_table, state_conv, state_hgrn,
           norm_g, ada_w, ada_b, w_in_ac, w_out_ac, conv_w, conv_b, ln_g, ln_b, lam_q1, lam_k1, lam_q2,
           lam_k2, subln_g, w_in_c, w_out_c, gn_g, lb_logits, final_g):
    bp, tp = x_prompt.shape[:2]
    bs, ts = x_sample.shape[:2]
    assert ts == 1
    n_pool = cache_k.shape[1]
    n_past = page_table.shape[1] * PAGE_SIZE
    mp = bp * tp

    lb_sm = jax.nn.softmax(lb_logits.astype(F32), axis=0)
    lb_all = jnp.cumsum(lb_sm, axis=0) - lb_sm[0]

    mod = _ada_modulation(jnp.concatenate([c_prompt, c_sample], axis=0), ada_w, ada_b)

    def mods(l):
        out = []
        for j in range(3):
            m = mod[l, :, j * D_MODEL:(j + 1) * D_MODEL]
            out.append((m[:bp].reshape(bp, 1, D_MODEL), m[bp:].reshape(1, bs, D_MODEL)))
        return out

    tabs_p = _rope_tables(jnp.arange(tp), tp)
    tabs_s = _rope_tables(jnp.full((1,), n_past), bs)
    ck4 = cache_k.reshape(N_AC, n_pool, PAGE_SIZE * H_B, 2 * HD_B)
    cv4 = cache_v.reshape(N_AC, n_pool, PAGE_SIZE * H_B, DV_B)

    tiles_p = tp // ROW_TILE
    hp = x_prompt.reshape(mp, D_MODEL)
    hs = x_sample.reshape(bs, D_MODEL)
    cp_l, sp_l, cs_l = [], [], []
    kv_p = kv_s = ss = None
    for l in range(DEPTH):
        (sh_p, sh_s), (sc_p, sc_s), (gt_p, gt_s) = mods(l)
        last = final_g if l == DEPTH - 1 else None
        if l % 2 == 0:
            a = l // 2
            lam_init = 0.8 - 0.6 * math.exp(-0.3 * l)
            lam = (jnp.exp(jnp.sum(lam_q1[a].astype(F32) * lam_k1[a].astype(F32)))
                   - jnp.exp(jnp.sum(lam_q2[a].astype(F32) * lam_k2[a].astype(F32))) + lam_init)
            lam = jnp.full((1, LANES), lam, F32)
            w_in = w_in_ac[a].astype(BF16)
            w_out = w_out_ac[a].astype(BF16)
            q, k, kb, v, vb, gb, u, ga = _inproj_ac(hp, norm_g[l], sc_p, sh_p, w_in, *tabs_p,
                                                    ROW_TILE, tiles_p, a, kv_p)
            kv_p = (k, v)
            qs, k, _, v, _, gbs, us, gas = _inproj_ac(hs, norm_g[l], sc_s, sh_s, w_in, *tabs_s, bs, 1, a, kv_s)
            kv_s = (k, v)
            ob, obs = _attention(q, kb, vb, gb, bp, tp, qs, k, v, gbs, ck4, cv4, a, page_table,
                                 lam, subln_g[a], lam_init)
            y, cst = _prompt_conv(u, ga, conv_w[a], conv_b[a], ln_g[a], ln_b[a], bp, tp)
            hp = _outproj([ob, y], w_out, hp, gt_p, ROW_TILE, tiles_p, last)
            cp_l.append(cst)
            y, cst = _decode_conv(state_conv, a, us, gas, conv_w[a], conv_b[a], ln_g[a], ln_b[a])
            hs = _outproj([obs, y], w_out, hs, gt_s, bs, 1, last)
            cs_l.append(cst)
        else:
            ci = l // 2
            w_in = w_in_c[ci].astype(BF16)
            w_out = w_out_c[ci].astype(BF16)
            o, st = _inproj_hgrn(hp, norm_g[l], sc_p, sh_p, w_in, lb_all[l], gn_g[ci], bp, tp)
            hp = _outproj([o], w_out, hp, gt_p, ROW_TILE, tiles_p, last)
            sp_l.append(st)
            qs, lg, iv, sg = _inproj_c(hs, norm_g[l], sc_s, sh_s, w_in, lb_all[l], bs, 1)
            o, ss = _decode_hgrn(qs, lg, iv, sg, gn_g[ci], state_hgrn, ci, ss)
            hs = _outproj([o], w_out, hs, gt_s, bs, 1, last)

    y_prompt = hp.reshape(bp, tp, D_MODEL)
    y_sample = hs.reshape(bs, ts, D_MODEL)
    return (y_prompt, y_sample,
            kv_p[0].reshape(N_AC, bp, tp, H_B, 2 * HD_B), kv_p[1].reshape(N_AC, bp, tp, H_B, DV_B),
            jnp.stack(cp_l), jnp.stack(sp_l),
            kv_s[0].reshape(N_AC, bs, ts, H_B, 2 * HD_B), kv_s[1].reshape(N_AC, bs, ts, H_B, DV_B),
            jnp.stack(cs_l), ss)
```

```python
import functools
import math

import numpy as np
import jax
import jax.numpy as jnp
from jax import lax
from jax.experimental import pallas as pl
from jax.experimental.pallas import tpu as pltpu

F32 = jnp.float32
BF16 = jnp.bfloat16

D_MODEL = 1024
DEPTH = 4
PAGE_SIZE = 128
N_AC = (DEPTH + 1) // 2
N_C = DEPTH // 2

H_B = 4
HD_B = 64
DV_B = 2 * HD_B
W_B = H_B * DV_B
QK_W = H_B * 2 * HD_B
ROT_DIM = HD_B // 4
ROPE_THETA = 500000.0
NEG_INF = -1e30

W_A = D_MODEL // 2
CONV_W = 31

H_C = 8
DK_C = D_MODEL // H_C
DV_C = D_MODEL // H_C
W_C = H_C * DV_C

EPS = 1e-6
LOG2_E = math.log2(math.e)
Q_SCALE = HD_B ** -0.5 * LOG2_E

AC_IN = 2 * QK_W + 2 * W_B + 3 * W_A
C_IN = 2 * H_C * DK_C + 2 * W_C

LANES = 128
SUBLANES = 8
MXU_WIDTH = 256
VMEM_LIMIT_BYTES = 56 * 1024 * 1024

ROW_TILE = 512
ATTN_TILE = 256
CONV_TILE = 512
CONV_HALO = 32
HGRN_CHUNK = 128
SAMPLE_STATE_TILE = 8


def _cparams(*sem):
    return pltpu.CompilerParams(dimension_semantics=sem, vmem_limit_bytes=VMEM_LIMIT_BYTES)


def _silu(x):
    return x * jax.nn.sigmoid(x)


def _dot(a, b):
    return jnp.dot(a, b, preferred_element_type=F32)


def _dot_nt(a, b):
    return lax.dot_general(a, b, (((1,), (1,)), ((), ())), preferred_element_type=F32)


def _dot_tn(a, b):
    return lax.dot_general(a, b, (((0,), (0,)), ((), ())), preferred_element_type=F32)


def _rms(x, g):
    return x * lax.rsqrt(jnp.mean(x * x, axis=-1, keepdims=True) + EPS) * g


def _ada_kernel(c_ref, w_ref, b_ref, o_ref):
    s = _silu(c_ref[...]).astype(BF16)
    o_ref[...] = _dot(s, w_ref[...].astype(BF16)) + b_ref[...]


def _ada_modulation(c_all, ada_w, ada_b):
    rows = c_all.shape[0]
    return pl.pallas_call(
        _ada_kernel,
        grid=(DEPTH, 3),
        in_specs=[
            pl.BlockSpec((rows, D_MODEL), lambda l, j: (0, 0)),
            pl.BlockSpec((None, D_MODEL, D_MODEL), lambda l, j: (l, 0, j)),
            pl.BlockSpec((None, 1, D_MODEL), lambda l, j: (l, 0, j)),
        ],
        out_specs=pl.BlockSpec((None, rows, D_MODEL), lambda l, j: (l, 0, j)),
        out_shape=jax.ShapeDtypeStruct((DEPTH, rows, 3 * D_MODEL), F32),
        compiler_params=_cparams("arbitrary", "arbitrary"),
        name="ada_modulation",
    )(c_all, ada_w, ada_b.reshape(DEPTH, 1, 3 * D_MODEL))


def _weight_spec(w):
    stacked, layer = w
    return pl.BlockSpec((None,) + stacked.shape[1:], lambda i: (layer, 0, 0))


def _modulated(x_ref, ng_ref, sc_ref, sh_ref):
    h = _rms(x_ref[...], ng_ref[...])
    return (h * (1.0 + sc_ref[...]) + sh_ref[...]).astype(BF16)


def _inproj_ac_kernel(x_ref, ng_ref, sc_ref, sh_ref, w_ref, cos_ref, sa_ref, sb_ref, *rest):
    q_ref, k_ref, kb_ref, v_ref, vb_ref, gb_ref, u_ref, ga_ref, h_scr = rest[-9:]
    h_scr[...] = _modulated(x_ref, ng_ref, sc_ref, sh_ref)

    def proj(s):
        return _dot(h_scr[...], w_ref[:, s * QK_W:(s + 1) * QK_W])

    cos, sa, sb = cos_ref[...], sa_ref[...], sb_ref[...]

    def rope(z):
        return (z * cos + pltpu.roll(z, LANES - ROT_DIM // 2, 1) * sa
                + pltpu.roll(z, ROT_DIM // 2, 1) * sb)

    zq = proj(0)
    for g in range(H_B):
        sl = slice(g * LANES, (g + 1) * LANES)
        q_ref[:, sl] = (rope(zq[:, sl]) * Q_SCALE).astype(BF16)
    tm = x_ref.shape[0]
    zk = proj(1)
    for g in range(H_B):
        sl = slice(g * LANES, (g + 1) * LANES)
        r = rope(zk[:, sl])
        k_ref[pl.ds(g, tm, stride=H_B), :] = r
        kb_ref[:, sl] = r.astype(BF16)
    zv = proj(2)
    for g in range(H_B):
        v_ref[pl.ds(g, tm, stride=H_B), :] = zv[:, g * LANES:(g + 1) * LANES]
    vb_ref[...] = zv.astype(BF16)
    gb_ref[...] = _silu(proj(3)).astype(BF16)
    a_val = proj(4)
    u_ref[...] = a_val * jax.nn.sigmoid(proj(5))
    ga_ref[...] = _silu(proj(6)).astype(BF16)


def _mod_specs(rows_per_mod, tm, tiles_per_mod):
    r = rows_per_mod
    return pl.BlockSpec((None, r, D_MODEL), lambda i: (i // tiles_per_mod, 0, 0))


def _inproj_ac(x, norm_g, scale, shift, w, cos_t, sa_t, sb_t, tm, tiles_per_mod, layer, kv_prev):
    m = x.shape[0]
    tab_tiles = cos_t.shape[0] // tm
    mod_spec = _mod_specs(scale.shape[1], tm, tiles_per_mod)
    row512 = pl.BlockSpec((tm, QK_W), lambda i: (i, 0))
    heads = pl.BlockSpec((None, H_B * tm, LANES), lambda i: (layer, i, 0))
    tab_spec = pl.BlockSpec((tm, LANES), lambda i: (i % tab_tiles, 0))
    out_shape = [
        jax.ShapeDtypeStruct((m, QK_W), BF16),
        jax.ShapeDtypeStruct((N_AC, H_B * m, LANES), F32),
        jax.ShapeDtypeStruct((m, QK_W), BF16),
        jax.ShapeDtypeStruct((N_AC, H_B * m, LANES), F32),
        jax.ShapeDtypeStruct((m, W_B), BF16),
        jax.ShapeDtypeStruct((m, W_B), BF16),
        jax.ShapeDtypeStruct((m, W_A), F32),
        jax.ShapeDtypeStruct((m, W_A), BF16),
    ]
    in_specs = [
        pl.BlockSpec((tm, D_MODEL), lambda i: (i, 0)),
        pl.BlockSpec((1, D_MODEL), lambda i: (0, 0)),
        mod_spec, mod_spec,
        _weight_spec(w),
        tab_spec, tab_spec, tab_spec,
    ]
    args = [x, norm_g.reshape(1, D_MODEL), scale, shift, w[0], cos_t, sa_t, sb_t]
    aliases = {}
    if kv_prev is not None:
        aliases = {len(args): 1, len(args) + 1: 3}
        in_specs += [pl.BlockSpec(memory_space=pl.ANY)] * 2
        args += list(kv_prev)
    return pl.pallas_call(
        _inproj_ac_kernel,
        grid=(m // tm,),
        in_specs=in_specs,
        out_specs=[row512, heads, row512, heads, row512, row512, row512, row512],
        out_shape=out_shape,
        input_output_aliases=aliases,
        scratch_shapes=[pltpu.VMEM((tm, D_MODEL), BF16)],
        compiler_params=_cparams("arbitrary"),
        name="inproj_conv_attn",
    )(*args)


def _inproj_c_kernel(x_ref, ng_ref, sc_ref, sh_ref, w_ref, lb_ref, q_ref, lg_ref, i_ref, sg_ref, h_scr):
    h_scr[...] = _modulated(x_ref, ng_ref, sc_ref, sh_ref)

    def proj(s):
        return _dot(h_scr[...], w_ref[:, s * W_C:(s + 1) * W_C])

    q_ref[...] = _silu(proj(0))
    lb = lb_ref[...]
    lg_ref[...] = jnp.log(lb + (1.0 - lb) * jax.nn.sigmoid(proj(1))) * LOG2_E
    i_ref[...] = proj(2)
    sg_ref[...] = _silu(proj(3)).astype(BF16)


def _inproj_c(x, norm_g, scale, shift, w, lb, tm, tiles_per_mod):
    m = x.shape[0]
    mod_spec = _mod_specs(scale.shape[1], tm, tiles_per_mod)
    row = pl.BlockSpec((tm, W_C), lambda i: (i, 0))
    out_shape = [
        jax.ShapeDtypeStruct((m, W_C), F32),
        jax.ShapeDtypeStruct((m, W_C), F32),
        jax.ShapeDtypeStruct((m, W_C), F32),
        jax.ShapeDtypeStruct((m, W_C), BF16),
    ]
    return pl.pallas_call(
        _inproj_c_kernel,
        grid=(m // tm,),
        in_specs=[
            pl.BlockSpec((tm, D_MODEL), lambda i: (i, 0)),
            pl.BlockSpec((1, D_MODEL), lambda i: (0, 0)),
            mod_spec, mod_spec,
            _weight_spec(w),
            pl.BlockSpec((1, W_C), lambda i: (0, 0)),
        ],
        out_specs=[row] * 4,
        out_shape=out_shape,
        scratch_shapes=[pltpu.VMEM((tm, D_MODEL), BF16)],
        compiler_params=_cparams("arbitrary"),
        name="inproj_hgrn",
    )(x, norm_g.reshape(1, D_MODEL), scale, shift, w[0], lb.reshape(1, W_C))


def _outproj_kernel(*refs, n_act, final_norm):
    acts = refs[:n_act]
    w_ref, x_ref, gate_ref = refs[n_act:n_act + 3]
    o_ref = refs[-1]
    acc = None
    lo = 0
    for a_ref in acts:
        width = a_ref.shape[-1]
        part = _dot(a_ref[...], w_ref[lo:lo + width, :])
        acc = part if acc is None else acc + part
        lo += width
    y = x_ref[...] + gate_ref[...] * acc
    if final_norm:
        y = _rms(y, refs[n_act + 3][...])
    o_ref[...] = y


def _outproj(acts, w, x, gate, tm, tiles_per_mod, final_g=None):
    m = x.shape[0]
    mod_spec = _mod_specs(gate.shape[1], tm, tiles_per_mod)
    in_specs = [pl.BlockSpec((tm, a.shape[1]), lambda i: (i, 0)) for a in acts]
    in_specs += [
        _weight_spec(w),
        pl.BlockSpec((tm, D_MODEL), lambda i: (i, 0)),
        mod_spec,
    ]
    args = list(acts) + [w[0], x, gate]
    if final_g is not None:
        in_specs.append(pl.BlockSpec((1, D_MODEL), lambda i: (0, 0)))
        args.append(final_g.reshape(1, D_MODEL))
    return pl.pallas_call(
        functools.partial(_outproj_kernel, n_act=len(acts), final_norm=final_g is not None),
        grid=(m // tm,),
        in_specs=in_specs,
        out_specs=pl.BlockSpec((tm, D_MODEL), lambda i: (i, 0)),
        out_shape=jax.ShapeDtypeStruct((m, D_MODEL), F32),
        compiler_params=_cparams("arbitrary"),
        name="outproj",
    )(*args)


def _attn_kernel(pt_ref, lam_ref, sgc_ref, sgr_ref, q_ref, k_ref, v_ref, gb_ref,
                 qs_ref, kn_ref, vn_ref, gbs_ref, *rest, tile, lam_init, n_dec, n_pages):
    pages = rest[:2 * n_dec * n_pages]
    o_ref, os_ref, q2_scr, s_a, s_b, m_scr, l_scr, acc_scr = rest[2 * n_dec * n_pages:]
    sg_ref = sgc_ref
    qi = pl.program_id(1)
    heads = [slice(h * LANES, (h + 1) * LANES) for h in range(H_B)]
    lane = lax.broadcasted_iota(jnp.int32, (tile, LANES), 1)
    for h, sl in enumerate(heads):
        q = q_ref[:, sl]
        zero = jnp.zeros_like(q)
        q2_scr[h] = jnp.concatenate([jnp.where(lane < HD_B, q, zero), jnp.where(lane >= HD_B, q, zero)], axis=0)
    m_scr[...] = jnp.full(m_scr.shape, NEG_INF, F32)
    l_scr[...] = jnp.zeros(l_scr.shape, F32)
    acc_scr[...] = jnp.zeros(acc_scr.shape, F32)

    def scores(t, s_scr):
        start = pl.multiple_of(t * tile, tile)
        for h, sl in enumerate(heads):
            s_scr[h] = _dot_nt(k_ref[pl.ds(start, tile), sl], q2_scr[h])

    def softmax_part(s_scr, mask):
        parts = []
        for h in range(H_B):
            sh = s_scr[h]
            if mask is not None:
                sh = jnp.where(mask, sh, NEG_INF)
            m_old = m_scr[h]
            m_new = jnp.maximum(m_old, jnp.max(sh, axis=0, keepdims=True))
            alpha = jnp.exp2(m_old - m_new)
            p = jnp.exp2(sh - m_new)
            m_scr[h] = m_new
            l_scr[h] = alpha * l_scr[h] + jnp.sum(p, axis=0, keepdims=True)
            parts.append((alpha, p.astype(BF16)))
        return parts

    def value_part(t, parts):
        start = pl.multiple_of(t * tile, tile)
        for h, sl in enumerate(heads):
            alpha, p = parts[h]
            acc_scr[h] = alpha * acc_scr[h] + _dot_tn(v_ref[pl.ds(start, tile), sl], p)

    def consume(s_scr, t, mask):
        value_part(t, softmax_part(s_scr, mask))

    odd = qi % 2

    @pl.when(odd == 1)
    def _():
        scores(0, s_b)
        scores(1, s_a)
        consume(s_b, 0, None)

    @pl.when(odd == 0)
    def _():
        scores(0, s_a)

    def body(jj, carry):
        t = odd + 2 * jj
        scores(t + 1, s_b)
        consume(s_a, t, None)
        scores(t + 2, s_a)
        consume(s_b, t + 1, None)
        return carry

    lax.fori_loop(0, qi // 2, body, 0)
    key = lax.broadcasted_iota(jnp.int32, (tile, 2 * tile), 0)
    qry = lax.broadcasted_iota(jnp.int32, (tile, 2 * tile), 1)
    diag = softmax_part(s_a, key <= jnp.where(qry >= tile, qry - tile, qry))
    lam = lam_ref[:, 0:1]
    dec = []
    for r in range(n_dec):
        k_pages = pages[2 * r * n_pages:(2 * r + 1) * n_pages]
        v_pages = pages[(2 * r + 1) * n_pages:(2 * r + 2) * n_pages]
        dec.append(_decode_scores(qs_ref[r].astype(F32), kn_ref[r], k_pages) + (v_pages,))
    value_part(qi, diag)
    for r, (s, s_new, v_pages) in enumerate(dec):
        os_ref[r] = _decode_finish(s, s_new, lam, vn_ref[r], v_pages, sgr_ref[...],
                                   gbs_ref[r].astype(F32), lam_init).astype(BF16)
    for h, sl in enumerate(heads):
        l, acc = l_scr[h], acc_scr[h]
        o = acc[:, :tile] / l[:, :tile] - lam_ref[:, 0:1] * (acc[:, tile:] / l[:, tile:])
        y = o * lax.rsqrt(jnp.mean(o * o, axis=0, keepdims=True) + EPS) * sg_ref[...]
        o_ref[:, sl] = (y.T * (1.0 - lam_init) * gb_ref[:, sl].astype(F32)).astype(BF16)


def _attention(q, kb, vb, gb, batch, seq, qs, kn_all, vn_all, gbs, cache_k4, cache_v4, layer, page_table,
               lam, subln_g, lam_init):
    tile = ATTN_TILE
    nq = seq // tile
    n_seq, n_pages = page_table.shape
    n_dec = n_seq // (batch * nq)
    assert n_dec * batch * nq == n_seq
    q3, k3, v3, g3 = (a.reshape(batch, seq, QK_W) for a in (q, kb, vb, gb))
    qspec = pl.BlockSpec((None, tile, QK_W), lambda b, i, pt: (b, i, 0))
    kvspec = pl.BlockSpec((None, seq, QK_W), lambda b, i, pt: (b, 0, 0))
    rows = pl.BlockSpec((n_dec, 1, QK_W), lambda b, i, pt: (b * nq + i, 0, 0))
    new_kv = pl.BlockSpec((None, n_dec, H_B, LANES), lambda b, i, pt: (layer, b * nq + i, 0, 0))

    def page_spec(r, j):
        return pl.BlockSpec((None, None, H_B * PAGE_SIZE, LANES),
                            lambda b, i, pt: (layer, pt[(b * nq + i) * n_dec + r, j], 0, 0))

    page_specs, page_args = [], []
    for r in range(n_dec):
        for cache in (cache_k4, cache_v4):
            page_specs += [page_spec(r, j) for j in range(n_pages)]
            page_args += [cache] * n_pages
    grid_spec = pltpu.PrefetchScalarGridSpec(
        num_scalar_prefetch=1,
        grid=(batch, nq),
        in_specs=[pl.BlockSpec((1, LANES), lambda b, i, pt: (0, 0)),
                  pl.BlockSpec((DV_B, 1), lambda b, i, pt: (0, 0)),
                  pl.BlockSpec((1, DV_B), lambda b, i, pt: (0, 0)),
                  qspec, kvspec, kvspec, qspec, rows, new_kv, new_kv, rows] + page_specs,
        out_specs=[qspec, rows],
        scratch_shapes=[pltpu.VMEM((H_B, 2 * tile, LANES), BF16),
                        pltpu.VMEM((H_B, tile, 2 * tile), F32),
                        pltpu.VMEM((H_B, tile, 2 * tile), F32),
                        pltpu.VMEM((H_B, 1, 2 * tile), F32),
                        pltpu.VMEM((H_B, 1, 2 * tile), F32),
                        pltpu.VMEM((H_B, DV_B, 2 * tile), F32)],
    )
    r3 = lambda a: a.reshape(n_seq, 1, QK_W)
    h4 = lambda a: a.reshape(N_AC, n_seq, H_B, LANES)
    out, out_s = pl.pallas_call(
        functools.partial(_attn_kernel, tile=tile, lam_init=lam_init, n_dec=n_dec, n_pages=n_pages),
        grid_spec=grid_spec,
        out_shape=[jax.ShapeDtypeStruct((batch, seq, W_B), BF16),
                   jax.ShapeDtypeStruct((n_seq, 1, W_B), BF16)],
        compiler_params=_cparams("arbitrary", "arbitrary"),
        name="diff_attention",
    )(page_table, lam, subln_g.reshape(DV_B, 1), subln_g.reshape(1, DV_B), q3, k3, v3, g3,
      r3(qs), h4(kn_all), h4(vn_all), r3(gbs), *page_args)
    return out.reshape(batch * seq, W_B), out_s.reshape(n_seq, W_B)


def _conv_post(y, ln_g, ln_b, ga):
    yc = y - jnp.mean(y, axis=-1, keepdims=True)
    yn = yc * lax.rsqrt(jnp.mean(yc * yc, axis=-1, keepdims=True) + EPS) * ln_g + ln_b
    return _silu(yn) * ga


def _conv_kernel(u_ref, ga_ref, w_ref, cb_ref, lg_ref, lb_ref, y_ref, st_ref, ext_scr, sh_scr, *, tile, rows):
    t = pl.program_id(1)
    first = CONV_HALO - (CONV_W - 1)
    sh_rows = sh_scr.shape[1]

    @pl.when(t == 0)
    def _():
        ext_scr[0:CONV_HALO, :] = jnp.zeros((CONV_HALO, W_A), F32)

    @pl.when(t > 0)
    def _():
        ext_scr[0:CONV_HALO, :] = ext_scr[tile:tile + CONV_HALO, :]

    ext_scr[CONV_HALO:CONV_HALO + tile, :] = u_ref[...]
    for r in range(1, SUBLANES):
        sh_scr[r - 1] = ext_scr[r:r + sh_rows, :]
    for r0 in range(0, tile, rows):
        acc = jnp.zeros((rows, W_A), F32) + cb_ref[...]
        for j in range(CONV_W):
            off = first + j
            r, base = off % SUBLANES, off - off % SUBLANES + r0
            src = ext_scr[base:base + rows, :] if r == 0 else sh_scr[r - 1, base:base + rows, :]
            acc = acc + src * w_ref[j:j + 1, :]
        y_ref[r0:r0 + rows, :] = _conv_post(acc, lg_ref[...], lb_ref[...],
                                            ga_ref[r0:r0 + rows, :].astype(F32)).astype(BF16)

    @pl.when(t == pl.num_programs(1) - 1)
    def _():
        st_ref[...] = ext_scr[CONV_HALO + tile - (CONV_W - 1):CONV_HALO + tile, :]


def _prompt_conv(u, ga, conv_w, conv_b, ln_g, ln_b, batch, seq):
    tile = CONV_TILE
    u3, g3 = u.reshape(batch, seq, W_A), ga.reshape(batch, seq, W_A)
    tspec = pl.BlockSpec((None, tile, W_A), lambda b, t: (b, t, 0))
    vec = pl.BlockSpec((1, W_A), lambda b, t: (0, 0))
    y, st = pl.pallas_call(
        functools.partial(_conv_kernel, tile=tile, rows=64),
        grid=(batch, seq // tile),
        in_specs=[tspec, tspec, pl.BlockSpec((CONV_W, W_A), lambda b, t: (0, 0)), vec, vec, vec],
        out_specs=[tspec, pl.BlockSpec((None, CONV_W - 1, W_A), lambda b, t: (b, 0, 0))],
        out_shape=[jax.ShapeDtypeStruct((batch, seq, W_A), BF16),
                   jax.ShapeDtypeStruct((batch, CONV_W - 1, W_A), F32)],
        scratch_shapes=[pltpu.VMEM((CONV_HALO + tile, W_A), F32),
                        pltpu.VMEM((SUBLANES - 1, CONV_HALO + tile - SUBLANES, W_A), F32)],
        compiler_params=_cparams("arbitrary", "arbitrary"),
        name="prompt_conv",
    )(u3, g3, conv_w, conv_b.reshape(1, W_A), ln_g.reshape(1, W_A), ln_b.reshape(1, W_A))
    return y.reshape(batch * seq, W_A), st


def _hgrn_tables(chunk):
    levels = int(math.log2(chunk))
    t = np.arange(chunk)[:, None]
    u = np.arange(chunk)[None, :]
    masks = []
    for v in range(levels):
        m = 1 << v
        masks.append((t // (2 * m) == u // (2 * m)) & (t % (2 * m) >= m) & (u % (2 * m) < m))
    return jnp.asarray(u <= t, BF16), jnp.asarray(np.stack(masks), BF16), levels


def _split3(x):
    hi = x.astype(BF16)
    r = x - hi.astype(F32)
    mid = r.astype(BF16)
    lo = (r - mid.astype(F32)).astype(BF16)
    return hi, mid, lo


def _hgrn_level_exponent(b, g2, m, row):
    chunk, width = b.shape
    if m == 1:
        return jnp.where((row & 1) != 0, g2, 0.0)
    if 2 * m < SUBLANES:
        b3 = b.reshape(chunk // SUBLANES, SUBLANES, width)
        sub = lax.broadcasted_iota(jnp.int32, b3.shape, 1)
        bm = b3[:, m - 1:m, :]
        for blk in range(1, SUBLANES // (2 * m)):
            bm = jnp.where(sub < blk * 2 * m, bm, b3[:, blk * 2 * m + m - 1:blk * 2 * m + m, :])
    else:
        b3 = b.reshape(chunk // (2 * m), 2 * m, width)
        bm = b3[:, m - 1:m, :]
    d = lax.bitcast_convert_type((b3 - bm).reshape(chunk, width), jnp.uint32)
    return lax.bitcast_convert_type(d | jnp.uint32(0x80000000), F32)


def _hgrn_chunk_stages(q_ref, lg_ref, i_ref, sg_ref, tri_ref, msk_ref, gn_ref, o_ref, st_scr, chunk, levels):
    heads = [slice(h * LANES, (h + 1) * LANES) for h in range(H_C)]
    g2 = lg_ref[...]
    q = q_ref[...]
    v = i_ref[...]
    k = 1.0 - jnp.exp2(g2)
    b3 = _dot(tri_ref[...], jnp.concatenate(_split3(g2), axis=1))
    yield
    b = b3[:, :W_C] + b3[:, W_C:2 * W_C] + b3[:, 2 * W_C:]
    row = lax.broadcasted_iota(jnp.int32, (chunk, W_C), 0)
    qb, kb = q.astype(BF16), k.astype(BF16)
    a = [jnp.zeros((chunk, chunk), BF16)] * H_C
    for lv in range(levels):
        m = 1 << lv
        x = jnp.where((row & m) != 0, qb, kb) * jnp.exp2(_hgrn_level_exponent(b, g2, m, row).astype(BF16))
        msk = msk_ref[lv]
        a = [a[h] + msk * _dot_nt(x[:, sl], x[:, sl]).astype(BF16) for h, sl in enumerate(heads)]
        yield
    vb = v.astype(BF16)
    qe = (q * jnp.exp2(b)).astype(BF16)
    b_last = b[chunk - 1:chunk, :]
    kd = (k * jnp.exp2(b_last - b)).astype(BF16)
    decay = jnp.exp2(b_last)
    qk = q * k
    for h, sl in enumerate(heads):
        st = st_scr[h]
        o = _dot(a[h], vb[:, sl]) + jnp.sum(qk[:, sl], axis=1, keepdims=True) * v[:, sl]
        o = o + _dot_nt(qe[:, sl], st.astype(BF16))
        st_scr[h] = st * decay[:, sl] + _dot_tn(vb[:, sl], kd[:, sl])
        o_ref[:, sl] = (_rms(o, gn_ref[...]) * sg_ref[:, sl].astype(F32)).astype(BF16)


def _inproj_hgrn_kernel(x_ref, ng_ref, sc_ref, sh_ref, w_ref, lb_ref, tri_ref, msk_ref, gn_ref,
                        o_ref, s_ref, h_scr, q_scr, lg_scr, iv_scr, sg_scr, st_scr,
                        *, chunk, levels, tiles_per_batch):
    i = pl.program_id(0)
    slot = i % 2
    prev = 1 - slot
    tm = x_ref.shape[0]
    assert tm // chunk == 4

    @pl.when(i == 0)
    def _():
        for scr in (q_scr, lg_scr, iv_scr, sg_scr):
            scr[1] = jnp.zeros(scr.shape[1:], scr.dtype)

    @pl.when((i + tiles_per_batch - 1) % tiles_per_batch == 0)
    def _():
        st_scr[...] = jnp.zeros_like(st_scr)

    h_scr[...] = _modulated(x_ref, ng_ref, sc_ref, sh_ref)
    lb = lb_ref[...]

    sections = [
        (q_scr, lambda z, cols: _silu(z)),
        (lg_scr, lambda z, cols: jnp.log(lb[:, cols] + (1.0 - lb[:, cols]) * jax.nn.sigmoid(z)) * LOG2_E),
        (iv_scr, lambda z, cols: z),
        (sg_scr, lambda z, cols: _silu(z).astype(BF16)),
    ]
    n_pieces = W_C // MXU_WIDTH
    for s, (scr, post) in enumerate(sections):
        rows = pl.ds(s * chunk, chunk)
        stages = _hgrn_chunk_stages(q_scr.at[prev, rows], lg_scr.at[prev, rows], iv_scr.at[prev, rows],
                                    sg_scr.at[prev, rows], tri_ref, msk_ref, gn_ref, o_ref.at[rows],
                                    st_scr, chunk, levels)
        for p in range(n_pieces):
            cols = slice(p * MXU_WIDTH, (p + 1) * MXU_WIDTH)
            z = _dot(h_scr[...], w_ref[:, s * W_C + p * MXU_WIDTH:s * W_C + (p + 1) * MXU_WIDTH])
            scr[slot, :, cols] = post(z, cols)
            next(stages, None)
            next(stages, None)
        for _ in stages:
            pass

    @pl.when(jnp.logical_and(i > 0, i % tiles_per_batch == 0))
    def _():
        for h in range(H_C):
            s_ref[h] = st_scr[h].T


def _inproj_hgrn(x, norm_g, scale, shift, w, lb, gn_g, batch, seq):
    tm, chunk = ROW_TILE, HGRN_CHUNK
    m = x.shape[0]
    nt = m // tm
    tiles_per_batch = seq // tm
    tri, masks, levels = _hgrn_tables(chunk)
    cur = lambda i: jnp.minimum(i, nt - 1)
    done = lambda i: jnp.maximum(i - 1, 0)
    mod_spec = pl.BlockSpec((None, 1, D_MODEL), lambda i: (cur(i) // tiles_per_batch, 0, 0))
    const = lambda shape: pl.BlockSpec(shape, lambda i: (0,) * len(shape))
    o, s = pl.pallas_call(
        functools.partial(_inproj_hgrn_kernel, chunk=chunk, levels=levels, tiles_per_batch=tiles_per_batch),
        grid=(nt + 1,),
        in_specs=[
            pl.BlockSpec((tm, D_MODEL), lambda i: (cur(i), 0)),
            const((1, D_MODEL)),
            mod_spec, mod_spec,
            _weight_spec(w),
            const((1, W_C)),
            const(tri.shape), const(masks.shape), const((1, DV_C)),
        ],
        out_specs=[pl.BlockSpec((tm, W_C), lambda i: (done(i), 0)),
                   pl.BlockSpec((None, H_C, DK_C, DV_C), lambda i: (done(i) // tiles_per_batch, 0, 0, 0))],
        out_shape=[jax.ShapeDtypeStruct((m, W_C), BF16),
                   jax.ShapeDtypeStruct((batch, H_C, DK_C, DV_C), F32)],
        scratch_shapes=[pltpu.VMEM((tm, D_MODEL), BF16),
                        pltpu.VMEM((2, tm, W_C), F32),
                        pltpu.VMEM((2, tm, W_C), F32),
                        pltpu.VMEM((2, tm, W_C), F32),
                        pltpu.VMEM((2, tm, W_C), BF16),
                        pltpu.VMEM((H_C, DV_C, DK_C), F32)],
        compiler_params=_cparams("arbitrary"),
        name="inproj_hgrn_recurrence",
    )(x, norm_g.reshape(1, D_MODEL), scale, shift, w[0], lb.reshape(1, W_C), tri, masks, gn_g.reshape(1, DV_C))
    return o, s


_DEC_ROWS = 2 * H_B


def _decode_head_rows(pieces):
    row = lax.broadcasted_iota(jnp.int32, (_DEC_ROWS, LANES), 0)
    out = jnp.zeros((_DEC_ROWS, LANES), F32)
    for h in range(H_B):
        out = jnp.where(row // 2 == h, jnp.broadcast_to(pieces[h], (_DEC_ROWS, LANES)), out)
    return out


def _decode_scores(q, kn, k_pages):
    row = lax.broadcasted_iota(jnp.int32, (_DEC_ROWS, LANES), 0)
    lane = lax.broadcasted_iota(jnp.int32, (_DEC_ROWS, LANES), 1)
    qm32 = jnp.where(lane // HD_B == row % 2,
                     _decode_head_rows([q[:, h * LANES:(h + 1) * LANES] for h in range(H_B)]), 0.0)
    qm = qm32.astype(BF16)
    s = jnp.concatenate([_dot_nt(qm, kp[...].astype(BF16)) for kp in k_pages], axis=1)
    key_head = lax.broadcasted_iota(jnp.int32, s.shape, 1) % H_B
    s = jnp.where(key_head == lax.broadcasted_iota(jnp.int32, s.shape, 0) // 2, s, NEG_INF)
    s_new = jnp.sum(qm32 * _decode_head_rows([kn[h:h + 1, :] for h in range(H_B)]), axis=-1, keepdims=True)
    return s, s_new


def _decode_finish(s, s_new, lam, vn, v_pages, subln_row, gb, lam_init):
    page_rows = H_B * PAGE_SIZE
    m = jnp.maximum(jnp.max(s, axis=-1, keepdims=True), s_new)
    p = jnp.exp2(s - m)
    p_new = jnp.exp2(s_new - m)
    l = jnp.sum(p, axis=-1, keepdims=True) + p_new
    comp = lax.broadcasted_iota(jnp.int32, (_DEC_ROWS, 1), 0) % 2
    wgt = jnp.where(comp == 0, 1.0, -lam) / l
    pw = (p * wgt).astype(BF16)
    o8 = (p_new * wgt) * _decode_head_rows([vn[h:h + 1, :] for h in range(H_B)])
    for j, vp in enumerate(v_pages):
        o8 = o8 + _dot(pw[:, j * page_rows:(j + 1) * page_rows], vp[...].astype(BF16))
    out = []
    for h in range(H_B):
        o = o8[2 * h:2 * h + 1, :] + o8[2 * h + 1:2 * h + 2, :]
        out.append(_rms(o, subln_row) * (1.0 - lam_init) * gb[:, h * DV_B:(h + 1) * DV_B])
    return jnp.concatenate(out, axis=1)


def _decode_conv_kernel(st_ref, u_ref, ga_ref, w_ref, cb_ref, lg_ref, lb_ref, *rest):
    y_ref, ns_ref = rest[-2:]
    u = u_ref[...]
    y = u * w_ref[CONV_W - 1:CONV_W, :] + cb_ref[...]
    for j in range(CONV_W - 1):
        y = y + st_ref[j] * w_ref[j:j + 1, :]
        if j > 0:
            ns_ref[j - 1] = st_ref[j]
    ns_ref[CONV_W - 2] = u
    y_ref[...] = _conv_post(y, lg_ref[...], lb_ref[...], ga_ref[...].astype(F32)).astype(BF16)


def _decode_conv(state_t, layer, u, ga, conv_w, conv_b, ln_g, ln_b, new_prev):
    n = u.shape[0]
    tile = 32
    vec = pl.BlockSpec((1, W_A), lambda i: (0, 0))
    sspec = pl.BlockSpec((None, CONV_W - 1, tile, W_A), lambda i: (layer, 0, i, 0))
    rspec = pl.BlockSpec((tile, W_A), lambda i: (i, 0))
    in_specs = [sspec, rspec, rspec, pl.BlockSpec((CONV_W, W_A), lambda i: (0, 0)), vec, vec, vec]
    args = [state_t, u, ga, conv_w, conv_b.reshape(1, W_A), ln_g.reshape(1, W_A), ln_b.reshape(1, W_A)]
    aliases = {}
    if new_prev is not None:
        aliases = {len(args): 1}
        in_specs.append(pl.BlockSpec(memory_space=pl.ANY))
        args.append(new_prev)
    return pl.pallas_call(
        _decode_conv_kernel,
        grid=(n // tile,),
        in_specs=in_specs,
        out_specs=[rspec, sspec],
        out_shape=[jax.ShapeDtypeStruct((n, W_A), BF16),
                   jax.ShapeDtypeStruct((N_AC, CONV_W - 1, n, W_A), F32)],
        input_output_aliases=aliases,
        compiler_params=_cparams("arbitrary"),
        name="decode_conv",
    )(*args)


def _decode_spread_table():
    rows = np.arange(3 * 3 * H_C)
    cols = np.arange(3 * W_C)
    same_vec = (rows[:, None] // (3 * H_C)) == (cols[None, :] // W_C)
    same_head = (rows[:, None] % H_C) == ((cols[None, :] % W_C) // DV_C)
    return jnp.asarray(same_vec & same_head, BF16)


def _decode_hgrn_kernel(q_ref, lg_ref, i_ref, sg_ref, gn_ref, e_ref, s_ref, *rest, n_seq):
    o_ref, ns_ref = rest[-2:]

    def split_rows(x):
        return [p.astype(F32) for p in _split3(x)]

    def body(n, carry):
        f8 = jnp.exp2(lg_ref[n])
        parts = jnp.concatenate(split_rows(f8) + split_rows(1.0 - f8) + split_rows(q_ref[n]), axis=0)
        spread = _dot(parts.T.astype(BF16), e_ref[...])
        v8 = i_ref[n]
        rows = []
        for h in range(H_C):
            f_b, k_b, q_b =(spread[:, j * W_C + h * DV_C:j * W_C + (h + 1) * DV_C] for j in range(3))
            s_new = f_b * s_ref[n, h] + k_b * v8[h:h + 1, :]
            ns_ref[n, h] = s_new
            rows.append(jnp.sum(q_b * s_new, axis=0, keepdims=True))
        o = jnp.concatenate(rows, axis=0)
        o_ref[n] = (_rms(o, gn_ref[...]) * sg_ref[n].astype(F32)).astype(BF16)
        return carry

    lax.fori_loop(0, n_seq, body, 0)


def _decode_hgrn(qs, lg, iv, sg, gn_g, state_all, layer, new_prev):
    n = qs.shape[0]
    tile = SAMPLE_STATE_TILE
    hspec = pl.BlockSpec((tile, H_C, DV_C), lambda i: (i, 0, 0))
    sspec = pl.BlockSpec((None, tile, H_C, DK_C, DV_C), lambda i: (layer, i, 0, 0, 0))
    h3 = lambda a: a.reshape(n, H_C, DV_C)
    table = _decode_spread_table()
    in_specs = [hspec, hspec, hspec, hspec, pl.BlockSpec((1, DV_C), lambda i: (0, 0)),
                pl.BlockSpec(table.shape, lambda i: (0, 0)), sspec]
    args = [h3(qs), h3(lg), h3(iv), h3(sg), gn_g.reshape(1, DV_C), table, state_all]
    aliases = {}
    if new_prev is not None:
        aliases = {len(args): 1}
        in_specs.append(pl.BlockSpec(memory_space=pl.ANY))
        args.append(new_prev)
    o, ns = pl.pallas_call(
        functools.partial(_decode_hgrn_kernel, n_seq=tile),
        grid=(n // tile,),
        in_specs=in_specs,
        out_specs=[hspec, sspec],
        out_shape=[jax.ShapeDtypeStruct((n, H_C, DV_C), BF16),
                   jax.ShapeDtypeStruct((N_C, n, H_C, DK_C, DV_C), F32)],
        input_output_aliases=aliases,
        compiler_params=_cparams("arbitrary"),
        name="decode_hgrn",
    )(*args)
    return o.reshape(n, W_C), ns


def _rope_tables(pos, rows):
    half = ROT_DIM // 2
    inv_freq = ROPE_THETA ** (-jnp.arange(half, dtype=F32) / half)
    ang = pos.astype(F32)[:, None] * inv_freq[None, :]
    cos, sin = jnp.cos(ang), jnp.sin(ang)
    n = pos.shape[0]
    ones = jnp.ones((n, HD_B - ROT_DIM), F32)
    zeros = jnp.zeros((n, HD_B - ROT_DIM), F32)
    z8 = jnp.zeros((n, half), F32)
    cos_t = jnp.concatenate([cos, cos, ones], axis=1)
    sa_t = jnp.concatenate([-sin, z8, zeros], axis=1)
    sb_t = jnp.concatenate([z8, sin, zeros], axis=1)
    out = []
    for t in (cos_t, sa_t, sb_t):
        t = jnp.concatenate([t, t], axis=1)
        out.append(jnp.broadcast_to(t, (rows, LANES)) if n == 1 else t)
    return out


def kernel(x_prompt, x_sample, c_prompt, c_sample, cache_k, cache_v, page_table, state_conv, state_hgrn,
           norm_g, ada_w, ada_b, w_in_ac, w_out_ac, conv_w, conv_b, ln_g, ln_b, lam_q1, lam_k1, lam_q2,
           lam_k2, subln_g, w_in_c, w_out_c, gn_g, lb_logits, final_g):
    bp, tp = x_prompt.shape[:2]
    bs, ts = x_sample.shape[:2]
    assert ts == 1
    n_pool = cache_k.shape[1]
    n_past = page_table.shape[1] * PAGE_SIZE
    mp = bp * tp

    lb_sm = jax.nn.softmax(lb_logits.astype(F32), axis=0)
    lb_all = jnp.cumsum(lb_sm, axis=0) - lb_sm[0]

    mod = _ada_modulation(jnp.concatenate([c_prompt, c_sample], axis=0), ada_w, ada_b)

    def mods(l):
        out = []
        for j in range(3):
            m = mod[l, :, j * D_MODEL:(j + 1) * D_MODEL]
            out.append((m[:bp].reshape(bp, 1, D_MODEL), m[bp:].reshape(1, bs, D_MODEL)))
        return out

    tabs_p = _rope_tables(jnp.arange(tp), tp)
    tabs_s = _rope_tables(jnp.full((1,), n_past), bs)
    ck4 = cache_k.reshape(N_AC, n_pool, PAGE_SIZE * H_B, 2 * HD_B)
    cv4 = cache_v.reshape(N_AC, n_pool, PAGE_SIZE * H_B, DV_B)

    w_in_ac_b, w_out_ac_b, w_in_c_b, w_out_c_b = (w.astype(BF16) for w in (w_in_ac, w_out_ac, w_in_c, w_out_c))

    tiles_p = tp // ROW_TILE
    hp = x_prompt.reshape(mp, D_MODEL)
    hs = x_sample.reshape(bs, D_MODEL)
    state_conv_t = jnp.swapaxes(state_conv, 1, 2)
    cp_l, sp_l = [], []
    kv_p = kv_s = ss = cs = None
    for l in range(DEPTH):
        (sh_p, sh_s), (sc_p, sc_s), (gt_p, gt_s) = mods(l)
        last = final_g if l == DEPTH - 1 else None
        if l % 2 == 0:
            a = l // 2
            lam_init = 0.8 - 0.6 * math.exp(-0.3 * l)
            lam = (jnp.exp(jnp.sum(lam_q1[a].astype(F32) * lam_k1[a].astype(F32)))
                   - jnp.exp(jnp.sum(lam_q2[a].astype(F32) * lam_k2[a].astype(F32))) + lam_init)
            lam = jnp.full((1, LANES), lam, F32)
            w_in, w_out = (w_in_ac_b, a), (w_out_ac_b, a)
            q, k, kb, v, vb, gb, u, ga = _inproj_ac(hp, norm_g[l], sc_p, sh_p, w_in, *tabs_p,
                                                    ROW_TILE, tiles_p, a, kv_p)
            kv_p = (k, v)
            qs, k, _, v, _, gbs, us, gas = _inproj_ac(hs, norm_g[l], sc_s, sh_s, w_in, *tabs_s, bs, 1, a, kv_s)
            kv_s = (k, v)
            ob, obs = _attention(q, kb, vb, gb, bp, tp, qs, k, v, gbs, ck4, cv4, a, page_table,
                                 lam, subln_g[a], lam_init)
            y, cst = _prompt_conv(u, ga, conv_w[a], conv_b[a], ln_g[a], ln_b[a], bp, tp)
            hp = _outproj([ob, y], w_out, hp, gt_p, ROW_TILE, tiles_p, last)
            cp_l.append(cst)
            y, cs = _decode_conv(state_conv_t, a, us, gas, conv_w[a], conv_b[a], ln_g[a], ln_b[a], cs)
            hs = _outproj([obs, y], w_out, hs, gt_s, bs, 1, last)
        else:
            ci = l // 2
            w_in, w_out = (w_in_c_b, ci), (w_out_c_b, ci)
            o, st = _inproj_hgrn(hp, norm_g[l], sc_p, sh_p, w_in, lb_all[l], gn_g[ci], bp, tp)
            hp = _outproj([o], w_out, hp, gt_p, ROW_TILE, tiles_p, last)
            sp_l.append(st)
            qs, lg, iv, sg = _inproj_c(hs, norm_g[l], sc_s, sh_s, w_in, lb_all[l], bs, 1)
            o, ss = _decode_hgrn(qs, lg, iv, sg, gn_g[ci], state_hgrn, ci, ss)
            hs = _outproj([o], w_out, hs, gt_s, bs, 1, last)

    y_prompt = hp.reshape(bp, tp, D_MODEL)
    y_sample = hs.reshape(bs, ts, D_MODEL)
    return (y_prompt, y_sample,
            kv_p[0].reshape(N_AC, bp, tp, H_B, 2 * HD_B), kv_p[1].reshape(N_AC, bp, tp, H_B, DV_B),
            jnp.stack(cp_l), jnp.stack(sp_l),
            kv_s[0].reshape(N_AC, bs, ts, H_B, 2 * HD_B), kv_s[1].reshape(N_AC, bs, ts, H_B, DV_B),
            jnp.swapaxes(cs, 1, 2), ss)
```

```python
import functools
import math

import numpy as np
import jax
import jax.numpy as jnp
from jax import lax
from jax.experimental import pallas as pl
from jax.experimental.pallas import tpu as pltpu

F32 = jnp.float32
BF16 = jnp.bfloat16

D_MODEL = 1024
DEPTH = 4
PAGE_SIZE = 128
N_AC = (DEPTH + 1) // 2
N_C = DEPTH // 2

H_B = 4
HD_B = 64
DV_B = 2 * HD_B
W_B = H_B * DV_B
QK_W = H_B * 2 * HD_B
ROT_DIM = HD_B // 4
ROPE_THETA = 500000.0
NEG_INF = -1e30

W_A = D_MODEL // 2
CONV_W = 31

H_C = 8
DK_C = D_MODEL // H_C
DV_C = D_MODEL // H_C
W_C = H_C * DV_C

EPS = 1e-6
LOG2_E = math.log2(math.e)
Q_SCALE = HD_B ** -0.5 * LOG2_E

AC_IN = 2 * QK_W + 2 * W_B + 3 * W_A
C_IN = 2 * H_C * DK_C + 2 * W_C

LANES = 128
SUBLANES = 8
MXU_WIDTH = 256
VMEM_LIMIT_BYTES = 56 * 1024 * 1024

ROW_TILE = 512
OUT_ROW_TILE = 1024
ATTN_TILE = 256
CONV_TILE = 512
CONV_HALO = 32
HGRN_CHUNK = 128
SAMPLE_STATE_TILE = 8


def _cparams(*sem):
    return pltpu.CompilerParams(dimension_semantics=sem, vmem_limit_bytes=VMEM_LIMIT_BYTES)


def _silu(x):
    return x * jax.nn.sigmoid(x)


def _dot(a, b):
    return jnp.dot(a, b, preferred_element_type=F32)


def _dot_nt(a, b):
    return lax.dot_general(a, b, (((1,), (1,)), ((), ())), preferred_element_type=F32)


def _dot_tn(a, b):
    return lax.dot_general(a, b, (((0,), (0,)), ((), ())), preferred_element_type=F32)


def _rms(x, g):
    return x * lax.rsqrt(jnp.mean(x * x, axis=-1, keepdims=True) + EPS) * g


def _ada_kernel(c_ref, w_ref, b_ref, o_ref):
    s = _silu(c_ref[...]).astype(BF16)
    o_ref[...] = _dot(s, w_ref[...].astype(BF16)) + b_ref[...]


def _ada_modulation(c_all, ada_w, ada_b):
    rows = c_all.shape[0]
    return pl.pallas_call(
        _ada_kernel,
        grid=(DEPTH, 3),
        in_specs=[
            pl.BlockSpec((rows, D_MODEL), lambda l, j: (0, 0)),
            pl.BlockSpec((None, D_MODEL, D_MODEL), lambda l, j: (l, 0, j)),
            pl.BlockSpec((None, 1, D_MODEL), lambda l, j: (l, 0, j)),
        ],
        out_specs=pl.BlockSpec((None, rows, D_MODEL), lambda l, j: (l, 0, j)),
        out_shape=jax.ShapeDtypeStruct((DEPTH, rows, 3 * D_MODEL), F32),
        compiler_params=_cparams("arbitrary", "arbitrary"),
        name="ada_modulation",
    )(c_all, ada_w, ada_b.reshape(DEPTH, 1, 3 * D_MODEL))


def _weight_spec(w):
    stacked, layer = w
    return pl.BlockSpec((None,) + stacked.shape[1:], lambda i: (layer, 0, 0))


def _modulated(x_ref, ng_ref, sc_ref, sh_ref):
    h = _rms(x_ref[...], ng_ref[...])
    return (h * (1.0 + sc_ref[...]) + sh_ref[...]).astype(BF16)


def _inproj_ac_kernel(x_ref, ng_ref, sc_ref, sh_ref, w_ref, cos_ref, sa_ref, sb_ref, *rest):
    q_ref, k_ref, kb_ref, v_ref, vb_ref, gb_ref, u_ref, ga_ref, h_scr = rest[-9:]
    h_scr[...] = _modulated(x_ref, ng_ref, sc_ref, sh_ref)

    def proj(s):
        return _dot(h_scr[...], w_ref[:, s * QK_W:(s + 1) * QK_W])

    cos, sa, sb = cos_ref[...], sa_ref[...], sb_ref[...]

    def rope(z):
        return (z * cos + pltpu.roll(z, LANES - ROT_DIM // 2, 1) * sa
                + pltpu.roll(z, ROT_DIM // 2, 1) * sb)

    zq = proj(0)
    for g in range(H_B):
        sl = slice(g * LANES, (g + 1) * LANES)
        q_ref[:, sl] = (rope(zq[:, sl]) * Q_SCALE).astype(BF16)
    tm = x_ref.shape[0]
    zk = proj(1)
    for g in range(H_B):
        sl = slice(g * LANES, (g + 1) * LANES)
        r = rope(zk[:, sl])
        k_ref[pl.ds(g, tm, stride=H_B), :] = r
        kb_ref[:, sl] = r.astype(BF16)
    zv = proj(2)
    for g in range(H_B):
        v_ref[pl.ds(g, tm, stride=H_B), :] = zv[:, g * LANES:(g + 1) * LANES]
    vb_ref[...] = zv.astype(BF16)
    gb_ref[...] = _silu(proj(3)).astype(BF16)
    a_val = proj(4)
    u_ref[...] = a_val * jax.nn.sigmoid(proj(5))
    ga_ref[...] = _silu(proj(6)).astype(BF16)


def _mod_specs(rows_per_mod, tm, tiles_per_mod):
    r = rows_per_mod
    return pl.BlockSpec((None, r, D_MODEL), lambda i: (i // tiles_per_mod, 0, 0))


def _inproj_ac(x, norm_g, scale, shift, w, cos_t, sa_t, sb_t, tm, tiles_per_mod, layer, kv_prev):
    m = x.shape[0]
    tab_tiles = cos_t.shape[0] // tm
    mod_spec = _mod_specs(scale.shape[1], tm, tiles_per_mod)
    row512 = pl.BlockSpec((tm, QK_W), lambda i: (i, 0))
    heads = pl.BlockSpec((None, H_B * tm, LANES), lambda i: (layer, i, 0))
    tab_spec = pl.BlockSpec((tm, LANES), lambda i: (i % tab_tiles, 0))
    out_shape = [
        jax.ShapeDtypeStruct((m, QK_W), BF16),
        jax.ShapeDtypeStruct((N_AC, H_B * m, LANES), F32),
        jax.ShapeDtypeStruct((m, QK_W), BF16),
        jax.ShapeDtypeStruct((N_AC, H_B * m, LANES), F32),
        jax.ShapeDtypeStruct((m, W_B), BF16),
        jax.ShapeDtypeStruct((m, W_B), BF16),
        jax.ShapeDtypeStruct((m, W_A), F32),
        jax.ShapeDtypeStruct((m, W_A), BF16),
    ]
    in_specs = [
        pl.BlockSpec((tm, D_MODEL), lambda i: (i, 0)),
        pl.BlockSpec((1, D_MODEL), lambda i: (0, 0)),
        mod_spec, mod_spec,
        _weight_spec(w),
        tab_spec, tab_spec, tab_spec,
    ]
    args = [x, norm_g.reshape(1, D_MODEL), scale, shift, w[0], cos_t, sa_t, sb_t]
    aliases = {}
    if kv_prev is not None:
        aliases = {len(args): 1, len(args) + 1: 3}
        in_specs += [pl.BlockSpec(memory_space=pl.ANY)] * 2
        args += list(kv_prev)
    return pl.pallas_call(
        _inproj_ac_kernel,
        grid=(m // tm,),
        in_specs=in_specs,
        out_specs=[row512, heads, row512, heads, row512, row512, row512, row512],
        out_shape=out_shape,
        input_output_aliases=aliases,
        scratch_shapes=[pltpu.VMEM((tm, D_MODEL), BF16)],
        compiler_params=_cparams("arbitrary"),
        name="inproj_conv_attn",
    )(*args)


def _inproj_c_kernel(x_ref, ng_ref, sc_ref, sh_ref, w_ref, lb_ref, q_ref, lg_ref, i_ref, sg_ref, h_scr):
    h_scr[...] = _modulated(x_ref, ng_ref, sc_ref, sh_ref)

    def proj(s):
        return _dot(h_scr[...], w_ref[:, s * W_C:(s + 1) * W_C])

    q_ref[...] = _silu(proj(0))
    lb = lb_ref[...]
    lg_ref[...] = jnp.log(lb + (1.0 - lb) * jax.nn.sigmoid(proj(1))) * LOG2_E
    i_ref[...] = proj(2)
    sg_ref[...] = _silu(proj(3)).astype(BF16)


def _inproj_c(x, norm_g, scale, shift, w, lb, tm, tiles_per_mod):
    m = x.shape[0]
    mod_spec = _mod_specs(scale.shape[1], tm, tiles_per_mod)
    row = pl.BlockSpec((tm, W_C), lambda i: (i, 0))
    out_shape = [
        jax.ShapeDtypeStruct((m, W_C), F32),
        jax.ShapeDtypeStruct((m, W_C), F32),
        jax.ShapeDtypeStruct((m, W_C), F32),
        jax.ShapeDtypeStruct((m, W_C), BF16),
    ]
    return pl.pallas_call(
        _inproj_c_kernel,
        grid=(m // tm,),
        in_specs=[
            pl.BlockSpec((tm, D_MODEL), lambda i: (i, 0)),
            pl.BlockSpec((1, D_MODEL), lambda i: (0, 0)),
            mod_spec, mod_spec,
            _weight_spec(w),
            pl.BlockSpec((1, W_C), lambda i: (0, 0)),
        ],
        out_specs=[row] * 4,
        out_shape=out_shape,
        scratch_shapes=[pltpu.VMEM((tm, D_MODEL), BF16)],
        compiler_params=_cparams("arbitrary"),
        name="inproj_hgrn",
    )(x, norm_g.reshape(1, D_MODEL), scale, shift, w[0], lb.reshape(1, W_C))


def _outproj_kernel(*refs, n_act, final_norm):
    acts = refs[:n_act]
    w_ref, x_ref, gate_ref = refs[n_act:n_act + 3]
    o_ref = refs[-1]
    acc = None
    lo = 0
    for a_ref in acts:
        width = a_ref.shape[-1]
        part = _dot(a_ref[...], w_ref[lo:lo + width, :])
        acc = part if acc is None else acc + part
        lo += width
    y = x_ref[...] + gate_ref[...] * acc
    if final_norm:
        y = _rms(y, refs[n_act + 3][...])
    o_ref[...] = y


def _outproj(acts, w, x, gate, tm, tiles_per_mod, final_g=None):
    m = x.shape[0]
    mod_spec = _mod_specs(gate.shape[1], tm, tiles_per_mod)
    in_specs = [pl.BlockSpec((tm, a.shape[1]), lambda i: (i, 0)) for a in acts]
    in_specs += [
        _weight_spec(w),
        pl.BlockSpec((tm, D_MODEL), lambda i: (i, 0)),
        mod_spec,
    ]
    args = list(acts) + [w[0], x, gate]
    if final_g is not None:
        in_specs.append(pl.BlockSpec((1, D_MODEL), lambda i: (0, 0)))
        args.append(final_g.reshape(1, D_MODEL))
    return pl.pallas_call(
        functools.partial(_outproj_kernel, n_act=len(acts), final_norm=final_g is not None),
        grid=(m // tm,),
        in_specs=in_specs,
        out_specs=pl.BlockSpec((tm, D_MODEL), lambda i: (i, 0)),
        out_shape=jax.ShapeDtypeStruct((m, D_MODEL), F32),
        compiler_params=_cparams("arbitrary"),
        name="outproj",
    )(*args)


def _attn_kernel(pt_ref, lam_ref, sgc_ref, sgr_ref, q_ref, k_ref, v_ref, gb_ref,
                 qs_ref, kn_ref, vn_ref, gbs_ref, *rest, tile, lam_init, n_dec, n_pages):
    pages = rest[:2 * n_dec * n_pages]
    o_ref, os_ref, q2_scr, s_a, s_b, m_scr, l_scr, acc_scr = rest[2 * n_dec * n_pages:]
    sg_ref = sgc_ref
    qi = pl.program_id(1)
    heads = [slice(h * LANES, (h + 1) * LANES) for h in range(H_B)]
    lane = lax.broadcasted_iota(jnp.int32, (tile, LANES), 1)
    for h, sl in enumerate(heads):
        q = q_ref[:, sl]
        zero = jnp.zeros_like(q)
        q2_scr[h] = jnp.concatenate([jnp.where(lane < HD_B, q, zero), jnp.where(lane >= HD_B, q, zero)], axis=0)
    m_scr[...] = jnp.full(m_scr.shape, NEG_INF, F32)
    l_scr[...] = jnp.zeros(l_scr.shape, F32)
    acc_scr[...] = jnp.zeros(acc_scr.shape, F32)

    def scores(t, s_scr):
        start = pl.multiple_of(t * tile, tile)
        for h, sl in enumerate(heads):
            s_scr[h] = _dot_nt(k_ref[pl.ds(start, tile), sl], q2_scr[h])

    def softmax_part(s_scr, mask):
        parts = []
        for h in range(H_B):
            sh = s_scr[h]
            if mask is not None:
                sh = jnp.where(mask, sh, NEG_INF)
            m_old = m_scr[h]
            m_new = jnp.maximum(m_old, jnp.max(sh, axis=0, keepdims=True))
            alpha = jnp.exp2(m_old - m_new)
            p = jnp.exp2(sh - m_new)
            m_scr[h] = m_new
            l_scr[h] = alpha * l_scr[h] + jnp.sum(p, axis=0, keepdims=True)
            parts.append((alpha, p.astype(BF16)))
        return parts

    def value_part(t, parts):
        start = pl.multiple_of(t * tile, tile)
        for h, sl in enumerate(heads):
            alpha, p = parts[h]
            acc_scr[h] = alpha * acc_scr[h] + _dot_tn(v_ref[pl.ds(start, tile), sl], p)

    def consume(s_scr, t, mask):
        value_part(t, softmax_part(s_scr, mask))

    odd = qi % 2

    @pl.when(odd == 1)
    def _():
        scores(0, s_b)
        scores(1, s_a)
        consume(s_b, 0, None)

    @pl.when(odd == 0)
    def _():
        scores(0, s_a)

    def body(jj, carry):
        t = odd + 2 * jj
        scores(t + 1, s_b)
        consume(s_a, t, None)
        scores(t + 2, s_a)
        consume(s_b, t + 1, None)
        return carry

    lax.fori_loop(0, qi // 2, body, 0)
    key = lax.broadcasted_iota(jnp.int32, (tile, 2 * tile), 0)
    qry = lax.broadcasted_iota(jnp.int32, (tile, 2 * tile), 1)
    diag = softmax_part(s_a, key <= jnp.where(qry >= tile, qry - tile, qry))
    lam = lam_ref[:, 0:1]
    dec = []
    for r in range(n_dec):
        k_pages = pages[2 * r * n_pages:(2 * r + 1) * n_pages]
        v_pages = pages[(2 * r + 1) * n_pages:(2 * r + 2) * n_pages]
        dec.append(_decode_scores(qs_ref[r].astype(F32), kn_ref[r], k_pages) + (v_pages,))
    value_part(qi, diag)
    for r, (s, s_new, v_pages) in enumerate(dec):
        os_ref[r] = _decode_finish(s, s_new, lam, vn_ref[r], v_pages, sgr_ref[...],
                                   gbs_ref[r].astype(F32), lam_init).astype(BF16)
    for h, sl in enumerate(heads):
        l, acc = l_scr[h], acc_scr[h]
        o = acc[:, :tile] / l[:, :tile] - lam_ref[:, 0:1] * (acc[:, tile:] / l[:, tile:])
        y = o * lax.rsqrt(jnp.mean(o * o, axis=0, keepdims=True) + EPS) * sg_ref[...]
        o_ref[:, sl] = (y.T * (1.0 - lam_init) * gb_ref[:, sl].astype(F32)).astype(BF16)


def _attention(q, kb, vb, gb, batch, seq, qs, kn_all, vn_all, gbs, cache_k4, cache_v4, layer, page_table,
               lam, subln_g, lam_init):
    tile = ATTN_TILE
    nq = seq // tile
    n_seq, n_pages = page_table.shape
    n_dec = n_seq // (batch * nq)
    assert n_dec * batch * nq == n_seq
    q3, k3, v3, g3 = (a.reshape(batch, seq, QK_W) for a in (q, kb, vb, gb))
    qspec = pl.BlockSpec((None, tile, QK_W), lambda b, i, pt: (b, i, 0))
    kvspec = pl.BlockSpec((None, seq, QK_W), lambda b, i, pt: (b, 0, 0))
    rows = pl.BlockSpec((n_dec, 1, QK_W), lambda b, i, pt: (b * nq + i, 0, 0))
    new_kv = pl.BlockSpec((None, n_dec, H_B, LANES), lambda b, i, pt: (layer, b * nq + i, 0, 0))

    def page_spec(r, j):
        return pl.BlockSpec((None, None, H_B * PAGE_SIZE, LANES),
                            lambda b, i, pt: (layer, pt[(b * nq + i) * n_dec + r, j], 0, 0))

    page_specs, page_args = [], []
    for r in range(n_dec):
        for cache in (cache_k4, cache_v4):
            page_specs += [page_spec(r, j) for j in range(n_pages)]
            page_args += [cache] * n_pages
    grid_spec = pltpu.PrefetchScalarGridSpec(
        num_scalar_prefetch=1,
        grid=(batch, nq),
        in_specs=[pl.BlockSpec((1, LANES), lambda b, i, pt: (0, 0)),
                  pl.BlockSpec((DV_B, 1), lambda b, i, pt: (0, 0)),
                  pl.BlockSpec((1, DV_B), lambda b, i, pt: (0, 0)),
                  qspec, kvspec, kvspec, qspec, rows, new_kv, new_kv, rows] + page_specs,
        out_specs=[qspec, rows],
        scratch_shapes=[pltpu.VMEM((H_B, 2 * tile, LANES), BF16),
                        pltpu.VMEM((H_B, tile, 2 * tile), F32),
                        pltpu.VMEM((H_B, tile, 2 * tile), F32),
                        pltpu.VMEM((H_B, 1, 2 * tile), F32),
                        pltpu.VMEM((H_B, 1, 2 * tile), F32),
                        pltpu.VMEM((H_B, DV_B, 2 * tile), F32)],
    )
    r3 = lambda a: a.reshape(n_seq, 1, QK_W)
    h4 = lambda a: a.reshape(N_AC, n_seq, H_B, LANES)
    out, out_s = pl.pallas_call(
        functools.partial(_attn_kernel, tile=tile, lam_init=lam_init, n_dec=n_dec, n_pages=n_pages),
        grid_spec=grid_spec,
        out_shape=[jax.ShapeDtypeStruct((batch, seq, W_B), BF16),
                   jax.ShapeDtypeStruct((n_seq, 1, W_B), BF16)],
        compiler_params=_cparams("arbitrary", "arbitrary"),
        name="diff_attention",
    )(page_table, lam, subln_g.reshape(DV_B, 1), subln_g.reshape(1, DV_B), q3, k3, v3, g3,
      r3(qs), h4(kn_all), h4(vn_all), r3(gbs), *page_args)
    return out.reshape(batch * seq, W_B), out_s.reshape(n_seq, W_B)


def _conv_post(y, ln_g, ln_b, ga):
    yc = y - jnp.mean(y, axis=-1, keepdims=True)
    yn = yc * lax.rsqrt(jnp.mean(yc * yc, axis=-1, keepdims=True) + EPS) * ln_g + ln_b
    return _silu(yn) * ga


def _conv_kernel(u_ref, ga_ref, w_ref, cb_ref, lg_ref, lb_ref, y_ref, st_ref, ext_scr, sh_scr, *, tile, rows):
    t = pl.program_id(1)
    first = CONV_HALO - (CONV_W - 1)
    sh_rows = sh_scr.shape[1]

    @pl.when(t == 0)
    def _():
        ext_scr[0:CONV_HALO, :] = jnp.zeros((CONV_HALO, W_A), F32)

    @pl.when(t > 0)
    def _():
        ext_scr[0:CONV_HALO, :] = ext_scr[tile:tile + CONV_HALO, :]

    ext_scr[CONV_HALO:CONV_HALO + tile, :] = u_ref[...]
    for r in range(1, SUBLANES):
        sh_scr[r - 1] = ext_scr[r:r + sh_rows, :]
    for r0 in range(0, tile, rows):
        acc = jnp.zeros((rows, W_A), F32) + cb_ref[...]
        for j in range(CONV_W):
            off = first + j
            r, base = off % SUBLANES, off - off % SUBLANES + r0
            src = ext_scr[base:base + rows, :] if r == 0 else sh_scr[r - 1, base:base + rows, :]
            acc = acc + src * w_ref[j:j + 1, :]
        y_ref[r0:r0 + rows, :] = _conv_post(acc, lg_ref[...], lb_ref[...],
                                            ga_ref[r0:r0 + rows, :].astype(F32)).astype(BF16)

    @pl.when(t == pl.num_programs(1) - 1)
    def _():
        st_ref[...] = ext_scr[CONV_HALO + tile - (CONV_W - 1):CONV_HALO + tile, :]


def _prompt_conv(u, ga, conv_w, conv_b, ln_g, ln_b, batch, seq):
    tile = CONV_TILE
    u3, g3 = u.reshape(batch, seq, W_A), ga.reshape(batch, seq, W_A)
    tspec = pl.BlockSpec((None, tile, W_A), lambda b, t: (b, t, 0))
    vec = pl.BlockSpec((1, W_A), lambda b, t: (0, 0))
    y, st = pl.pallas_call(
        functools.partial(_conv_kernel, tile=tile, rows=64),
        grid=(batch, seq // tile),
        in_specs=[tspec, tspec, pl.BlockSpec((CONV_W, W_A), lambda b, t: (0, 0)), vec, vec, vec],
        out_specs=[tspec, pl.BlockSpec((None, CONV_W - 1, W_A), lambda b, t: (b, 0, 0))],
        out_shape=[jax.ShapeDtypeStruct((batch, seq, W_A), BF16),
                   jax.ShapeDtypeStruct((batch, CONV_W - 1, W_A), F32)],
        scratch_shapes=[pltpu.VMEM((CONV_HALO + tile, W_A), F32),
                        pltpu.VMEM((SUBLANES - 1, CONV_HALO + tile - SUBLANES, W_A), F32)],
        compiler_params=_cparams("arbitrary", "arbitrary"),
        name="prompt_conv",
    )(u3, g3, conv_w, conv_b.reshape(1, W_A), ln_g.reshape(1, W_A), ln_b.reshape(1, W_A))
    return y.reshape(batch * seq, W_A), st


def _hgrn_tables(chunk):
    levels = int(math.log2(chunk))
    t = np.arange(chunk)[:, None]
    u = np.arange(chunk)[None, :]
    masks = []
    for v in range(levels):
        m = 1 << v
        masks.append((t // (2 * m) == u // (2 * m)) & (t % (2 * m) >= m) & (u % (2 * m) < m))
    return jnp.asarray(u <= t, BF16), jnp.asarray(np.stack(masks), BF16), levels


def _split3(x):
    hi = x.astype(BF16)
    r = x - hi.astype(F32)
    mid = r.astype(BF16)
    lo = (r - mid.astype(F32)).astype(BF16)
    return hi, mid, lo


def _hgrn_level_exponent(b, g2, m, row):
    chunk, width = b.shape
    if m == 1:
        return jnp.where((row & 1) != 0, g2, 0.0)
    if 2 * m < SUBLANES:
        b3 = b.reshape(chunk // SUBLANES, SUBLANES, width)
        sub = lax.broadcasted_iota(jnp.int32, b3.shape, 1)
        bm = b3[:, m - 1:m, :]
        for blk in range(1, SUBLANES // (2 * m)):
            bm = jnp.where(sub < blk * 2 * m, bm, b3[:, blk * 2 * m + m - 1:blk * 2 * m + m, :])
    else:
        b3 = b.reshape(chunk // (2 * m), 2 * m, width)
        bm = b3[:, m - 1:m, :]
    d = lax.bitcast_convert_type((b3 - bm).reshape(chunk, width), jnp.uint32)
    return lax.bitcast_convert_type(d | jnp.uint32(0x80000000), F32)


def _hgrn_chunk_stages(q_ref, lg_ref, i_ref, sg_ref, tri_ref, msk_ref, gn_ref, o_ref, st_scr, chunk, levels):
    heads = [slice(h * LANES, (h + 1) * LANES) for h in range(H_C)]
    g2 = lg_ref[...]
    q = q_ref[...]
    v = i_ref[...]
    k = 1.0 - jnp.exp2(g2)
    b3 = _dot(tri_ref[...], jnp.concatenate(_split3(g2), axis=1))
    yield
    b = b3[:, :W_C] + b3[:, W_C:2 * W_C] + b3[:, 2 * W_C:]
    row = lax.broadcasted_iota(jnp.int32, (chunk, W_C), 0)
    qb, kb = q.astype(BF16), k.astype(BF16)
    a = [jnp.zeros((chunk, chunk), BF16)] * H_C
    for lv in range(levels):
        m = 1 << lv
        x = jnp.where((row & m) != 0, qb, kb) * jnp.exp2(_hgrn_level_exponent(b, g2, m, row).astype(BF16))
        msk = msk_ref[lv]
        a = [a[h] + msk * _dot_nt(x[:, sl], x[:, sl]).astype(BF16) for h, sl in enumerate(heads)]
        yield
    vb = v.astype(BF16)
    qe = (q * jnp.exp2(b)).astype(BF16)
    b_last = b[chunk - 1:chunk, :]
    kd = (k * jnp.exp2(b_last - b)).astype(BF16)
    decay = jnp.exp2(b_last)
    qk = q * k
    for h, sl in enumerate(heads):
        st = st_scr[h]
        o = _dot(a[h], vb[:, sl]) + jnp.sum(qk[:, sl], axis=1, keepdims=True) * v[:, sl]
        o = o + _dot_nt(qe[:, sl], st.astype(BF16))
        st_scr[h] = st * decay[:, sl] + _dot_tn(vb[:, sl], kd[:, sl])
        o_ref[:, sl] = (_rms(o, gn_ref[...]) * sg_ref[:, sl].astype(F32)).astype(BF16)


def _inproj_hgrn_kernel(x_ref, ng_ref, sc_ref, sh_ref, w_ref, lb_ref, tri_ref, msk_ref, gn_ref,
                        o_ref, s_ref, h_scr, q_scr, lg_scr, iv_scr, sg_scr, st_scr,
                        *, chunk, levels, tiles_per_batch):
    i = pl.program_id(0)
    slot = i % 2
    prev = 1 - slot
    tm = x_ref.shape[0]
    assert tm // chunk == 4

    @pl.when(i == 0)
    def _():
        for scr in (q_scr, lg_scr, iv_scr, sg_scr):
            scr[1] = jnp.zeros(scr.shape[1:], scr.dtype)

    @pl.when((i + tiles_per_batch - 1) % tiles_per_batch == 0)
    def _():
        st_scr[...] = jnp.zeros_like(st_scr)

    h_scr[...] = _modulated(x_ref, ng_ref, sc_ref, sh_ref)
    lb = lb_ref[...]

    sections = [
        (q_scr, lambda z, cols: _silu(z)),
        (lg_scr, lambda z, cols: jnp.log(lb[:, cols] + (1.0 - lb[:, cols]) * jax.nn.sigmoid(z)) * LOG2_E),
        (iv_scr, lambda z, cols: z),
        (sg_scr, lambda z, cols: _silu(z).astype(BF16)),
    ]
    n_pieces = W_C // MXU_WIDTH
    for s, (scr, post) in enumerate(sections):
        rows = pl.ds(s * chunk, chunk)
        stages = _hgrn_chunk_stages(q_scr.at[prev, rows], lg_scr.at[prev, rows], iv_scr.at[prev, rows],
                                    sg_scr.at[prev, rows], tri_ref, msk_ref, gn_ref, o_ref.at[rows],
                                    st_scr, chunk, levels)
        for p in range(n_pieces):
            cols = slice(p * MXU_WIDTH, (p + 1) * MXU_WIDTH)
            z = _dot(h_scr[...], w_ref[:, s * W_C + p * MXU_WIDTH:s * W_C + (p + 1) * MXU_WIDTH])
            scr[slot, :, cols] = post(z, cols)
            next(stages, None)
            next(stages, None)
        for _ in stages:
            pass

    @pl.when(jnp.logical_and(i > 0, i % tiles_per_batch == 0))
    def _():
        for h in range(H_C):
            s_ref[h] = st_scr[h].T


def _inproj_hgrn(x, norm_g, scale, shift, w, lb, gn_g, batch, seq):
    tm, chunk = ROW_TILE, HGRN_CHUNK
    m = x.shape[0]
    nt = m // tm
    tiles_per_batch = seq // tm
    tri, masks, levels = _hgrn_tables(chunk)
    cur = lambda i: jnp.minimum(i, nt - 1)
    done = lambda i: jnp.maximum(i - 1, 0)
    mod_spec = pl.BlockSpec((None, 1, D_MODEL), lambda i: (cur(i) // tiles_per_batch, 0, 0))
    const = lambda shape: pl.BlockSpec(shape, lambda i: (0,) * len(shape))
    o, s = pl.pallas_call(
        functools.partial(_inproj_hgrn_kernel, chunk=chunk, levels=levels, tiles_per_batch=tiles_per_batch),
        grid=(nt + 1,),
        in_specs=[
            pl.BlockSpec((tm, D_MODEL), lambda i: (cur(i), 0)),
            const((1, D_MODEL)),
            mod_spec, mod_spec,
            _weight_spec(w),
            const((1, W_C)),
            const(tri.shape), const(masks.shape), const((1, DV_C)),
        ],
        out_specs=[pl.BlockSpec((tm, W_C), lambda i: (done(i), 0)),
                   pl.BlockSpec((None, H_C, DK_C, DV_C), lambda i: (done(i) // tiles_per_batch, 0, 0, 0))],
        out_shape=[jax.ShapeDtypeStruct((m, W_C), BF16),
                   jax.ShapeDtypeStruct((batch, H_C, DK_C, DV_C), F32)],
        scratch_shapes=[pltpu.VMEM((tm, D_MODEL), BF16),
                        pltpu.VMEM((2, tm, W_C), F32),
                        pltpu.VMEM((2, tm, W_C), F32),
                        pltpu.VMEM((2, tm, W_C), F32),
                        pltpu.VMEM((2, tm, W_C), BF16),
                        pltpu.VMEM((H_C, DV_C, DK_C), F32)],
        compiler_params=_cparams("arbitrary"),
        name="inproj_hgrn_recurrence",
    )(x, norm_g.reshape(1, D_MODEL), scale, shift, w[0], lb.reshape(1, W_C), tri, masks, gn_g.reshape(1, DV_C))
    return o, s


_DEC_ROWS = 2 * H_B


def _decode_head_rows(pieces):
    row = lax.broadcasted_iota(jnp.int32, (_DEC_ROWS, LANES), 0)
    out = jnp.zeros((_DEC_ROWS, LANES), F32)
    for h in range(H_B):
        out = jnp.where(row // 2 == h, jnp.broadcast_to(pieces[h], (_DEC_ROWS, LANES)), out)
    return out


def _decode_scores(q, kn, k_pages):
    row = lax.broadcasted_iota(jnp.int32, (_DEC_ROWS, LANES), 0)
    lane = lax.broadcasted_iota(jnp.int32, (_DEC_ROWS, LANES), 1)
    qm32 = jnp.where(lane // HD_B == row % 2,
                     _decode_head_rows([q[:, h * LANES:(h + 1) * LANES] for h in range(H_B)]), 0.0)
    qm = qm32.astype(BF16)
    s = jnp.concatenate([_dot_nt(qm, kp[...].astype(BF16)) for kp in k_pages], axis=1)
    key_head = lax.broadcasted_iota(jnp.int32, s.shape, 1) % H_B
    s = jnp.where(key_head == lax.broadcasted_iota(jnp.int32, s.shape, 0) // 2, s, NEG_INF)
    s_new = jnp.sum(qm32 * _decode_head_rows([kn[h:h + 1, :] for h in range(H_B)]), axis=-1, keepdims=True)
    return s, s_new


def _decode_finish(s, s_new, lam, vn, v_pages, subln_row, gb, lam_init):
    page_rows = H_B * PAGE_SIZE
    m = jnp.maximum(jnp.max(s, axis=-1, keepdims=True), s_new)
    p = jnp.exp2(s - m)
    p_new = jnp.exp2(s_new - m)
    l = jnp.sum(p, axis=-1, keepdims=True) + p_new
    comp = lax.broadcasted_iota(jnp.int32, (_DEC_ROWS, 1), 0) % 2
    wgt = jnp.where(comp == 0, 1.0, -lam) / l
    pw = (p * wgt).astype(BF16)
    o8 = (p_new * wgt) * _decode_head_rows([vn[h:h + 1, :] for h in range(H_B)])
    for j, vp in enumerate(v_pages):
        o8 = o8 + _dot(pw[:, j * page_rows:(j + 1) * page_rows], vp[...].astype(BF16))
    out = []
    for h in range(H_B):
        o = o8[2 * h:2 * h + 1, :] + o8[2 * h + 1:2 * h + 2, :]
        out.append(_rms(o, subln_row) * (1.0 - lam_init) * gb[:, h * DV_B:(h + 1) * DV_B])
    return jnp.concatenate(out, axis=1)


def _decode_conv_kernel(st_ref, u_ref, ga_ref, w_ref, cb_ref, lg_ref, lb_ref, *rest):
    y_ref, ns_ref = rest[-2:]
    u = u_ref[...]
    y = u * w_ref[CONV_W - 1:CONV_W, :] + cb_ref[...]
    for j in range(CONV_W - 1):
        y = y + st_ref[j] * w_ref[j:j + 1, :]
        if j > 0:
            ns_ref[j - 1] = st_ref[j]
    ns_ref[CONV_W - 2] = u
    y_ref[...] = _conv_post(y, lg_ref[...], lb_ref[...], ga_ref[...].astype(F32)).astype(BF16)


def _decode_conv(state_t, layer, u, ga, conv_w, conv_b, ln_g, ln_b, new_prev):
    n = u.shape[0]
    tile = 32
    vec = pl.BlockSpec((1, W_A), lambda i: (0, 0))
    sspec = pl.BlockSpec((None, CONV_W - 1, tile, W_A), lambda i: (layer, 0, i, 0))
    rspec = pl.BlockSpec((tile, W_A), lambda i: (i, 0))
    in_specs = [sspec, rspec, rspec, pl.BlockSpec((CONV_W, W_A), lambda i: (0, 0)), vec, vec, vec]
    args = [state_t, u, ga, conv_w, conv_b.reshape(1, W_A), ln_g.reshape(1, W_A), ln_b.reshape(1, W_A)]
    aliases = {}
    if new_prev is not None:
        aliases = {len(args): 1}
        in_specs.append(pl.BlockSpec(memory_space=pl.ANY))
        args.append(new_prev)
    return pl.pallas_call(
        _decode_conv_kernel,
        grid=(n // tile,),
        in_specs=in_specs,
        out_specs=[rspec, sspec],
        out_shape=[jax.ShapeDtypeStruct((n, W_A), BF16),
                   jax.ShapeDtypeStruct((N_AC, CONV_W - 1, n, W_A), F32)],
        input_output_aliases=aliases,
        compiler_params=_cparams("arbitrary"),
        name="decode_conv",
    )(*args)


def _decode_spread_table():
    rows = np.arange(3 * 3 * H_C)
    cols = np.arange(3 * W_C)
    same_vec = (rows[:, None] // (3 * H_C)) == (cols[None, :] // W_C)
    same_head = (rows[:, None] % H_C) == ((cols[None, :] % W_C) // DV_C)
    return jnp.asarray(same_vec & same_head, BF16)


def _decode_hgrn_kernel(q_ref, lg_ref, i_ref, sg_ref, gn_ref, e_ref, s_ref, *rest, n_seq):
    o_ref, ns_ref = rest[-2:]

    def split_rows(x):
        return [p.astype(F32) for p in _split3(x)]

    def body(n, carry):
        f8 = jnp.exp2(lg_ref[n])
        parts = jnp.concatenate(split_rows(f8) + split_rows(1.0 - f8) + split_rows(q_ref[n]), axis=0)
        spread = _dot(parts.T.astype(BF16), e_ref[...])
        v8 = i_ref[n]
        rows = []
        for h in range(H_C):
            f_b, k_b, q_b =(spread[:, j * W_C + h * DV_C:j * W_C + (h + 1) * DV_C] for j in range(3))
            s_new = f_b * s_ref[n, h] + k_b * v8[h:h + 1, :]
            ns_ref[n, h] = s_new
            rows.append(jnp.sum(q_b * s_new, axis=0, keepdims=True))
        o = jnp.concatenate(rows, axis=0)
        o_ref[n] = (_rms(o, gn_ref[...]) * sg_ref[n].astype(F32)).astype(BF16)
        return carry

    lax.fori_loop(0, n_seq, body, 0)


def _decode_hgrn(qs, lg, iv, sg, gn_g, state_all, layer, new_prev):
    n = qs.shape[0]
    tile = SAMPLE_STATE_TILE
    hspec = pl.BlockSpec((tile, H_C, DV_C), lambda i: (i, 0, 0))
    sspec = pl.BlockSpec((None, tile, H_C, DK_C, DV_C), lambda i: (layer, i, 0, 0, 0))
    h3 = lambda a: a.reshape(n, H_C, DV_C)
    table = _decode_spread_table()
    in_specs = [hspec, hspec, hspec, hspec, pl.BlockSpec((1, DV_C), lambda i: (0, 0)),
                pl.BlockSpec(table.shape, lambda i: (0, 0)), sspec]
    args = [h3(qs), h3(lg), h3(iv), h3(sg), gn_g.reshape(1, DV_C), table, state_all]
    aliases = {}
    if new_prev is not None:
        aliases = {len(args): 1}
        in_specs.append(pl.BlockSpec(memory_space=pl.ANY))
        args.append(new_prev)
    o, ns = pl.pallas_call(
        functools.partial(_decode_hgrn_kernel, n_seq=tile),
        grid=(n // tile,),
        in_specs=in_specs,
        out_specs=[hspec, sspec],
        out_shape=[jax.ShapeDtypeStruct((n, H_C, DV_C), BF16),
                   jax.ShapeDtypeStruct((N_C, n, H_C, DK_C, DV_C), F32)],
        input_output_aliases=aliases,
        compiler_params=_cparams("arbitrary"),
        name="decode_hgrn",
    )(*args)
    return o.reshape(n, W_C), ns


def _rope_tables(pos, rows):
    half = ROT_DIM // 2
    inv_freq = ROPE_THETA ** (-jnp.arange(half, dtype=F32) / half)
    ang = pos.astype(F32)[:, None] * inv_freq[None, :]
    cos, sin = jnp.cos(ang), jnp.sin(ang)
    n = pos.shape[0]
    ones = jnp.ones((n, HD_B - ROT_DIM), F32)
    zeros = jnp.zeros((n, HD_B - ROT_DIM), F32)
    z8 = jnp.zeros((n, half), F32)
    cos_t = jnp.concatenate([cos, cos, ones], axis=1)
    sa_t = jnp.concatenate([-sin, z8, zeros], axis=1)
    sb_t = jnp.concatenate([z8, sin, zeros], axis=1)
    out = []
    for t in (cos_t, sa_t, sb_t):
        t = jnp.concatenate([t, t], axis=1)
        out.append(jnp.broadcast_to(t, (rows, LANES)) if n == 1 else t)
    return out


def kernel(x_prompt, x_sample, c_prompt, c_sample, cache_k, cache_v, page_table, state_conv, state_hgrn,
           norm_g, ada_w, ada_b, w_in_ac, w_out_ac, conv_w, conv_b, ln_g, ln_b, lam_q1, lam_k1, lam_q2,
           lam_k2, subln_g, w_in_c, w_out_c, gn_g, lb_logits, final_g):
    bp, tp = x_prompt.shape[:2]
    bs, ts = x_sample.shape[:2]
    assert ts == 1
    n_pool = cache_k.shape[1]
    n_past = page_table.shape[1] * PAGE_SIZE
    mp = bp * tp

    lb_sm = jax.nn.softmax(lb_logits.astype(F32), axis=0)
    lb_all = jnp.cumsum(lb_sm, axis=0) - lb_sm[0]

    mod = _ada_modulation(jnp.concatenate([c_prompt, c_sample], axis=0), ada_w, ada_b)

    def mods(l):
        out = []
        for j in range(3):
            m = mod[l, :, j * D_MODEL:(j + 1) * D_MODEL]
            out.append((m[:bp].reshape(bp, 1, D_MODEL), m[bp:].reshape(1, bs, D_MODEL)))
        return out

    tabs_p = _rope_tables(jnp.arange(tp), tp)
    tabs_s = _rope_tables(jnp.full((1,), n_past), bs)
    ck4 = cache_k.reshape(N_AC, n_pool, PAGE_SIZE * H_B, 2 * HD_B)
    cv4 = cache_v.reshape(N_AC, n_pool, PAGE_SIZE * H_B, DV_B)

    w_in_ac_b, w_out_ac_b, w_in_c_b, w_out_c_b = (w.astype(BF16) for w in (w_in_ac, w_out_ac, w_in_c, w_out_c))

    tiles_p = tp // ROW_TILE
    hp = x_prompt.reshape(mp, D_MODEL)
    hs = x_sample.reshape(bs, D_MODEL)
    state_conv_t = jnp.swapaxes(state_conv, 1, 2)
    cp_l, sp_l = [], []
    kv_p = kv_s = ss = cs = None
    for l in range(DEPTH):
        (sh_p, sh_s), (sc_p, sc_s), (gt_p, gt_s) = mods(l)
        last = final_g if l == DEPTH - 1 else None
        if l % 2 == 0:
            a = l // 2
            lam_init = 0.8 - 0.6 * math.exp(-0.3 * l)
            lam = (jnp.exp(jnp.sum(lam_q1[a].astype(F32) * lam_k1[a].astype(F32)))
                   - jnp.exp(jnp.sum(lam_q2[a].astype(F32) * lam_k2[a].astype(F32))) + lam_init)
            lam = jnp.full((1, LANES), lam, F32)
            w_in, w_out = (w_in_ac_b, a), (w_out_ac_b, a)
            q, k, kb, v, vb, gb, u, ga = _inproj_ac(hp, norm_g[l], sc_p, sh_p, w_in, *tabs_p,
                                                    ROW_TILE, tiles_p, a, kv_p)
            kv_p = (k, v)
            qs, k, _, v, _, gbs, us, gas = _inproj_ac(hs, norm_g[l], sc_s, sh_s, w_in, *tabs_s, bs, 1, a, kv_s)
            kv_s = (k, v)
            ob, obs = _attention(q, kb, vb, gb, bp, tp, qs, k, v, gbs, ck4, cv4, a, page_table,
                                 lam, subln_g[a], lam_init)
            y, cst = _prompt_conv(u, ga, conv_w[a], conv_b[a], ln_g[a], ln_b[a], bp, tp)
            hp = _outproj([ob, y], w_out, hp, gt_p, OUT_ROW_TILE, tp // OUT_ROW_TILE, last)
            cp_l.append(cst)
            y, cs = _decode_conv(state_conv_t, a, us, gas, conv_w[a], conv_b[a], ln_g[a], ln_b[a], cs)
            hs = _outproj([obs, y], w_out, hs, gt_s, bs, 1, last)
        else:
            ci = l // 2
            w_in, w_out = (w_in_c_b, ci), (w_out_c_b, ci)
            o, st = _inproj_hgrn(hp, norm_g[l], sc_p, sh_p, w_in, lb_all[l], gn_g[ci], bp, tp)
            hp = _outproj([o], w_out, hp, gt_p, OUT_ROW_TILE, tp // OUT_ROW_TILE, last)
            sp_l.append(st)
            qs, lg, iv, sg = _inproj_c(hs, norm_g[l], sc_s, sh_s, w_in, lb_all[l], bs, 1)
            o, ss = _decode_hgrn(qs, lg, iv, sg, gn_g[ci], state_hgrn, ci, ss)
            hs = _outproj([o], w_out, hs, gt_s, bs, 1, last)

    y_prompt = hp.reshape(bp, tp, D_MODEL)
    y_sample = hs.reshape(bs, ts, D_MODEL)
    return (y_prompt, y_sample,
            kv_p[0].reshape(N_AC, bp, tp, H_B, 2 * HD_B), kv_p[1].reshape(N_AC, bp, tp, H_B, DV_B),
            jnp.stack(cp_l), jnp.stack(sp_l),
            kv_s[0].reshape(N_AC, bs, ts, H_B, 2 * HD_B), kv_s[1].reshape(N_AC, bs, ts, H_B, DV_B),
            jnp.swapaxes(cs, 1, 2), ss)
```

```python
import functools
import math

import numpy as np
import jax
import jax.numpy as jnp
from jax import lax
from jax.experimental import pallas as pl
from jax.experimental.pallas import tpu as pltpu

F32 = jnp.float32
BF16 = jnp.bfloat16

D_MODEL = 1024
DEPTH = 4
PAGE_SIZE = 128
N_AC = (DEPTH + 1) // 2
N_C = DEPTH // 2

H_B = 4
HD_B = 64
DV_B = 2 * HD_B
W_B = H_B * DV_B
QK_W = H_B * 2 * HD_B
ROT_DIM = HD_B // 4
ROPE_THETA = 500000.0
NEG_INF = -1e30

W_A = D_MODEL // 2
CONV_W = 31

H_C = 8
DK_C = D_MODEL // H_C
DV_C = D_MODEL // H_C
W_C = H_C * DV_C

EPS = 1e-6
LOG2_E = math.log2(math.e)
Q_SCALE = HD_B ** -0.5 * LOG2_E

AC_IN = 2 * QK_W + 2 * W_B + 3 * W_A
C_IN = 2 * H_C * DK_C + 2 * W_C

LANES = 128
SUBLANES = 8
MXU_WIDTH = 256
VMEM_LIMIT_BYTES = 56 * 1024 * 1024

ROW_TILE = 512
OUT_ROW_TILE = 2048
ATTN_TILE = 256
CONV_TILE = 512
CONV_HALO = 32
HGRN_CHUNK = 128
SAMPLE_STATE_TILE = 8


def _cparams(*sem):
    return pltpu.CompilerParams(dimension_semantics=sem, vmem_limit_bytes=VMEM_LIMIT_BYTES)


def _silu(x):
    return x * jax.nn.sigmoid(x)


def _dot(a, b):
    return jnp.dot(a, b, preferred_element_type=F32)


def _dot_nt(a, b):
    return lax.dot_general(a, b, (((1,), (1,)), ((), ())), preferred_element_type=F32)


def _dot_tn(a, b):
    return lax.dot_general(a, b, (((0,), (0,)), ((), ())), preferred_element_type=F32)


def _rms(x, g):
    return x * lax.rsqrt(jnp.mean(x * x, axis=-1, keepdims=True) + EPS) * g


def _ada_kernel(c_ref, w_ref, b_ref, o_ref):
    s = _silu(c_ref[...]).astype(BF16)
    o_ref[...] = _dot(s, w_ref[...].astype(BF16)) + b_ref[...]


def _ada_modulation(c_all, ada_w, ada_b):
    rows = c_all.shape[0]
    return pl.pallas_call(
        _ada_kernel,
        grid=(DEPTH, 3),
        in_specs=[
            pl.BlockSpec((rows, D_MODEL), lambda l, j: (0, 0)),
            pl.BlockSpec((None, D_MODEL, D_MODEL), lambda l, j: (l, 0, j)),
            pl.BlockSpec((None, 1, D_MODEL), lambda l, j: (l, 0, j)),
        ],
        out_specs=pl.BlockSpec((None, rows, D_MODEL), lambda l, j: (l, 0, j)),
        out_shape=jax.ShapeDtypeStruct((DEPTH, rows, 3 * D_MODEL), F32),
        compiler_params=_cparams("arbitrary", "arbitrary"),
        name="ada_modulation",
    )(c_all, ada_w, ada_b.reshape(DEPTH, 1, 3 * D_MODEL))


def _weight_spec(w):
    stacked, layer = w
    return pl.BlockSpec((None,) + stacked.shape[1:], lambda i: (layer, 0, 0))


def _modulated(x_ref, ng_ref, sc_ref, sh_ref):
    h = _rms(x_ref[...], ng_ref[...])
    return (h * (1.0 + sc_ref[...]) + sh_ref[...]).astype(BF16)


def _inproj_ac_kernel(x_ref, ng_ref, sc_ref, sh_ref, w_ref, cos_ref, sa_ref, sb_ref, *rest):
    q_ref, k_ref, kb_ref, v_ref, vb_ref, gb_ref, u_ref, ga_ref, h_scr = rest[-9:]
    h_scr[...] = _modulated(x_ref, ng_ref, sc_ref, sh_ref)

    def proj(s):
        return _dot(h_scr[...], w_ref[:, s * QK_W:(s + 1) * QK_W])

    cos, sa, sb = cos_ref[...], sa_ref[...], sb_ref[...]

    def rope(z):
        return (z * cos + pltpu.roll(z, LANES - ROT_DIM // 2, 1) * sa
                + pltpu.roll(z, ROT_DIM // 2, 1) * sb)

    zq = proj(0)
    for g in range(H_B):
        sl = slice(g * LANES, (g + 1) * LANES)
        q_ref[:, sl] = (rope(zq[:, sl]) * Q_SCALE).astype(BF16)
    tm = x_ref.shape[0]
    zk = proj(1)
    for g in range(H_B):
        sl = slice(g * LANES, (g + 1) * LANES)
        r = rope(zk[:, sl])
        k_ref[pl.ds(g, tm, stride=H_B), :] = r
        kb_ref[:, sl] = r.astype(BF16)
    zv = proj(2)
    for g in range(H_B):
        v_ref[pl.ds(g, tm, stride=H_B), :] = zv[:, g * LANES:(g + 1) * LANES]
    vb_ref[...] = zv.astype(BF16)
    gb_ref[...] = _silu(proj(3)).astype(BF16)
    a_val = proj(4)
    u_ref[...] = a_val * jax.nn.sigmoid(proj(5))
    ga_ref[...] = _silu(proj(6)).astype(BF16)


def _mod_specs(rows_per_mod, tm, tiles_per_mod):
    r = rows_per_mod
    return pl.BlockSpec((None, r, D_MODEL), lambda i: (i // tiles_per_mod, 0, 0))


def _inproj_ac(x, norm_g, scale, shift, w, cos_t, sa_t, sb_t, tm, tiles_per_mod, layer, kv_prev):
    m = x.shape[0]
    tab_tiles = cos_t.shape[0] // tm
    mod_spec = _mod_specs(scale.shape[1], tm, tiles_per_mod)
    row512 = pl.BlockSpec((tm, QK_W), lambda i: (i, 0))
    heads = pl.BlockSpec((None, H_B * tm, LANES), lambda i: (layer, i, 0))
    tab_spec = pl.BlockSpec((tm, LANES), lambda i: (i % tab_tiles, 0))
    out_shape = [
        jax.ShapeDtypeStruct((m, QK_W), BF16),
        jax.ShapeDtypeStruct((N_AC, H_B * m, LANES), F32),
        jax.ShapeDtypeStruct((m, QK_W), BF16),
        jax.ShapeDtypeStruct((N_AC, H_B * m, LANES), F32),
        jax.ShapeDtypeStruct((m, W_B), BF16),
        jax.ShapeDtypeStruct((m, W_B), BF16),
        jax.ShapeDtypeStruct((m, W_A), F32),
        jax.ShapeDtypeStruct((m, W_A), BF16),
    ]
    in_specs = [
        pl.BlockSpec((tm, D_MODEL), lambda i: (i, 0)),
        pl.BlockSpec((1, D_MODEL), lambda i: (0, 0)),
        mod_spec, mod_spec,
        _weight_spec(w),
        tab_spec, tab_spec, tab_spec,
    ]
    args = [x, norm_g.reshape(1, D_MODEL), scale, shift, w[0], cos_t, sa_t, sb_t]
    aliases = {}
    if kv_prev is not None:
        aliases = {len(args): 1, len(args) + 1: 3}
        in_specs += [pl.BlockSpec(memory_space=pl.ANY)] * 2
        args += list(kv_prev)
    return pl.pallas_call(
        _inproj_ac_kernel,
        grid=(m // tm,),
        in_specs=in_specs,
        out_specs=[row512, heads, row512, heads, row512, row512, row512, row512],
        out_shape=out_shape,
        input_output_aliases=aliases,
        scratch_shapes=[pltpu.VMEM((tm, D_MODEL), BF16)],
        compiler_params=_cparams("arbitrary"),
        name="inproj_conv_attn",
    )(*args)


def _inproj_c_kernel(x_ref, ng_ref, sc_ref, sh_ref, w_ref, lb_ref, q_ref, lg_ref, i_ref, sg_ref, h_scr):
    h_scr[...] = _modulated(x_ref, ng_ref, sc_ref, sh_ref)

    def proj(s):
        return _dot(h_scr[...], w_ref[:, s * W_C:(s + 1) * W_C])

    q_ref[...] = _silu(proj(0))
    lb = lb_ref[...]
    lg_ref[...] = jnp.log(lb + (1.0 - lb) * jax.nn.sigmoid(proj(1))) * LOG2_E
    i_ref[...] = proj(2)
    sg_ref[...] = _silu(proj(3)).astype(BF16)


def _inproj_c(x, norm_g, scale, shift, w, lb, tm, tiles_per_mod):
    m = x.shape[0]
    mod_spec = _mod_specs(scale.shape[1], tm, tiles_per_mod)
    row = pl.BlockSpec((tm, W_C), lambda i: (i, 0))
    out_shape = [
        jax.ShapeDtypeStruct((m, W_C), F32),
        jax.ShapeDtypeStruct((m, W_C), F32),
        jax.ShapeDtypeStruct((m, W_C), F32),
        jax.ShapeDtypeStruct((m, W_C), BF16),
    ]
    return pl.pallas_call(
        _inproj_c_kernel,
        grid=(m // tm,),
        in_specs=[
            pl.BlockSpec((tm, D_MODEL), lambda i: (i, 0)),
            pl.BlockSpec((1, D_MODEL), lambda i: (0, 0)),
            mod_spec, mod_spec,
            _weight_spec(w),
            pl.BlockSpec((1, W_C), lambda i: (0, 0)),
        ],
        out_specs=[row] * 4,
        out_shape=out_shape,
        scratch_shapes=[pltpu.VMEM((tm, D_MODEL), BF16)],
        compiler_params=_cparams("arbitrary"),
        name="inproj_hgrn",
    )(x, norm_g.reshape(1, D_MODEL), scale, shift, w[0], lb.reshape(1, W_C))


def _outproj_kernel(*refs, n_act, final_norm):
    acts = refs[:n_act]
    w_ref, x_ref, gate_ref = refs[n_act:n_act + 3]
    o_ref = refs[-1]
    acc = None
    lo = 0
    for a_ref in acts:
        width = a_ref.shape[-1]
        part = _dot(a_ref[...], w_ref[lo:lo + width, :])
        acc = part if acc is None else acc + part
        lo += width
    y = x_ref[...] + gate_ref[...] * acc
    if final_norm:
        y = _rms(y, refs[n_act + 3][...])
    o_ref[...] = y


def _outproj(acts, w, x, gate, tm, tiles_per_mod, final_g=None):
    m = x.shape[0]
    mod_spec = _mod_specs(gate.shape[1], tm, tiles_per_mod)
    in_specs = [pl.BlockSpec((tm, a.shape[1]), lambda i: (i, 0)) for a in acts]
    in_specs += [
        _weight_spec(w),
        pl.BlockSpec((tm, D_MODEL), lambda i: (i, 0)),
        mod_spec,
    ]
    args = list(acts) + [w[0], x, gate]
    if final_g is not None:
        in_specs.append(pl.BlockSpec((1, D_MODEL), lambda i: (0, 0)))
        args.append(final_g.reshape(1, D_MODEL))
    return pl.pallas_call(
        functools.partial(_outproj_kernel, n_act=len(acts), final_norm=final_g is not None),
        grid=(m // tm,),
        in_specs=in_specs,
        out_specs=pl.BlockSpec((tm, D_MODEL), lambda i: (i, 0)),
        out_shape=jax.ShapeDtypeStruct((m, D_MODEL), F32),
        compiler_params=_cparams("arbitrary"),
        name="outproj",
    )(*args)


def _attn_kernel(pt_ref, lam_ref, sgc_ref, sgr_ref, q_ref, k_ref, v_ref, gb_ref,
                 qs_ref, kn_ref, vn_ref, gbs_ref, *rest, tile, lam_init, n_dec, n_pages):
    pages = rest[:2 * n_dec * n_pages]
    o_ref, os_ref, q2_scr, s_a, s_b, m_scr, l_scr, acc_scr = rest[2 * n_dec * n_pages:]
    sg_ref = sgc_ref
    qi = pl.program_id(1)
    heads = [slice(h * LANES, (h + 1) * LANES) for h in range(H_B)]
    lane = lax.broadcasted_iota(jnp.int32, (tile, LANES), 1)
    for h, sl in enumerate(heads):
        q = q_ref[:, sl]
        zero = jnp.zeros_like(q)
        q2_scr[h] = jnp.concatenate([jnp.where(lane < HD_B, q, zero), jnp.where(lane >= HD_B, q, zero)], axis=0)
    m_scr[...] = jnp.full(m_scr.shape, NEG_INF, F32)
    l_scr[...] = jnp.zeros(l_scr.shape, F32)
    acc_scr[...] = jnp.zeros(acc_scr.shape, F32)

    def scores(t, s_scr):
        start = pl.multiple_of(t * tile, tile)
        for h, sl in enumerate(heads):
            s_scr[h] = _dot_nt(k_ref[pl.ds(start, tile), sl], q2_scr[h])

    def softmax_part(s_scr, mask):
        parts = []
        for h in range(H_B):
            sh = s_scr[h]
            if mask is not None:
                sh = jnp.where(mask, sh, NEG_INF)
            m_old = m_scr[h]
            m_new = jnp.maximum(m_old, jnp.max(sh, axis=0, keepdims=True))
            alpha = jnp.exp2(m_old - m_new)
            p = jnp.exp2(sh - m_new)
            m_scr[h] = m_new
            l_scr[h] = alpha * l_scr[h] + jnp.sum(p, axis=0, keepdims=True)
            parts.append((alpha, p.astype(BF16)))
        return parts

    def value_part(t, parts):
        start = pl.multiple_of(t * tile, tile)
        for h, sl in enumerate(heads):
            alpha, p = parts[h]
            acc_scr[h] = alpha * acc_scr[h] + _dot_tn(v_ref[pl.ds(start, tile), sl], p)

    def consume(s_scr, t, mask):
        value_part(t, softmax_part(s_scr, mask))

    odd = qi % 2

    @pl.when(odd == 1)
    def _():
        scores(0, s_b)
        scores(1, s_a)
        consume(s_b, 0, None)

    @pl.when(odd == 0)
    def _():
        scores(0, s_a)

    def body(jj, carry):
        t = odd + 2 * jj
        scores(t + 1, s_b)
        consume(s_a, t, None)
        scores(t + 2, s_a)
        consume(s_b, t + 1, None)
        return carry

    lax.fori_loop(0, qi // 2, body, 0)
    key = lax.broadcasted_iota(jnp.int32, (tile, 2 * tile), 0)
    qry = lax.broadcasted_iota(jnp.int32, (tile, 2 * tile), 1)
    diag = softmax_part(s_a, key <= jnp.where(qry >= tile, qry - tile, qry))
    lam = lam_ref[:, 0:1]
    dec = []
    for r in range(n_dec):
        k_pages = pages[2 * r * n_pages:(2 * r + 1) * n_pages]
        v_pages = pages[(2 * r + 1) * n_pages:(2 * r + 2) * n_pages]
        dec.append(_decode_scores(qs_ref[r].astype(F32), kn_ref[r], k_pages) + (v_pages,))
    value_part(qi, diag)
    for r, (s, s_new, v_pages) in enumerate(dec):
        os_ref[r] = _decode_finish(s, s_new, lam, vn_ref[r], v_pages, sgr_ref[...],
                                   gbs_ref[r].astype(F32), lam_init).astype(BF16)
    for h, sl in enumerate(heads):
        l, acc = l_scr[h], acc_scr[h]
        o = acc[:, :tile] / l[:, :tile] - lam_ref[:, 0:1] * (acc[:, tile:] / l[:, tile:])
        y = o * lax.rsqrt(jnp.mean(o * o, axis=0, keepdims=True) + EPS) * sg_ref[...]
        o_ref[:, sl] = (y.T * (1.0 - lam_init) * gb_ref[:, sl].astype(F32)).astype(BF16)


def _attention(q, kb, vb, gb, batch, seq, qs, kn_all, vn_all, gbs, cache_k4, cache_v4, layer, page_table,
               lam, subln_g, lam_init):
    tile = ATTN_TILE
    nq = seq // tile
    n_seq, n_pages = page_table.shape
    n_dec = n_seq // (batch * nq)
    assert n_dec * batch * nq == n_seq
    q3, k3, v3, g3 = (a.reshape(batch, seq, QK_W) for a in (q, kb, vb, gb))
    qspec = pl.BlockSpec((None, tile, QK_W), lambda b, i, pt: (b, i, 0))
    kvspec = pl.BlockSpec((None, seq, QK_W), lambda b, i, pt: (b, 0, 0))
    rows = pl.BlockSpec((n_dec, 1, QK_W), lambda b, i, pt: (b * nq + i, 0, 0))
    new_kv = pl.BlockSpec((None, n_dec, H_B, LANES), lambda b, i, pt: (layer, b * nq + i, 0, 0))

    def page_spec(r, j):
        return pl.BlockSpec((None, None, H_B * PAGE_SIZE, LANES),
                            lambda b, i, pt: (layer, pt[(b * nq + i) * n_dec + r, j], 0, 0))

    page_specs, page_args = [], []
    for r in range(n_dec):
        for cache in (cache_k4, cache_v4):
            page_specs += [page_spec(r, j) for j in range(n_pages)]
            page_args += [cache] * n_pages
    grid_spec = pltpu.PrefetchScalarGridSpec(
        num_scalar_prefetch=1,
        grid=(batch, nq),
        in_specs=[pl.BlockSpec((1, LANES), lambda b, i, pt: (0, 0)),
                  pl.BlockSpec((DV_B, 1), lambda b, i, pt: (0, 0)),
                  pl.BlockSpec((1, DV_B), lambda b, i, pt: (0, 0)),
                  qspec, kvspec, kvspec, qspec, rows, new_kv, new_kv, rows] + page_specs,
        out_specs=[qspec, rows],
        scratch_shapes=[pltpu.VMEM((H_B, 2 * tile, LANES), BF16),
                        pltpu.VMEM((H_B, tile, 2 * tile), F32),
                        pltpu.VMEM((H_B, tile, 2 * tile), F32),
                        pltpu.VMEM((H_B, 1, 2 * tile), F32),
                        pltpu.VMEM((H_B, 1, 2 * tile), F32),
                        pltpu.VMEM((H_B, DV_B, 2 * tile), F32)],
    )
    r3 = lambda a: a.reshape(n_seq, 1, QK_W)
    h4 = lambda a: a.reshape(N_AC, n_seq, H_B, LANES)
    out, out_s = pl.pallas_call(
        functools.partial(_attn_kernel, tile=tile, lam_init=lam_init, n_dec=n_dec, n_pages=n_pages),
        grid_spec=grid_spec,
        out_shape=[jax.ShapeDtypeStruct((batch, seq, W_B), BF16),
                   jax.ShapeDtypeStruct((n_seq, 1, W_B), BF16)],
        compiler_params=_cparams("arbitrary", "arbitrary"),
        name="diff_attention",
    )(page_table, lam, subln_g.reshape(DV_B, 1), subln_g.reshape(1, DV_B), q3, k3, v3, g3,
      r3(qs), h4(kn_all), h4(vn_all), r3(gbs), *page_args)
    return out.reshape(batch * seq, W_B), out_s.reshape(n_seq, W_B)


def _conv_post(y, ln_g, ln_b, ga):
    yc = y - jnp.mean(y, axis=-1, keepdims=True)
    yn = yc * lax.rsqrt(jnp.mean(yc * yc, axis=-1, keepdims=True) + EPS) * ln_g + ln_b
    return _silu(yn) * ga


def _conv_kernel(u_ref, ga_ref, w_ref, cb_ref, lg_ref, lb_ref, y_ref, st_ref, ext_scr, sh_scr, *, tile, rows):
    t = pl.program_id(1)
    first = CONV_HALO - (CONV_W - 1)
    sh_rows = sh_scr.shape[1]

    @pl.when(t == 0)
    def _():
        ext_scr[0:CONV_HALO, :] = jnp.zeros((CONV_HALO, W_A), F32)

    @pl.when(t > 0)
    def _():
        ext_scr[0:CONV_HALO, :] = ext_scr[tile:tile + CONV_HALO, :]

    ext_scr[CONV_HALO:CONV_HALO + tile, :] = u_ref[...]
    for r in range(1, SUBLANES):
        sh_scr[r - 1] = ext_scr[r:r + sh_rows, :]
    for r0 in range(0, tile, rows):
        acc = jnp.zeros((rows, W_A), F32) + cb_ref[...]
        for j in range(CONV_W):
            off = first + j
            r, base = off % SUBLANES, off - off % SUBLANES + r0
            src = ext_scr[base:base + rows, :] if r == 0 else sh_scr[r - 1, base:base + rows, :]
            acc = acc + src * w_ref[j:j + 1, :]
        y_ref[r0:r0 + rows, :] = _conv_post(acc, lg_ref[...], lb_ref[...],
                                            ga_ref[r0:r0 + rows, :].astype(F32)).astype(BF16)

    @pl.when(t == pl.num_programs(1) - 1)
    def _():
        st_ref[...] = ext_scr[CONV_HALO + tile - (CONV_W - 1):CONV_HALO + tile, :]


def _prompt_conv(u, ga, conv_w, conv_b, ln_g, ln_b, batch, seq):
    tile = CONV_TILE
    u3, g3 = u.reshape(batch, seq, W_A), ga.reshape(batch, seq, W_A)
    tspec = pl.BlockSpec((None, tile, W_A), lambda b, t: (b, t, 0))
    vec = pl.BlockSpec((1, W_A), lambda b, t: (0, 0))
    y, st = pl.pallas_call(
        functools.partial(_conv_kernel, tile=tile, rows=64),
        grid=(batch, seq // tile),
        in_specs=[tspec, tspec, pl.BlockSpec((CONV_W, W_A), lambda b, t: (0, 0)), vec, vec, vec],
        out_specs=[tspec, pl.BlockSpec((None, CONV_W - 1, W_A), lambda b, t: (b, 0, 0))],
        out_shape=[jax.ShapeDtypeStruct((batch, seq, W_A), BF16),
                   jax.ShapeDtypeStruct((batch, CONV_W - 1, W_A), F32)],
        scratch_shapes=[pltpu.VMEM((CONV_HALO + tile, W_A), F32),
                        pltpu.VMEM((SUBLANES - 1, CONV_HALO + tile - SUBLANES, W_A), F32)],
        compiler_params=_cparams("arbitrary", "arbitrary"),
        name="prompt_conv",
    )(u3, g3, conv_w, conv_b.reshape(1, W_A), ln_g.reshape(1, W_A), ln_b.reshape(1, W_A))
    return y.reshape(batch * seq, W_A), st


def _hgrn_tables(chunk):
    levels = int(math.log2(chunk))
    t = np.arange(chunk)[:, None]
    u = np.arange(chunk)[None, :]
    masks = []
    for v in range(levels):
        m = 1 << v
        masks.append((t // (2 * m) == u // (2 * m)) & (t % (2 * m) >= m) & (u % (2 * m) < m))
    return jnp.asarray(u <= t, BF16), jnp.asarray(np.stack(masks), BF16), levels


def _split3(x):
    hi = x.astype(BF16)
    r = x - hi.astype(F32)
    mid = r.astype(BF16)
    lo = (r - mid.astype(F32)).astype(BF16)
    return hi, mid, lo


def _hgrn_level_exponent(b, g2, m, row):
    chunk, width = b.shape
    if m == 1:
        return jnp.where((row & 1) != 0, g2, 0.0)
    if 2 * m < SUBLANES:
        b3 = b.reshape(chunk // SUBLANES, SUBLANES, width)
        sub = lax.broadcasted_iota(jnp.int32, b3.shape, 1)
        bm = b3[:, m - 1:m, :]
        for blk in range(1, SUBLANES // (2 * m)):
            bm = jnp.where(sub < blk * 2 * m, bm, b3[:, blk * 2 * m + m - 1:blk * 2 * m + m, :])
    else:
        b3 = b.reshape(chunk // (2 * m), 2 * m, width)
        bm = b3[:, m - 1:m, :]
    d = lax.bitcast_convert_type((b3 - bm).reshape(chunk, width), jnp.uint32)
    return lax.bitcast_convert_type(d | jnp.uint32(0x80000000), F32)


def _hgrn_chunk_stages(q_ref, lg_ref, i_ref, sg_ref, tri_ref, msk_ref, gn_ref, o_ref, st_scr, chunk, levels):
    heads = [slice(h * LANES, (h + 1) * LANES) for h in range(H_C)]
    g2 = lg_ref[...]
    q = q_ref[...]
    v = i_ref[...]
    k = 1.0 - jnp.exp2(g2)
    b3 = _dot(tri_ref[...], jnp.concatenate(_split3(g2), axis=1))
    yield
    b = b3[:, :W_C] + b3[:, W_C:2 * W_C] + b3[:, 2 * W_C:]
    row = lax.broadcasted_iota(jnp.int32, (chunk, W_C), 0)
    qb, kb = q.astype(BF16), k.astype(BF16)
    a = [jnp.zeros((chunk, chunk), BF16)] * H_C
    for lv in range(levels):
        m = 1 << lv
        x = jnp.where((row & m) != 0, qb, kb) * jnp.exp2(_hgrn_level_exponent(b, g2, m, row).astype(BF16))
        msk = msk_ref[lv]
        a = [a[h] + msk * _dot_nt(x[:, sl], x[:, sl]).astype(BF16) for h, sl in enumerate(heads)]
        yield
    vb = v.astype(BF16)
    qe = (q * jnp.exp2(b)).astype(BF16)
    b_last = b[chunk - 1:chunk, :]
    kd = (k * jnp.exp2(b_last - b)).astype(BF16)
    decay = jnp.exp2(b_last)
    qk = q * k
    for h, sl in enumerate(heads):
        st = st_scr[h]
        o = _dot(a[h], vb[:, sl]) + jnp.sum(qk[:, sl], axis=1, keepdims=True) * v[:, sl]
        o = o + _dot_nt(qe[:, sl], st.astype(BF16))
        st_scr[h] = st * decay[:, sl] + _dot_tn(vb[:, sl], kd[:, sl])
        o_ref[:, sl] = (_rms(o, gn_ref[...]) * sg_ref[:, sl].astype(F32)).astype(BF16)


def _inproj_hgrn_kernel(x_ref, ng_ref, sc_ref, sh_ref, w_ref, lb_ref, tri_ref, msk_ref, gn_ref,
                        o_ref, s_ref, h_scr, q_scr, lg_scr, iv_scr, sg_scr, st_scr,
                        *, chunk, levels, tiles_per_batch):
    i = pl.program_id(0)
    slot = i % 2
    prev = 1 - slot
    tm = x_ref.shape[0]
    assert tm // chunk == 4

    @pl.when(i == 0)
    def _():
        for scr in (q_scr, lg_scr, iv_scr, sg_scr):
            scr[1] = jnp.zeros(scr.shape[1:], scr.dtype)

    @pl.when((i + tiles_per_batch - 1) % tiles_per_batch == 0)
    def _():
        st_scr[...] = jnp.zeros_like(st_scr)

    h_scr[...] = _modulated(x_ref, ng_ref, sc_ref, sh_ref)
    lb = lb_ref[...]

    sections = [
        (q_scr, lambda z, cols: _silu(z)),
        (lg_scr, lambda z, cols: jnp.log(lb[:, cols] + (1.0 - lb[:, cols]) * jax.nn.sigmoid(z)) * LOG2_E),
        (iv_scr, lambda z, cols: z),
        (sg_scr, lambda z, cols: _silu(z).astype(BF16)),
    ]
    n_pieces = W_C // MXU_WIDTH
    for s, (scr, post) in enumerate(sections):
        rows = pl.ds(s * chunk, chunk)
        stages = _hgrn_chunk_stages(q_scr.at[prev, rows], lg_scr.at[prev, rows], iv_scr.at[prev, rows],
                                    sg_scr.at[prev, rows], tri_ref, msk_ref, gn_ref, o_ref.at[rows],
                                    st_scr, chunk, levels)
        for p in range(n_pieces):
            cols = slice(p * MXU_WIDTH, (p + 1) * MXU_WIDTH)
            z = _dot(h_scr[...], w_ref[:, s * W_C + p * MXU_WIDTH:s * W_C + (p + 1) * MXU_WIDTH])
            scr[slot, :, cols] = post(z, cols)
            next(stages, None)
            next(stages, None)
        for _ in stages:
            pass

    @pl.when(jnp.logical_and(i > 0, i % tiles_per_batch == 0))
    def _():
        for h in range(H_C):
            s_ref[h] = st_scr[h].T


def _inproj_hgrn(x, norm_g, scale, shift, w, lb, gn_g, batch, seq):
    tm, chunk = ROW_TILE, HGRN_CHUNK
    m = x.shape[0]
    nt = m // tm
    tiles_per_batch = seq // tm
    tri, masks, levels = _hgrn_tables(chunk)
    cur = lambda i: jnp.minimum(i, nt - 1)
    done = lambda i: jnp.maximum(i - 1, 0)
    mod_spec = pl.BlockSpec((None, 1, D_MODEL), lambda i: (cur(i) // tiles_per_batch, 0, 0))
    const = lambda shape: pl.BlockSpec(shape, lambda i: (0,) * len(shape))
    o, s = pl.pallas_call(
        functools.partial(_inproj_hgrn_kernel, chunk=chunk, levels=levels, tiles_per_batch=tiles_per_batch),
        grid=(nt + 1,),
        in_specs=[
            pl.BlockSpec((tm, D_MODEL), lambda i: (cur(i), 0)),
            const((1, D_MODEL)),
            mod_spec, mod_spec,
            _weight_spec(w),
            const((1, W_C)),
            const(tri.shape), const(masks.shape), const((1, DV_C)),
        ],
        out_specs=[pl.BlockSpec((tm, W_C), lambda i: (done(i), 0)),
                   pl.BlockSpec((None, H_C, DK_C, DV_C), lambda i: (done(i) // tiles_per_batch, 0, 0, 0))],
        out_shape=[jax.ShapeDtypeStruct((m, W_C), BF16),
                   jax.ShapeDtypeStruct((batch, H_C, DK_C, DV_C), F32)],
        scratch_shapes=[pltpu.VMEM((tm, D_MODEL), BF16),
                        pltpu.VMEM((2, tm, W_C), F32),
                        pltpu.VMEM((2, tm, W_C), F32),
                        pltpu.VMEM((2, tm, W_C), F32),
                        pltpu.VMEM((2, tm, W_C), BF16),
                        pltpu.VMEM((H_C, DV_C, DK_C), F32)],
        compiler_params=_cparams("arbitrary"),
        name="inproj_hgrn_recurrence",
    )(x, norm_g.reshape(1, D_MODEL), scale, shift, w[0], lb.reshape(1, W_C), tri, masks, gn_g.reshape(1, DV_C))
    return o, s


_DEC_ROWS = 2 * H_B


def _decode_head_rows(pieces):
    row = lax.broadcasted_iota(jnp.int32, (_DEC_ROWS, LANES), 0)
    out = jnp.zeros((_DEC_ROWS, LANES), F32)
    for h in range(H_B):
        out = jnp.where(row // 2 == h, jnp.broadcast_to(pieces[h], (_DEC_ROWS, LANES)), out)
    return out


def _decode_scores(q, kn, k_pages):
    row = lax.broadcasted_iota(jnp.int32, (_DEC_ROWS, LANES), 0)
    lane = lax.broadcasted_iota(jnp.int32, (_DEC_ROWS, LANES), 1)
    qm32 = jnp.where(lane // HD_B == row % 2,
                     _decode_head_rows([q[:, h * LANES:(h + 1) * LANES] for h in range(H_B)]), 0.0)
    qm = qm32.astype(BF16)
    s = jnp.concatenate([_dot_nt(qm, kp[...].astype(BF16)) for kp in k_pages], axis=1)
    key_head = lax.broadcasted_iota(jnp.int32, s.shape, 1) % H_B
    s = jnp.where(key_head == lax.broadcasted_iota(jnp.int32, s.shape, 0) // 2, s, NEG_INF)
    s_new = jnp.sum(qm32 * _decode_head_rows([kn[h:h + 1, :] for h in range(H_B)]), axis=-1, keepdims=True)
    return s, s_new


def _decode_finish(s, s_new, lam, vn, v_pages, subln_row, gb, lam_init):
    page_rows = H_B * PAGE_SIZE
    m = jnp.maximum(jnp.max(s, axis=-1, keepdims=True), s_new)
    p = jnp.exp2(s - m)
    p_new = jnp.exp2(s_new - m)
    l = jnp.sum(p, axis=-1, keepdims=True) + p_new
    comp = lax.broadcasted_iota(jnp.int32, (_DEC_ROWS, 1), 0) % 2
    wgt = jnp.where(comp == 0, 1.0, -lam) / l
    pw = (p * wgt).astype(BF16)
    o8 = (p_new * wgt) * _decode_head_rows([vn[h:h + 1, :] for h in range(H_B)])
    for j, vp in enumerate(v_pages):
        o8 = o8 + _dot(pw[:, j * page_rows:(j + 1) * page_rows], vp[...].astype(BF16))
    out = []
    for h in range(H_B):
        o = o8[2 * h:2 * h + 1, :] + o8[2 * h + 1:2 * h + 2, :]
        out.append(_rms(o, subln_row) * (1.0 - lam_init) * gb[:, h * DV_B:(h + 1) * DV_B])
    return jnp.concatenate(out, axis=1)


def _decode_conv_kernel(st_ref, u_ref, ga_ref, w_ref, cb_ref, lg_ref, lb_ref, *rest):
    y_ref, ns_ref = rest[-2:]
    u = u_ref[...]
    y = u * w_ref[CONV_W - 1:CONV_W, :] + cb_ref[...]
    for j in range(CONV_W - 1):
        y = y + st_ref[j] * w_ref[j:j + 1, :]
        if j > 0:
            ns_ref[j - 1] = st_ref[j]
    ns_ref[CONV_W - 2] = u
    y_ref[...] = _conv_post(y, lg_ref[...], lb_ref[...], ga_ref[...].astype(F32)).astype(BF16)


def _decode_conv(state_t, layer, u, ga, conv_w, conv_b, ln_g, ln_b, new_prev):
    n = u.shape[0]
    tile = 32
    vec = pl.BlockSpec((1, W_A), lambda i: (0, 0))
    sspec = pl.BlockSpec((None, CONV_W - 1, tile, W_A), lambda i: (layer, 0, i, 0))
    rspec = pl.BlockSpec((tile, W_A), lambda i: (i, 0))
    in_specs = [sspec, rspec, rspec, pl.BlockSpec((CONV_W, W_A), lambda i: (0, 0)), vec, vec, vec]
    args = [state_t, u, ga, conv_w, conv_b.reshape(1, W_A), ln_g.reshape(1, W_A), ln_b.reshape(1, W_A)]
    aliases = {}
    if new_prev is not None:
        aliases = {len(args): 1}
        in_specs.append(pl.BlockSpec(memory_space=pl.ANY))
        args.append(new_prev)
    return pl.pallas_call(
        _decode_conv_kernel,
        grid=(n // tile,),
        in_specs=in_specs,
        out_specs=[rspec, sspec],
        out_shape=[jax.ShapeDtypeStruct((n, W_A), BF16),
                   jax.ShapeDtypeStruct((N_AC, CONV_W - 1, n, W_A), F32)],
        input_output_aliases=aliases,
        compiler_params=_cparams("arbitrary"),
        name="decode_conv",
    )(*args)


def _decode_spread_table():
    rows = np.arange(3 * 3 * H_C)
    cols = np.arange(3 * W_C)
    same_vec = (rows[:, None] // (3 * H_C)) == (cols[None, :] // W_C)
    same_head = (rows[:, None] % H_C) == ((cols[None, :] % W_C) // DV_C)
    return jnp.asarray(same_vec & same_head, BF16)


def _decode_hgrn_kernel(q_ref, lg_ref, i_ref, sg_ref, gn_ref, e_ref, s_ref, *rest, n_seq):
    o_ref, ns_ref = rest[-2:]

    def split_rows(x):
        return [p.astype(F32) for p in _split3(x)]

    def body(n, carry):
        f8 = jnp.exp2(lg_ref[n])
        parts = jnp.concatenate(split_rows(f8) + split_rows(1.0 - f8) + split_rows(q_ref[n]), axis=0)
        spread = _dot(parts.T.astype(BF16), e_ref[...])
        v8 = i_ref[n]
        rows = []
        for h in range(H_C):
            f_b, k_b, q_b =(spread[:, j * W_C + h * DV_C:j * W_C + (h + 1) * DV_C] for j in range(3))
            s_new = f_b * s_ref[n, h] + k_b * v8[h:h + 1, :]
            ns_ref[n, h] = s_new
            rows.append(jnp.sum(q_b * s_new, axis=0, keepdims=True))
        o = jnp.concatenate(rows, axis=0)
        o_ref[n] = (_rms(o, gn_ref[...]) * sg_ref[n].astype(F32)).astype(BF16)
        return carry

    lax.fori_loop(0, n_seq, body, 0)


def _decode_hgrn(qs, lg, iv, sg, gn_g, state_all, layer, new_prev):
    n = qs.shape[0]
    tile = SAMPLE_STATE_TILE
    hspec = pl.BlockSpec((tile, H_C, DV_C), lambda i: (i, 0, 0))
    sspec = pl.BlockSpec((None, tile, H_C, DK_C, DV_C), lambda i: (layer, i, 0, 0, 0))
    h3 = lambda a: a.reshape(n, H_C, DV_C)
    table = _decode_spread_table()
    in_specs = [hspec, hspec, hspec, hspec, pl.BlockSpec((1, DV_C), lambda i: (0, 0)),
                pl.BlockSpec(table.shape, lambda i: (0, 0)), sspec]
    args = [h3(qs), h3(lg), h3(iv), h3(sg), gn_g.reshape(1, DV_C), table, state_all]
    aliases = {}
    if new_prev is not None:
        aliases = {len(args): 1}
        in_specs.append(pl.BlockSpec(memory_space=pl.ANY))
        args.append(new_prev)
    o, ns = pl.pallas_call(
        functools.partial(_decode_hgrn_kernel, n_seq=tile),
        grid=(n // tile,),
        in_specs=in_specs,
        out_specs=[hspec, sspec],
        out_shape=[jax.ShapeDtypeStruct((n, H_C, DV_C), BF16),
                   jax.ShapeDtypeStruct((N_C, n, H_C, DK_C, DV_C), F32)],
        input_output_aliases=aliases,
        compiler_params=_cparams("arbitrary"),
        name="decode_hgrn",
    )(*args)
    return o.reshape(n, W_C), ns


def _rope_tables(pos, rows):
    half = ROT_DIM // 2
    inv_freq = ROPE_THETA ** (-jnp.arange(half, dtype=F32) / half)
    ang = pos.astype(F32)[:, None] * inv_freq[None, :]
    cos, sin = jnp.cos(ang), jnp.sin(ang)
    n = pos.shape[0]
    ones = jnp.ones((n, HD_B - ROT_DIM), F32)
    zeros = jnp.zeros((n, HD_B - ROT_DIM), F32)
    z8 = jnp.zeros((n, half), F32)
    cos_t = jnp.concatenate([cos, cos, ones], axis=1)
    sa_t = jnp.concatenate([-sin, z8, zeros], axis=1)
    sb_t = jnp.concatenate([z8, sin, zeros], axis=1)
    out = []
    for t in (cos_t, sa_t, sb_t):
        t = jnp.concatenate([t, t], axis=1)
        out.append(jnp.broadcast_to(t, (rows, LANES)) if n == 1 else t)
    return out


def kernel(x_prompt, x_sample, c_prompt, c_sample, cache_k, cache_v, page_table, state_conv, state_hgrn,
           norm_g, ada_w, ada_b, w_in_ac, w_out_ac, conv_w, conv_b, ln_g, ln_b, lam_q1, lam_k1, lam_q2,
           lam_k2, subln_g, w_in_c, w_out_c, gn_g, lb_logits, final_g):
    bp, tp = x_prompt.shape[:2]
    bs, ts = x_sample.shape[:2]
    assert ts == 1
    n_pool = cache_k.shape[1]
    n_past = page_table.shape[1] * PAGE_SIZE
    mp = bp * tp

    lb_sm = jax.nn.softmax(lb_logits.astype(F32), axis=0)
    lb_all = jnp.cumsum(lb_sm, axis=0) - lb_sm[0]

    mod = _ada_modulation(jnp.concatenate([c_prompt, c_sample], axis=0), ada_w, ada_b)

    def mods(l):
        out = []
        for j in range(3):
            m = mod[l, :, j * D_MODEL:(j + 1) * D_MODEL]
            out.append((m[:bp].reshape(bp, 1, D_MODEL), m[bp:].reshape(1, bs, D_MODEL)))
        return out

    tabs_p = _rope_tables(jnp.arange(tp), tp)
    tabs_s = _rope_tables(jnp.full((1,), n_past), bs)
    ck4 = cache_k.reshape(N_AC, n_pool, PAGE_SIZE * H_B, 2 * HD_B)
    cv4 = cache_v.reshape(N_AC, n_pool, PAGE_SIZE * H_B, DV_B)

    w_in_ac_b, w_out_ac_b, w_in_c_b, w_out_c_b = (w.astype(BF16) for w in (w_in_ac, w_out_ac, w_in_c, w_out_c))

    tiles_p = tp // ROW_TILE
    hp = x_prompt.reshape(mp, D_MODEL)
    hs = x_sample.reshape(bs, D_MODEL)
    state_conv_t = jnp.swapaxes(state_conv, 1, 2)
    cp_l, sp_l = [], []
    kv_p = kv_s = ss = cs = None
    for l in range(DEPTH):
        (sh_p, sh_s), (sc_p, sc_s), (gt_p, gt_s) = mods(l)
        last = final_g if l == DEPTH - 1 else None
        if l % 2 == 0:
            a = l // 2
            lam_init = 0.8 - 0.6 * math.exp(-0.3 * l)
            lam = (jnp.exp(jnp.sum(lam_q1[a].astype(F32) * lam_k1[a].astype(F32)))
                   - jnp.exp(jnp.sum(lam_q2[a].astype(F32) * lam_k2[a].astype(F32))) + lam_init)
            lam = jnp.full((1, LANES), lam, F32)
            w_in, w_out = (w_in_ac_b, a), (w_out_ac_b, a)
            q, k, kb, v, vb, gb, u, ga = _inproj_ac(hp, norm_g[l], sc_p, sh_p, w_in, *tabs_p,
                                                    ROW_TILE, tiles_p, a, kv_p)
            kv_p = (k, v)
            qs, k, _, v, _, gbs, us, gas = _inproj_ac(hs, norm_g[l], sc_s, sh_s, w_in, *tabs_s, bs, 1, a, kv_s)
            kv_s = (k, v)
            ob, obs = _attention(q, kb, vb, gb, bp, tp, qs, k, v, gbs, ck4, cv4, a, page_table,
                                 lam, subln_g[a], lam_init)
            y, cst = _prompt_conv(u, ga, conv_w[a], conv_b[a], ln_g[a], ln_b[a], bp, tp)
            hp = _outproj([ob, y], w_out, hp, gt_p, OUT_ROW_TILE, tp // OUT_ROW_TILE, last)
            cp_l.append(cst)
            y, cs = _decode_conv(state_conv_t, a, us, gas, conv_w[a], conv_b[a], ln_g[a], ln_b[a], cs)
            hs = _outproj([obs, y], w_out, hs, gt_s, bs, 1, last)
        else:
            ci = l // 2
            w_in, w_out = (w_in_c_b, ci), (w_out_c_b, ci)
            o, st = _inproj_hgrn(hp, norm_g[l], sc_p, sh_p, w_in, lb_all[l], gn_g[ci], bp, tp)
            hp = _outproj([o], w_out, hp, gt_p, OUT_ROW_TILE, tp // OUT_ROW_TILE, last)
            sp_l.append(st)
            qs, lg, iv, sg = _inproj_c(hs, norm_g[l], sc_s, sh_s, w_in, lb_all[l], bs, 1)
            o, ss = _decode_hgrn(qs, lg, iv, sg, gn_g[ci], state_hgrn, ci, ss)
            hs = _outproj([o], w_out, hs, gt_s, bs, 1, last)

    y_prompt = hp.reshape(bp, tp, D_MODEL)
    y_sample = hs.reshape(bs, ts, D_MODEL)
    return (y_prompt, y_sample,
            kv_p[0].reshape(N_AC, bp, tp, H_B, 2 * HD_B), kv_p[1].reshape(N_AC, bp, tp, H_B, DV_B),
            jnp.stack(cp_l), jnp.stack(sp_l),
            kv_s[0].reshape(N_AC, bs, ts, H_B, 2 * HD_B), kv_s[1].reshape(N_AC, bs, ts, H_B, DV_B),
            jnp.swapaxes(cs, 1, 2), ss)
```

```python
import functools
import math

import numpy as np
import jax
import jax.numpy as jnp
from jax import lax
from jax.experimental import pallas as pl
from jax.experimental.pallas import tpu as pltpu

F32 = jnp.float32
BF16 = jnp.bfloat16

D_MODEL = 1024
DEPTH = 4
PAGE_SIZE = 128
N_AC = (DEPTH + 1) // 2
N_C = DEPTH // 2

H_B = 4
HD_B = 64
DV_B = 2 * HD_B
W_B = H_B * DV_B
QK_W = H_B * 2 * HD_B
ROT_DIM = HD_B // 4
ROPE_THETA = 500000.0
NEG_INF = -1e30

W_A = D_MODEL // 2
CONV_W = 31

H_C = 8
DK_C = D_MODEL // H_C
DV_C = D_MODEL // H_C
W_C = H_C * DV_C

EPS = 1e-6
LOG2_E = math.log2(math.e)
Q_SCALE = HD_B ** -0.5 * LOG2_E

AC_IN = 2 * QK_W + 2 * W_B + 3 * W_A
C_IN = 2 * H_C * DK_C + 2 * W_C

LANES = 128
SUBLANES = 8
MXU_WIDTH = 256
VMEM_LIMIT_BYTES = 56 * 1024 * 1024

ROW_TILE = 512
OUT_ROW_TILE = 1024
ATTN_TILE = 256
CONV_TILE = 512
CONV_HALO = 32
HGRN_CHUNK = 128
SAMPLE_STATE_TILE = 8


def _cparams(*sem):
    return pltpu.CompilerParams(dimension_semantics=sem, vmem_limit_bytes=VMEM_LIMIT_BYTES)


def _silu(x):
    return x * jax.nn.sigmoid(x)


def _dot(a, b):
    return jnp.dot(a, b, preferred_element_type=F32)


def _dot_nt(a, b):
    return lax.dot_general(a, b, (((1,), (1,)), ((), ())), preferred_element_type=F32)


def _dot_tn(a, b):
    return lax.dot_general(a, b, (((0,), (0,)), ((), ())), preferred_element_type=F32)


def _rms(x, g):
    return x * lax.rsqrt(jnp.mean(x * x, axis=-1, keepdims=True) + EPS) * g


def _ada_kernel(c_ref, w_ref, b_ref, o_ref):
    s = _silu(c_ref[...]).astype(BF16)
    o_ref[...] = _dot(s, w_ref[...].astype(BF16)) + b_ref[...]


def _ada_modulation(c_all, ada_w, ada_b):
    rows = c_all.shape[0]
    return pl.pallas_call(
        _ada_kernel,
        grid=(DEPTH, 3),
        in_specs=[
            pl.BlockSpec((rows, D_MODEL), lambda l, j: (0, 0)),
            pl.BlockSpec((None, D_MODEL, D_MODEL), lambda l, j: (l, 0, j)),
            pl.BlockSpec((None, 1, D_MODEL), lambda l, j: (l, 0, j)),
        ],
        out_specs=pl.BlockSpec((None, rows, D_MODEL), lambda l, j: (l, 0, j)),
        out_shape=jax.ShapeDtypeStruct((DEPTH, rows, 3 * D_MODEL), F32),
        compiler_params=_cparams("arbitrary", "arbitrary"),
        name="ada_modulation",
    )(c_all, ada_w, ada_b.reshape(DEPTH, 1, 3 * D_MODEL))


def _weight_spec(w):
    stacked, layer = w
    return pl.BlockSpec((None,) + stacked.shape[1:], lambda i: (layer, 0, 0))


def _modulated(x_ref, ng_ref, sc_ref, sh_ref):
    h = _rms(x_ref[...], ng_ref[...])
    return (h * (1.0 + sc_ref[...]) + sh_ref[...]).astype(BF16)


def _inproj_ac_kernel(x_ref, ng_ref, sc_ref, sh_ref, w_ref, cos_ref, sa_ref, sb_ref, *rest):
    q_ref, k_ref, kb_ref, v_ref, vb_ref, gb_ref, u_ref, ga_ref, h_scr = rest[-9:]
    h_scr[...] = _modulated(x_ref, ng_ref, sc_ref, sh_ref)

    def proj(s):
        return _dot(h_scr[...], w_ref[:, s * QK_W:(s + 1) * QK_W])

    cos, sa, sb = cos_ref[...], sa_ref[...], sb_ref[...]

    def rope(z):
        return (z * cos + pltpu.roll(z, LANES - ROT_DIM // 2, 1) * sa
                + pltpu.roll(z, ROT_DIM // 2, 1) * sb)

    zq = proj(0)
    for g in range(H_B):
        sl = slice(g * LANES, (g + 1) * LANES)
        q_ref[:, sl] = (rope(zq[:, sl]) * Q_SCALE).astype(BF16)
    tm = x_ref.shape[0]
    zk = proj(1)
    for g in range(H_B):
        sl = slice(g * LANES, (g + 1) * LANES)
        r = rope(zk[:, sl])
        k_ref[pl.ds(g, tm, stride=H_B), :] = r
        kb_ref[:, sl] = r.astype(BF16)
    zv = proj(2)
    for g in range(H_B):
        v_ref[pl.ds(g, tm, stride=H_B), :] = zv[:, g * LANES:(g + 1) * LANES]
    vb_ref[...] = zv.astype(BF16)
    gb_ref[...] = _silu(proj(3)).astype(BF16)
    a_val = proj(4)
    u_ref[...] = a_val * jax.nn.sigmoid(proj(5))
    ga_ref[...] = _silu(proj(6)).astype(BF16)


def _mod_specs(rows_per_mod, tm, tiles_per_mod):
    r = rows_per_mod
    return pl.BlockSpec((None, r, D_MODEL), lambda i: (i // tiles_per_mod, 0, 0))


def _inproj_ac(x, norm_g, scale, shift, w, cos_t, sa_t, sb_t, tm, tiles_per_mod, layer, kv_prev):
    m = x.shape[0]
    tab_tiles = cos_t.shape[0] // tm
    mod_spec = _mod_specs(scale.shape[1], tm, tiles_per_mod)
    row512 = pl.BlockSpec((tm, QK_W), lambda i: (i, 0))
    heads = pl.BlockSpec((None, H_B * tm, LANES), lambda i: (layer, i, 0))
    tab_spec = pl.BlockSpec((tm, LANES), lambda i: (i % tab_tiles, 0))
    out_shape = [
        jax.ShapeDtypeStruct((m, QK_W), BF16),
        jax.ShapeDtypeStruct((N_AC, H_B * m, LANES), F32),
        jax.ShapeDtypeStruct((m, QK_W), BF16),
        jax.ShapeDtypeStruct((N_AC, H_B * m, LANES), F32),
        jax.ShapeDtypeStruct((m, W_B), BF16),
        jax.ShapeDtypeStruct((m, W_B), BF16),
        jax.ShapeDtypeStruct((m, W_A), F32),
        jax.ShapeDtypeStruct((m, W_A), BF16),
    ]
    in_specs = [
        pl.BlockSpec((tm, D_MODEL), lambda i: (i, 0)),
        pl.BlockSpec((1, D_MODEL), lambda i: (0, 0)),
        mod_spec, mod_spec,
        _weight_spec(w),
        tab_spec, tab_spec, tab_spec,
    ]
    args = [x, norm_g.reshape(1, D_MODEL), scale, shift, w[0], cos_t, sa_t, sb_t]
    aliases = {}
    if kv_prev is not None:
        aliases = {len(args): 1, len(args) + 1: 3}
        in_specs += [pl.BlockSpec(memory_space=pl.ANY)] * 2
        args += list(kv_prev)
    return pl.pallas_call(
        _inproj_ac_kernel,
        grid=(m // tm,),
        in_specs=in_specs,
        out_specs=[row512, heads, row512, heads, row512, row512, row512, row512],
        out_shape=out_shape,
        input_output_aliases=aliases,
        scratch_shapes=[pltpu.VMEM((tm, D_MODEL), BF16)],
        compiler_params=_cparams("arbitrary"),
        name="inproj_conv_attn",
    )(*args)


def _inproj_c_kernel(x_ref, ng_ref, sc_ref, sh_ref, w_ref, lb_ref, q_ref, lg_ref, i_ref, sg_ref, h_scr):
    h_scr[...] = _modulated(x_ref, ng_ref, sc_ref, sh_ref)

    def proj(s):
        return _dot(h_scr[...], w_ref[:, s * W_C:(s + 1) * W_C])

    q_ref[...] = _silu(proj(0))
    lb = lb_ref[...]
    lg_ref[...] = jnp.log(lb + (1.0 - lb) * jax.nn.sigmoid(proj(1))) * LOG2_E
    i_ref[...] = proj(2)
    sg_ref[...] = _silu(proj(3)).astype(BF16)


def _inproj_c(x, norm_g, scale, shift, w, lb, tm, tiles_per_mod):
    m = x.shape[0]
    mod_spec = _mod_specs(scale.shape[1], tm, tiles_per_mod)
    row = pl.BlockSpec((tm, W_C), lambda i: (i, 0))
    out_shape = [
        jax.ShapeDtypeStruct((m, W_C), F32),
        jax.ShapeDtypeStruct((m, W_C), F32),
        jax.ShapeDtypeStruct((m, W_C), F32),
        jax.ShapeDtypeStruct((m, W_C), BF16),
    ]
    return pl.pallas_call(
        _inproj_c_kernel,
        grid=(m // tm,),
        in_specs=[
            pl.BlockSpec((tm, D_MODEL), lambda i: (i, 0)),
            pl.BlockSpec((1, D_MODEL), lambda i: (0, 0)),
            mod_spec, mod_spec,
            _weight_spec(w),
            pl.BlockSpec((1, W_C), lambda i: (0, 0)),
        ],
        out_specs=[row] * 4,
        out_shape=out_shape,
        scratch_shapes=[pltpu.VMEM((tm, D_MODEL), BF16)],
        compiler_params=_cparams("arbitrary"),
        name="inproj_hgrn",
    )(x, norm_g.reshape(1, D_MODEL), scale, shift, w[0], lb.reshape(1, W_C))


def _outproj_kernel(*refs, n_act, final_norm):
    acts = refs[:n_act]
    w_ref, x_ref, gate_ref = refs[n_act:n_act + 3]
    o_ref = refs[-1]
    acc = None
    lo = 0
    for a_ref in acts:
        width = a_ref.shape[-1]
        part = _dot(a_ref[...], w_ref[lo:lo + width, :])
        acc = part if acc is None else acc + part
        lo += width
    y = x_ref[...] + gate_ref[...] * acc
    if final_norm:
        y = _rms(y, refs[n_act + 3][...])
    o_ref[...] = y


def _outproj(acts, w, x, gate, tm, tiles_per_mod, final_g=None):
    m = x.shape[0]
    mod_spec = _mod_specs(gate.shape[1], tm, tiles_per_mod)
    in_specs = [pl.BlockSpec((tm, a.shape[1]), lambda i: (i, 0)) for a in acts]
    in_specs += [
        _weight_spec(w),
        pl.BlockSpec((tm, D_MODEL), lambda i: (i, 0)),
        mod_spec,
    ]
    args = list(acts) + [w[0], x, gate]
    if final_g is not None:
        in_specs.append(pl.BlockSpec((1, D_MODEL), lambda i: (0, 0)))
        args.append(final_g.reshape(1, D_MODEL))
    return pl.pallas_call(
        functools.partial(_outproj_kernel, n_act=len(acts), final_norm=final_g is not None),
        grid=(m // tm,),
        in_specs=in_specs,
        out_specs=pl.BlockSpec((tm, D_MODEL), lambda i: (i, 0)),
        out_shape=jax.ShapeDtypeStruct((m, D_MODEL), F32),
        compiler_params=_cparams("arbitrary"),
        name="outproj",
    )(*args)


def _attn_kernel(pt_ref, lam_ref, sgc_ref, sgr_ref, q_ref, k_ref, v_ref, gb_ref,
                 qs_ref, kn_ref, vn_ref, gbs_ref, *rest, tile, lam_init, n_dec, n_pages):
    pages = rest[:2 * n_dec * n_pages]
    o_ref, os_ref, q2_scr, s_a, s_b, m_scr, l_scr, acc_scr = rest[2 * n_dec * n_pages:]
    sg_ref = sgc_ref
    qi = pl.program_id(1)
    heads = [slice(h * LANES, (h + 1) * LANES) for h in range(H_B)]
    lane = lax.broadcasted_iota(jnp.int32, (tile, LANES), 1)
    for h, sl in enumerate(heads):
        q = q_ref[:, sl]
        zero = jnp.zeros_like(q)
        q2_scr[h] = jnp.concatenate([jnp.where(lane < HD_B, q, zero), jnp.where(lane >= HD_B, q, zero)], axis=0)
    m_scr[...] = jnp.full(m_scr.shape, NEG_INF, F32)
    l_scr[...] = jnp.zeros(l_scr.shape, F32)
    acc_scr[...] = jnp.zeros(acc_scr.shape, F32)

    def scores(t, s_scr):
        start = pl.multiple_of(t * tile, tile)
        for h, sl in enumerate(heads):
            s_scr[h] = _dot_nt(k_ref[pl.ds(start, tile), sl], q2_scr[h])

    def softmax_part(s_scr, mask):
        parts = []
        for h in range(H_B):
            sh = s_scr[h]
            if mask is not None:
                sh = jnp.where(mask, sh, NEG_INF)
            m_old = m_scr[h]
            m_new = jnp.maximum(m_old, jnp.max(sh, axis=0, keepdims=True))
            alpha = jnp.exp2(m_old - m_new)
            p = jnp.exp2(sh - m_new)
            m_scr[h] = m_new
            l_scr[h] = alpha * l_scr[h] + jnp.sum(p, axis=0, keepdims=True)
            parts.append((alpha, p.astype(BF16)))
        return parts

    def value_part(t, parts):
        start = pl.multiple_of(t * tile, tile)
        for h, sl in enumerate(heads):
            alpha, p = parts[h]
            acc_scr[h] = alpha * acc_scr[h] + _dot_tn(v_ref[pl.ds(start, tile), sl], p)

    def consume(s_scr, t, mask):
        value_part(t, softmax_part(s_scr, mask))

    odd = qi % 2

    @pl.when(odd == 1)
    def _():
        scores(0, s_b)
        scores(1, s_a)
        consume(s_b, 0, None)

    @pl.when(odd == 0)
    def _():
        scores(0, s_a)

    def body(jj, carry):
        t = odd + 2 * jj
        scores(t + 1, s_b)
        consume(s_a, t, None)
        scores(t + 2, s_a)
        consume(s_b, t + 1, None)
        return carry

    lax.fori_loop(0, qi // 2, body, 0)
    key = lax.broadcasted_iota(jnp.int32, (tile, 2 * tile), 0)
    qry = lax.broadcasted_iota(jnp.int32, (tile, 2 * tile), 1)
    diag = softmax_part(s_a, key <= jnp.where(qry >= tile, qry - tile, qry))
    lam = lam_ref[:, 0:1]
    dec = []
    for r in range(n_dec):
        k_pages = pages[2 * r * n_pages:(2 * r + 1) * n_pages]
        v_pages = pages[(2 * r + 1) * n_pages:(2 * r + 2) * n_pages]
        dec.append(_decode_scores(qs_ref[r].astype(F32), kn_ref[r], k_pages) + (v_pages,))
    value_part(qi, diag)
    for r, (s, s_new, v_pages) in enumerate(dec):
        os_ref[r] = _decode_finish(s, s_new, lam, vn_ref[r], v_pages, sgr_ref[...],
                                   gbs_ref[r].astype(F32), lam_init).astype(BF16)
    for h, sl in enumerate(heads):
        l, acc = l_scr[h], acc_scr[h]
        o = acc[:, :tile] / l[:, :tile] - lam_ref[:, 0:1] * (acc[:, tile:] / l[:, tile:])
        y = o * lax.rsqrt(jnp.mean(o * o, axis=0, keepdims=True) + EPS) * sg_ref[...]
        o_ref[:, sl] = (y.T * (1.0 - lam_init) * gb_ref[:, sl].astype(F32)).astype(BF16)


def _attention(q, kb, vb, gb, batch, seq, qs, kn_all, vn_all, gbs, cache_k4, cache_v4, layer, page_table,
               lam, subln_g, lam_init):
    tile = ATTN_TILE
    nq = seq // tile
    n_seq, n_pages = page_table.shape
    n_dec = n_seq // (batch * nq)
    assert n_dec * batch * nq == n_seq
    q3, k3, v3, g3 = (a.reshape(batch, seq, QK_W) for a in (q, kb, vb, gb))
    qspec = pl.BlockSpec((None, tile, QK_W), lambda b, i, pt: (b, i, 0))
    kvspec = pl.BlockSpec((None, seq, QK_W), lambda b, i, pt: (b, 0, 0))
    rows = pl.BlockSpec((n_dec, 1, QK_W), lambda b, i, pt: (b * nq + i, 0, 0))
    new_kv = pl.BlockSpec((None, n_dec, H_B, LANES), lambda b, i, pt: (layer, b * nq + i, 0, 0))

    def page_spec(r, j):
        return pl.BlockSpec((None, None, H_B * PAGE_SIZE, LANES),
                            lambda b, i, pt: (layer, pt[(b * nq + i) * n_dec + r, j], 0, 0))

    page_specs, page_args = [], []
    for r in range(n_dec):
        for cache in (cache_k4, cache_v4):
            page_specs += [page_spec(r, j) for j in range(n_pages)]
            page_args += [cache] * n_pages
    grid_spec = pltpu.PrefetchScalarGridSpec(
        num_scalar_prefetch=1,
        grid=(batch, nq),
        in_specs=[pl.BlockSpec((1, LANES), lambda b, i, pt: (0, 0)),
                  pl.BlockSpec((DV_B, 1), lambda b, i, pt: (0, 0)),
                  pl.BlockSpec((1, DV_B), lambda b, i, pt: (0, 0)),
                  qspec, kvspec, kvspec, qspec, rows, new_kv, new_kv, rows] + page_specs,
        out_specs=[qspec, rows],
        scratch_shapes=[pltpu.VMEM((H_B, 2 * tile, LANES), BF16),
                        pltpu.VMEM((H_B, tile, 2 * tile), F32),
                        pltpu.VMEM((H_B, tile, 2 * tile), F32),
                        pltpu.VMEM((H_B, 1, 2 * tile), F32),
                        pltpu.VMEM((H_B, 1, 2 * tile), F32),
                        pltpu.VMEM((H_B, DV_B, 2 * tile), F32)],
    )
    r3 = lambda a: a.reshape(n_seq, 1, QK_W)
    h4 = lambda a: a.reshape(N_AC, n_seq, H_B, LANES)
    out, out_s = pl.pallas_call(
        functools.partial(_attn_kernel, tile=tile, lam_init=lam_init, n_dec=n_dec, n_pages=n_pages),
        grid_spec=grid_spec,
        out_shape=[jax.ShapeDtypeStruct((batch, seq, W_B), BF16),
                   jax.ShapeDtypeStruct((n_seq, 1, W_B), BF16)],
        compiler_params=_cparams("arbitrary", "arbitrary"),
        name="diff_attention",
    )(page_table, lam, subln_g.reshape(DV_B, 1), subln_g.reshape(1, DV_B), q3, k3, v3, g3,
      r3(qs), h4(kn_all), h4(vn_all), r3(gbs), *page_args)
    return out.reshape(batch * seq, W_B), out_s.reshape(n_seq, W_B)


def _conv_post(y, ln_g, ln_b, ga):
    yc = y - jnp.mean(y, axis=-1, keepdims=True)
    yn = yc * lax.rsqrt(jnp.mean(yc * yc, axis=-1, keepdims=True) + EPS) * ln_g + ln_b
    return _silu(yn) * ga


def _conv_kernel(u_ref, ga_ref, w_ref, cb_ref, lg_ref, lb_ref, y_ref, st_ref, ext_scr, sh_scr, *, tile, rows):
    t = pl.program_id(1)
    first = CONV_HALO - (CONV_W - 1)
    sh_rows = sh_scr.shape[1]

    @pl.when(t == 0)
    def _():
        ext_scr[0:CONV_HALO, :] = jnp.zeros((CONV_HALO, W_A), F32)

    @pl.when(t > 0)
    def _():
        ext_scr[0:CONV_HALO, :] = ext_scr[tile:tile + CONV_HALO, :]

    ext_scr[CONV_HALO:CONV_HALO + tile, :] = u_ref[...]
    for r in range(1, SUBLANES):
        sh_scr[r - 1] = ext_scr[r:r + sh_rows, :]
    for r0 in range(0, tile, rows):
        acc = jnp.zeros((rows, W_A), F32) + cb_ref[...]
        for j in range(CONV_W):
            off = first + j
            r, base = off % SUBLANES, off - off % SUBLANES + r0
            src = ext_scr[base:base + rows, :] if r == 0 else sh_scr[r - 1, base:base + rows, :]
            acc = acc + src * w_ref[j:j + 1, :]
        y_ref[r0:r0 + rows, :] = _conv_post(acc, lg_ref[...], lb_ref[...],
                                            ga_ref[r0:r0 + rows, :].astype(F32)).astype(BF16)

    @pl.when(t == pl.num_programs(1) - 1)
    def _():
        st_ref[...] = ext_scr[CONV_HALO + tile - (CONV_W - 1):CONV_HALO + tile, :]


def _prompt_conv(u, ga, conv_w, conv_b, ln_g, ln_b, batch, seq):
    tile = CONV_TILE
    u3, g3 = u.reshape(batch, seq, W_A), ga.reshape(batch, seq, W_A)
    tspec = pl.BlockSpec((None, tile, W_A), lambda b, t: (b, t, 0))
    vec = pl.BlockSpec((1, W_A), lambda b, t: (0, 0))
    y, st = pl.pallas_call(
        functools.partial(_conv_kernel, tile=tile, rows=64),
        grid=(batch, seq // tile),
        in_specs=[tspec, tspec, pl.BlockSpec((CONV_W, W_A), lambda b, t: (0, 0)), vec, vec, vec],
        out_specs=[tspec, pl.BlockSpec((None, CONV_W - 1, W_A), lambda b, t: (b, 0, 0))],
        out_shape=[jax.ShapeDtypeStruct((batch, seq, W_A), BF16),
                   jax.ShapeDtypeStruct((batch, CONV_W - 1, W_A), F32)],
        scratch_shapes=[pltpu.VMEM((CONV_HALO + tile, W_A), F32),
                        pltpu.VMEM((SUBLANES - 1, CONV_HALO + tile - SUBLANES, W_A), F32)],
        compiler_params=_cparams("arbitrary", "arbitrary"),
        name="prompt_conv",
    )(u3, g3, conv_w, conv_b.reshape(1, W_A), ln_g.reshape(1, W_A), ln_b.reshape(1, W_A))
    return y.reshape(batch * seq, W_A), st


def _hgrn_tables(chunk):
    levels = int(math.log2(chunk))
    t = np.arange(chunk)[:, None]
    u = np.arange(chunk)[None, :]
    masks = []
    for v in range(levels):
        m = 1 << v
        masks.append((t // (2 * m) == u // (2 * m)) & (t % (2 * m) >= m) & (u % (2 * m) < m))
    return jnp.asarray(u <= t, BF16), jnp.asarray(np.stack(masks), BF16), levels


def _split3(x):
    hi = x.astype(BF16)
    r = x - hi.astype(F32)
    mid = r.astype(BF16)
    lo = (r - mid.astype(F32)).astype(BF16)
    return hi, mid, lo


def _hgrn_level_exponent(b, g2, m, row):
    chunk, width = b.shape
    if m == 1:
        return jnp.where((row & 1) != 0, g2, 0.0)
    if 2 * m < SUBLANES:
        b3 = b.reshape(chunk // SUBLANES, SUBLANES, width)
        sub = lax.broadcasted_iota(jnp.int32, b3.shape, 1)
        bm = b3[:, m - 1:m, :]
        for blk in range(1, SUBLANES // (2 * m)):
            bm = jnp.where(sub < blk * 2 * m, bm, b3[:, blk * 2 * m + m - 1:blk * 2 * m + m, :])
    else:
        b3 = b.reshape(chunk // (2 * m), 2 * m, width)
        bm = b3[:, m - 1:m, :]
    d = lax.bitcast_convert_type((b3 - bm).reshape(chunk, width), jnp.uint32)
    return lax.bitcast_convert_type(d | jnp.uint32(0x80000000), F32)


def _hgrn_chunk_stages(q_ref, lg_ref, i_ref, sg_ref, tri_ref, msk_ref, gn_ref, o_ref, st_scr, chunk, levels):
    heads = [slice(h * LANES, (h + 1) * LANES) for h in range(H_C)]
    g2 = lg_ref[...]
    q = q_ref[...]
    v = i_ref[...]
    k = 1.0 - jnp.exp2(g2)
    b3 = _dot(tri_ref[...], jnp.concatenate(_split3(g2), axis=1))
    yield
    b = b3[:, :W_C] + b3[:, W_C:2 * W_C] + b3[:, 2 * W_C:]
    row = lax.broadcasted_iota(jnp.int32, (chunk, W_C), 0)
    qb, kb = q.astype(BF16), k.astype(BF16)
    a = [jnp.zeros((chunk, chunk), BF16)] * H_C
    for lv in range(levels):
        m = 1 << lv
        x = jnp.where((row & m) != 0, qb, kb) * jnp.exp2(_hgrn_level_exponent(b, g2, m, row).astype(BF16))
        msk = msk_ref[lv]
        a = [a[h] + msk * _dot_nt(x[:, sl], x[:, sl]).astype(BF16) for h, sl in enumerate(heads)]
        yield
    vb = v.astype(BF16)
    qe = (q * jnp.exp2(b)).astype(BF16)
    b_last = b[chunk - 1:chunk, :]
    kd = (k * jnp.exp2(b_last - b)).astype(BF16)
    decay = jnp.exp2(b_last)
    qk = q * k
    for h, sl in enumerate(heads):
        st = st_scr[h]
        o = _dot(a[h], vb[:, sl]) + jnp.sum(qk[:, sl], axis=1, keepdims=True) * v[:, sl]
        o = o + _dot_nt(qe[:, sl], st.astype(BF16))
        st_scr[h] = st * decay[:, sl] + _dot_tn(vb[:, sl], kd[:, sl])
        o_ref[:, sl] = (_rms(o, gn_ref[...]) * sg_ref[:, sl].astype(F32)).astype(BF16)


def _inproj_hgrn_kernel(x_ref, ng_ref, sc_ref, sh_ref, w_ref, lb_ref, tri_ref, msk_ref, gn_ref,
                        o_ref, s_ref, h_scr, q_scr, lg_scr, iv_scr, sg_scr, st_scr,
                        *, chunk, levels, tiles_per_batch):
    i = pl.program_id(0)
    slot = i % 2
    prev = 1 - slot
    tm = x_ref.shape[0]
    assert tm // chunk == 4

    @pl.when(i == 0)
    def _():
        for scr in (q_scr, lg_scr, iv_scr, sg_scr):
            scr[1] = jnp.zeros(scr.shape[1:], scr.dtype)

    @pl.when((i + tiles_per_batch - 1) % tiles_per_batch == 0)
    def _():
        st_scr[...] = jnp.zeros_like(st_scr)

    h_scr[...] = _modulated(x_ref, ng_ref, sc_ref, sh_ref)
    lb = lb_ref[...]

    sections = [
        (q_scr, lambda z, cols: _silu(z)),
        (lg_scr, lambda z, cols: jnp.log(lb[:, cols] + (1.0 - lb[:, cols]) * jax.nn.sigmoid(z)) * LOG2_E),
        (iv_scr, lambda z, cols: z),
        (sg_scr, lambda z, cols: _silu(z).astype(BF16)),
    ]
    n_pieces = W_C // MXU_WIDTH
    for s, (scr, post) in enumerate(sections):
        rows = pl.ds(s * chunk, chunk)
        stages = _hgrn_chunk_stages(q_scr.at[prev, rows], lg_scr.at[prev, rows], iv_scr.at[prev, rows],
                                    sg_scr.at[prev, rows], tri_ref, msk_ref, gn_ref, o_ref.at[rows],
                                    st_scr, chunk, levels)
        for p in range(n_pieces):
            cols = slice(p * MXU_WIDTH, (p + 1) * MXU_WIDTH)
            z = _dot(h_scr[...], w_ref[:, s * W_C + p * MXU_WIDTH:s * W_C + (p + 1) * MXU_WIDTH])
            scr[slot, :, cols] = post(z, cols)
            next(stages, None)
            next(stages, None)
        for _ in stages:
            pass

    @pl.when(jnp.logical_and(i > 0, i % tiles_per_batch == 0))
    def _():
        for h in range(H_C):
            s_ref[h] = st_scr[h].T


def _inproj_hgrn(x, norm_g, scale, shift, w, lb, gn_g, batch, seq):
    tm, chunk = ROW_TILE, HGRN_CHUNK
    m = x.shape[0]
    nt = m // tm
    tiles_per_batch = seq // tm
    tri, masks, levels = _hgrn_tables(chunk)
    cur = lambda i: jnp.minimum(i, nt - 1)
    done = lambda i: jnp.maximum(i - 1, 0)
    mod_spec = pl.BlockSpec((None, 1, D_MODEL), lambda i: (cur(i) // tiles_per_batch, 0, 0))
    const = lambda shape: pl.BlockSpec(shape, lambda i: (0,) * len(shape))
    o, s = pl.pallas_call(
        functools.partial(_inproj_hgrn_kernel, chunk=chunk, levels=levels, tiles_per_batch=tiles_per_batch),
        grid=(nt + 1,),
        in_specs=[
            pl.BlockSpec((tm, D_MODEL), lambda i: (cur(i), 0)),
            const((1, D_MODEL)),
            mod_spec, mod_spec,
            _weight_spec(w),
            const((1, W_C)),
            const(tri.shape), const(masks.shape), const((1, DV_C)),
        ],
        out_specs=[pl.BlockSpec((tm, W_C), lambda i: (done(i), 0)),
                   pl.BlockSpec((None, H_C, DK_C, DV_C), lambda i: (done(i) // tiles_per_batch, 0, 0, 0))],
        out_shape=[jax.ShapeDtypeStruct((m, W_C), BF16),
                   jax.ShapeDtypeStruct((batch, H_C, DK_C, DV_C), F32)],
        scratch_shapes=[pltpu.VMEM((tm, D_MODEL), BF16),
                        pltpu.VMEM((2, tm, W_C), F32),
                        pltpu.VMEM((2, tm, W_C), F32),
                        pltpu.VMEM((2, tm, W_C), F32),
                        pltpu.VMEM((2, tm, W_C), BF16),
                        pltpu.VMEM((H_C, DV_C, DK_C), F32)],
        compiler_params=_cparams("arbitrary"),
        name="inproj_hgrn_recurrence",
    )(x, norm_g.reshape(1, D_MODEL), scale, shift, w[0], lb.reshape(1, W_C), tri, masks, gn_g.reshape(1, DV_C))
    return o, s


_DEC_ROWS = 2 * H_B


def _decode_head_rows(pieces):
    row = lax.broadcasted_iota(jnp.int32, (_DEC_ROWS, LANES), 0)
    out = jnp.zeros((_DEC_ROWS, LANES), F32)
    for h in range(H_B):
        out = jnp.where(row // 2 == h, jnp.broadcast_to(pieces[h], (_DEC_ROWS, LANES)), out)
    return out


def _decode_scores(q, kn, k_pages):
    row = lax.broadcasted_iota(jnp.int32, (_DEC_ROWS, LANES), 0)
    lane = lax.broadcasted_iota(jnp.int32, (_DEC_ROWS, LANES), 1)
    qm32 = jnp.where(lane // HD_B == row % 2,
                     _decode_head_rows([q[:, h * LANES:(h + 1) * LANES] for h in range(H_B)]), 0.0)
    qm = qm32.astype(BF16)
    s = jnp.concatenate([_dot_nt(qm, kp[...].astype(BF16)) for kp in k_pages], axis=1)
    key_head = lax.broadcasted_iota(jnp.int32, s.shape, 1) % H_B
    s = jnp.where(key_head == lax.broadcasted_iota(jnp.int32, s.shape, 0) // 2, s, NEG_INF)
    s_new = jnp.sum(qm32 * _decode_head_rows([kn[h:h + 1, :] for h in range(H_B)]), axis=-1, keepdims=True)
    return s, s_new


def _decode_finish(s, s_new, lam, vn, v_pages, subln_row, gb, lam_init):
    page_rows = H_B * PAGE_SIZE
    m = jnp.maximum(jnp.max(s, axis=-1, keepdims=True), s_new)
    p = jnp.exp2(s - m)
    p_new = jnp.exp2(s_new - m)
    l = jnp.sum(p, axis=-1, keepdims=True) + p_new
    comp = lax.broadcasted_iota(jnp.int32, (_DEC_ROWS, 1), 0) % 2
    wgt = jnp.where(comp == 0, 1.0, -lam) / l
    pw = (p * wgt).astype(BF16)
    o8 = (p_new * wgt) * _decode_head_rows([vn[h:h + 1, :] for h in range(H_B)])
    for j, vp in enumerate(v_pages):
        o8 = o8 + _dot(pw[:, j * page_rows:(j + 1) * page_rows], vp[...].astype(BF16))
    out = []
    for h in range(H_B):
        o = o8[2 * h:2 * h + 1, :] + o8[2 * h + 1:2 * h + 2, :]
        out.append(_rms(o, subln_row) * (1.0 - lam_init) * gb[:, h * DV_B:(h + 1) * DV_B])
    return jnp.concatenate(out, axis=1)


def _decode_conv_kernel(st_ref, u_ref, ga_ref, w_ref, cb_ref, lg_ref, lb_ref, *rest):
    y_ref, ns_ref = rest[-2:]
    u = u_ref[...]
    y = u * w_ref[CONV_W - 1:CONV_W, :] + cb_ref[...]
    for j in range(CONV_W - 1):
        y = y + st_ref[j] * w_ref[j:j + 1, :]
        if j > 0:
            ns_ref[j - 1] = st_ref[j]
    ns_ref[CONV_W - 2] = u
    y_ref[...] = _conv_post(y, lg_ref[...], lb_ref[...], ga_ref[...].astype(F32)).astype(BF16)


def _decode_conv(state_t, layer, u, ga, conv_w, conv_b, ln_g, ln_b, new_prev):
    n = u.shape[0]
    tile = 32
    vec = pl.BlockSpec((1, W_A), lambda i: (0, 0))
    sspec = pl.BlockSpec((None, CONV_W - 1, tile, W_A), lambda i: (layer, 0, i, 0))
    rspec = pl.BlockSpec((tile, W_A), lambda i: (i, 0))
    in_specs = [sspec, rspec, rspec, pl.BlockSpec((CONV_W, W_A), lambda i: (0, 0)), vec, vec, vec]
    args = [state_t, u, ga, conv_w, conv_b.reshape(1, W_A), ln_g.reshape(1, W_A), ln_b.reshape(1, W_A)]
    aliases = {}
    if new_prev is not None:
        aliases = {len(args): 1}
        in_specs.append(pl.BlockSpec(memory_space=pl.ANY))
        args.append(new_prev)
    return pl.pallas_call(
        _decode_conv_kernel,
        grid=(n // tile,),
        in_specs=in_specs,
        out_specs=[rspec, sspec],
        out_shape=[jax.ShapeDtypeStruct((n, W_A), BF16),
                   jax.ShapeDtypeStruct((N_AC, CONV_W - 1, n, W_A), F32)],
        input_output_aliases=aliases,
        compiler_params=_cparams("arbitrary"),
        name="decode_conv",
    )(*args)


def _decode_spread_table():
    rows = np.arange(3 * 3 * H_C)
    cols = np.arange(3 * W_C)
    same_vec = (rows[:, None] // (3 * H_C)) == (cols[None, :] // W_C)
    same_head = (rows[:, None] % H_C) == ((cols[None, :] % W_C) // DV_C)
    return jnp.asarray(same_vec & same_head, BF16)


def _decode_hgrn_kernel(q_ref, lg_ref, i_ref, sg_ref, gn_ref, e_ref, s_ref, *rest, n_seq):
    o_ref, ns_ref = rest[-2:]

    def split_rows(x):
        return [p.astype(F32) for p in _split3(x)]

    def one(n):
        f8 = jnp.exp2(lg_ref[n])
        parts = jnp.concatenate(split_rows(f8) + split_rows(1.0 - f8) + split_rows(q_ref[n]), axis=0)
        spread = _dot(parts.T.astype(BF16), e_ref[...])
        v8 = i_ref[n]
        rows = []
        for h in range(H_C):
            f_b, k_b, q_b =(spread[:, j * W_C + h * DV_C:j * W_C + (h + 1) * DV_C] for j in range(3))
            s_new = f_b * s_ref[n, h] + k_b * v8[h:h + 1, :]
            ns_ref[n, h] = s_new
            rows.append(jnp.sum(q_b * s_new, axis=0, keepdims=True))
        o = jnp.concatenate(rows, axis=0)
        o_ref[n] = (_rms(o, gn_ref[...]) * sg_ref[n].astype(F32)).astype(BF16)

    def body(pair, carry):
        one(2 * pair)
        one(2 * pair + 1)
        return carry

    assert n_seq % 2 == 0
    lax.fori_loop(0, n_seq // 2, body, 0)


def _decode_hgrn(qs, lg, iv, sg, gn_g, state_all, layer, new_prev):
    n = qs.shape[0]
    tile = SAMPLE_STATE_TILE
    hspec = pl.BlockSpec((tile, H_C, DV_C), lambda i: (i, 0, 0))
    sspec = pl.BlockSpec((None, tile, H_C, DK_C, DV_C), lambda i: (layer, i, 0, 0, 0))
    h3 = lambda a: a.reshape(n, H_C, DV_C)
    table = _decode_spread_table()
    in_specs = [hspec, hspec, hspec, hspec, pl.BlockSpec((1, DV_C), lambda i: (0, 0)),
                pl.BlockSpec(table.shape, lambda i: (0, 0)), sspec]
    args = [h3(qs), h3(lg), h3(iv), h3(sg), gn_g.reshape(1, DV_C), table, state_all]
    aliases = {}
    if new_prev is not None:
        aliases = {len(args): 1}
        in_specs.append(pl.BlockSpec(memory_space=pl.ANY))
        args.append(new_prev)
    o, ns = pl.pallas_call(
        functools.partial(_decode_hgrn_kernel, n_seq=tile),
        grid=(n // tile,),
        in_specs=in_specs,
        out_specs=[hspec, sspec],
        out_shape=[jax.ShapeDtypeStruct((n, H_C, DV_C), BF16),
                   jax.ShapeDtypeStruct((N_C, n, H_C, DK_C, DV_C), F32)],
        input_output_aliases=aliases,
        compiler_params=_cparams("arbitrary"),
        name="decode_hgrn",
    )(*args)
    return o.reshape(n, W_C), ns


def _rope_tables(pos, rows):
    half = ROT_DIM // 2
    inv_freq = ROPE_THETA ** (-jnp.arange(half, dtype=F32) / half)
    ang = pos.astype(F32)[:, None] * inv_freq[None, :]
    cos, sin = jnp.cos(ang), jnp.sin(ang)
    n = pos.shape[0]
    ones = jnp.ones((n, HD_B - ROT_DIM), F32)
    zeros = jnp.zeros((n, HD_B - ROT_DIM), F32)
    z8 = jnp.zeros((n, half), F32)
    cos_t = jnp.concatenate([cos, cos, ones], axis=1)
    sa_t = jnp.concatenate([-sin, z8, zeros], axis=1)
    sb_t = jnp.concatenate([z8, sin, zeros], axis=1)
    out = []
    for t in (cos_t, sa_t, sb_t):
        t = jnp.concatenate([t, t], axis=1)
        out.append(jnp.broadcast_to(t, (rows, LANES)) if n == 1 else t)
    return out


def kernel(x_prompt, x_sample, c_prompt, c_sample, cache_k, cache_v, page_table, state_conv, state_hgrn,
           norm_g, ada_w, ada_b, w_in_ac, w_out_ac, conv_w, conv_b, ln_g, ln_b, lam_q1, lam_k1, lam_q2,
           lam_k2, subln_g, w_in_c, w_out_c, gn_g, lb_logits, final_g):
    bp, tp = x_prompt.shape[:2]
    bs, ts = x_sample.shape[:2]
    assert ts == 1
    n_pool = cache_k.shape[1]
    n_past = page_table.shape[1] * PAGE_SIZE
    mp = bp * tp

    lb_sm = jax.nn.softmax(lb_logits.astype(F32), axis=0)
    lb_all = jnp.cumsum(lb_sm, axis=0) - lb_sm[0]

    mod = _ada_modulation(jnp.concatenate([c_prompt, c_sample], axis=0), ada_w, ada_b)

    def mods(l):
        out = []
        for j in range(3):
            m = mod[l, :, j * D_MODEL:(j + 1) * D_MODEL]
            out.append((m[:bp].reshape(bp, 1, D_MODEL), m[bp:].reshape(1, bs, D_MODEL)))
        return out

    tabs_p = _rope_tables(jnp.arange(tp), tp)
    tabs_s = _rope_tables(jnp.full((1,), n_past), bs)
    ck4 = cache_k.reshape(N_AC, n_pool, PAGE_SIZE * H_B, 2 * HD_B)
    cv4 = cache_v.reshape(N_AC, n_pool, PAGE_SIZE * H_B, DV_B)

    w_in_ac_b, w_out_ac_b, w_in_c_b, w_out_c_b = (w.astype(BF16) for w in (w_in_ac, w_out_ac, w_in_c, w_out_c))

    tiles_p = tp // ROW_TILE
    hp = x_prompt.reshape(mp, D_MODEL)
    hs = x_sample.reshape(bs, D_MODEL)
    state_conv_t = jnp.swapaxes(state_conv, 1, 2)
    cp_l, sp_l = [], []
    kv_p = kv_s = ss = cs = None
    for l in range(DEPTH):
        (sh_p, sh_s), (sc_p, sc_s), (gt_p, gt_s) = mods(l)
        last = final_g if l == DEPTH - 1 else None
        if l % 2 == 0:
            a = l // 2
            lam_init = 0.8 - 0.6 * math.exp(-0.3 * l)
            lam = (jnp.exp(jnp.sum(lam_q1[a].astype(F32) * lam_k1[a].astype(F32)))
                   - jnp.exp(jnp.sum(lam_q2[a].astype(F32) * lam_k2[a].astype(F32))) + lam_init)
            lam = jnp.full((1, LANES), lam, F32)
            w_in, w_out = (w_in_ac_b, a), (w_out_ac_b, a)
            q, k, kb, v, vb, gb, u, ga = _inproj_ac(hp, norm_g[l], sc_p, sh_p, w_in, *tabs_p,
                                                    ROW_TILE, tiles_p, a, kv_p)
            kv_p = (k, v)
            qs, k, _, v, _, gbs, us, gas = _inproj_ac(hs, norm_g[l], sc_s, sh_s, w_in, *tabs_s, bs, 1, a, kv_s)
            kv_s = (k, v)
            ob, obs = _attention(q, kb, vb, gb, bp, tp, qs, k, v, gbs, ck4, cv4, a, page_table,
                                 lam, subln_g[a], lam_init)
            y, cst = _prompt_conv(u, ga, conv_w[a], conv_b[a], ln_g[a], ln_b[a], bp, tp)
            hp = _outproj([ob, y], w_out, hp, gt_p, OUT_ROW_TILE, tp // OUT_ROW_TILE, last)
            cp_l.append(cst)
            y, cs = _decode_conv(state_conv_t, a, us, gas, conv_w[a], conv_b[a], ln_g[a], ln_b[a], cs)
            hs = _outproj([obs, y], w_out, hs, gt_s, bs, 1, last)
        else:
            ci = l // 2
            w_in, w_out = (w_in_c_b, ci), (w_out_c_b, ci)
            o, st = _inproj_hgrn(hp, norm_g[l], sc_p, sh_p, w_in, lb_all[l], gn_g[ci], bp, tp)
            hp = _outproj([o], w_out, hp, gt_p, OUT_ROW_TILE, tp // OUT_ROW_TILE, last)
            sp_l.append(st)
            qs, lg, iv, sg = _inproj_c(hs, norm_g[l], sc_s, sh_s, w_in, lb_all[l], bs, 1)
            o, ss = _decode_hgrn(qs, lg, iv, sg, gn_g[ci], state_hgrn, ci, ss)
            hs = _outproj([o], w_out, hs, gt_s, bs, 1, last)

    y_prompt = hp.reshape(bp, tp, D_MODEL)
    y_sample = hs.reshape(bs, ts, D_MODEL)
    return (y_prompt, y_sample,
            kv_p[0].reshape(N_AC, bp, tp, H_B, 2 * HD_B), kv_p[1].reshape(N_AC, bp, tp, H_B, DV_B),
            jnp.stack(cp_l), jnp.stack(sp_l),
            kv_s[0].reshape(N_AC, bs, ts, H_B, 2 * HD_B), kv_s[1].reshape(N_AC, bs, ts, H_B, DV_B),
            jnp.swapaxes(cs, 1, 2), ss)
```

```python
import functools
import math

import numpy as np
import jax
import jax.numpy as jnp
from jax import lax
from jax.experimental import pallas as pl
from jax.experimental.pallas import tpu as pltpu

F32 = jnp.float32
BF16 = jnp.bfloat16

D_MODEL = 1024
DEPTH = 4
PAGE_SIZE = 128
N_AC = (DEPTH + 1) // 2
N_C = DEPTH // 2

H_B = 4
HD_B = 64
DV_B = 2 * HD_B
W_B = H_B * DV_B
QK_W = H_B * 2 * HD_B
ROT_DIM = HD_B // 4
ROPE_THETA = 500000.0
NEG_INF = -1e30

W_A = D_MODEL // 2
CONV_W = 31

H_C = 8
DK_C = D_MODEL // H_C
DV_C = D_MODEL // H_C
W_C = H_C * DV_C

EPS = 1e-6
LOG2_E = math.log2(math.e)
Q_SCALE = HD_B ** -0.5 * LOG2_E

AC_IN = 2 * QK_W + 2 * W_B + 3 * W_A
C_IN = 2 * H_C * DK_C + 2 * W_C

LANES = 128
SUBLANES = 8
MXU_WIDTH = 256
VMEM_LIMIT_BYTES = 56 * 1024 * 1024

ROW_TILE = 512
OUT_ROW_TILE = 1024
ATTN_TILE = 256
CONV_TILE = 512
CONV_HALO = 32
HGRN_CHUNK = 128
SAMPLE_STATE_TILE = 8


def _cparams(*sem):
    return pltpu.CompilerParams(dimension_semantics=sem, vmem_limit_bytes=VMEM_LIMIT_BYTES)


def _silu(x):
    return x * jax.nn.sigmoid(x)


def _dot(a, b):
    return jnp.dot(a, b, preferred_element_type=F32)


def _dot_nt(a, b):
    return lax.dot_general(a, b, (((1,), (1,)), ((), ())), preferred_element_type=F32)


def _dot_tn(a, b):
    return lax.dot_general(a, b, (((0,), (0,)), ((), ())), preferred_element_type=F32)


def _rms(x, g):
    return x * lax.rsqrt(jnp.mean(x * x, axis=-1, keepdims=True) + EPS) * g


def _ada_kernel(c_ref, w_ref, b_ref, o_ref):
    s = _silu(c_ref[...]).astype(BF16)
    o_ref[...] = _dot(s, w_ref[...].astype(BF16)) + b_ref[...]


def _ada_modulation(c_all, ada_w, ada_b):
    rows = c_all.shape[0]
    return pl.pallas_call(
        _ada_kernel,
        grid=(DEPTH, 3),
        in_specs=[
            pl.BlockSpec((rows, D_MODEL), lambda l, j: (0, 0)),
            pl.BlockSpec((None, D_MODEL, D_MODEL), lambda l, j: (l, 0, j)),
            pl.BlockSpec((None, 1, D_MODEL), lambda l, j: (l, 0, j)),
        ],
        out_specs=pl.BlockSpec((None, rows, D_MODEL), lambda l, j: (l, 0, j)),
        out_shape=jax.ShapeDtypeStruct((DEPTH, rows, 3 * D_MODEL), F32),
        compiler_params=_cparams("arbitrary", "arbitrary"),
        name="ada_modulation",
    )(c_all, ada_w, ada_b.reshape(DEPTH, 1, 3 * D_MODEL))


def _weight_spec(w):
    stacked, layer = w
    return pl.BlockSpec((None,) + stacked.shape[1:], lambda i: (layer, 0, 0))


def _modulated(x_ref, ng_ref, sc_ref, sh_ref):
    h = _rms(x_ref[...], ng_ref[...])
    return (h * (1.0 + sc_ref[...]) + sh_ref[...]).astype(BF16)


def _inproj_ac_kernel(x_ref, ng_ref, sc_ref, sh_ref, w_ref, cos_ref, sa_ref, sb_ref, *rest):
    q_ref, k_ref, kb_ref, v_ref, vb_ref, gb_ref, u_ref, ga_ref, h_scr = rest[-9:]
    h_scr[...] = _modulated(x_ref, ng_ref, sc_ref, sh_ref)

    def proj(s):
        return _dot(h_scr[...], w_ref[:, s * QK_W:(s + 1) * QK_W])

    cos, sa, sb = cos_ref[...], sa_ref[...], sb_ref[...]

    def rope(z):
        return (z * cos + pltpu.roll(z, LANES - ROT_DIM // 2, 1) * sa
                + pltpu.roll(z, ROT_DIM // 2, 1) * sb)

    zq = proj(0)
    for g in range(H_B):
        sl = slice(g * LANES, (g + 1) * LANES)
        q_ref[:, sl] = (rope(zq[:, sl]) * Q_SCALE).astype(BF16)
    tm = x_ref.shape[0]
    zk = proj(1)
    for g in range(H_B):
        sl = slice(g * LANES, (g + 1) * LANES)
        r = rope(zk[:, sl])
        k_ref[pl.ds(g, tm, stride=H_B), :] = r
        kb_ref[:, sl] = r.astype(BF16)
    zv = proj(2)
    for g in range(H_B):
        v_ref[pl.ds(g, tm, stride=H_B), :] = zv[:, g * LANES:(g + 1) * LANES]
    vb_ref[...] = zv.astype(BF16)
    gb_ref[...] = _silu(proj(3)).astype(BF16)
    a_val = proj(4)
    u_ref[...] = a_val * jax.nn.sigmoid(proj(5))
    ga_ref[...] = _silu(proj(6)).astype(BF16)


def _mod_specs(rows_per_mod, tm, tiles_per_mod):
    r = rows_per_mod
    return pl.BlockSpec((None, r, D_MODEL), lambda i: (i // tiles_per_mod, 0, 0))


def _inproj_ac(x, norm_g, scale, shift, w, cos_t, sa_t, sb_t, tm, tiles_per_mod, layer, kv_prev):
    m = x.shape[0]
    tab_tiles = cos_t.shape[0] // tm
    mod_spec = _mod_specs(scale.shape[1], tm, tiles_per_mod)
    row512 = pl.BlockSpec((tm, QK_W), lambda i: (i, 0))
    heads = pl.BlockSpec((None, H_B * tm, LANES), lambda i: (layer, i, 0))
    tab_spec = pl.BlockSpec((tm, LANES), lambda i: (i % tab_tiles, 0))
    out_shape = [
        jax.ShapeDtypeStruct((m, QK_W), BF16),
        jax.ShapeDtypeStruct((N_AC, H_B * m, LANES), F32),
        jax.ShapeDtypeStruct((m, QK_W), BF16),
        jax.ShapeDtypeStruct((N_AC, H_B * m, LANES), F32),
        jax.ShapeDtypeStruct((m, W_B), BF16),
        jax.ShapeDtypeStruct((m, W_B), BF16),
        jax.ShapeDtypeStruct((m, W_A), F32),
        jax.ShapeDtypeStruct((m, W_A), BF16),
    ]
    in_specs = [
        pl.BlockSpec((tm, D_MODEL), lambda i: (i, 0)),
        pl.BlockSpec((1, D_MODEL), lambda i: (0, 0)),
        mod_spec, mod_spec,
        _weight_spec(w),
        tab_spec, tab_spec, tab_spec,
    ]
    args = [x, norm_g.reshape(1, D_MODEL), scale, shift, w[0], cos_t, sa_t, sb_t]
    aliases = {}
    if kv_prev is not None:
        aliases = {len(args): 1, len(args) + 1: 3}
        in_specs += [pl.BlockSpec(memory_space=pl.ANY)] * 2
        args += list(kv_prev)
    return pl.pallas_call(
        _inproj_ac_kernel,
        grid=(m // tm,),
        in_specs=in_specs,
        out_specs=[row512, heads, row512, heads, row512, row512, row512, row512],
        out_shape=out_shape,
        input_output_aliases=aliases,
        scratch_shapes=[pltpu.VMEM((tm, D_MODEL), BF16)],
        compiler_params=_cparams("arbitrary"),
        name="inproj_conv_attn",
    )(*args)


def _inproj_c_kernel(x_ref, ng_ref, sc_ref, sh_ref, w_ref, lb_ref, q_ref, lg_ref, i_ref, sg_ref, h_scr):
    h_scr[...] = _modulated(x_ref, ng_ref, sc_ref, sh_ref)

    def proj(s):
        return _dot(h_scr[...], w_ref[:, s * W_C:(s + 1) * W_C])

    q_ref[...] = _silu(proj(0))
    lb = lb_ref[...]
    lg_ref[...] = jnp.log(lb + (1.0 - lb) * jax.nn.sigmoid(proj(1))) * LOG2_E
    i_ref[...] = proj(2)
    sg_ref[...] = _silu(proj(3)).astype(BF16)


def _inproj_c(x, norm_g, scale, shift, w, lb, tm, tiles_per_mod):
    m = x.shape[0]
    mod_spec = _mod_specs(scale.shape[1], tm, tiles_per_mod)
    row = pl.BlockSpec((tm, W_C), lambda i: (i, 0))
    out_shape = [
        jax.ShapeDtypeStruct((m, W_C), F32),
        jax.ShapeDtypeStruct((m, W_C), F32),
        jax.ShapeDtypeStruct((m, W_C), F32),
        jax.ShapeDtypeStruct((m, W_C), BF16),
    ]
    return pl.pallas_call(
        _inproj_c_kernel,
        grid=(m // tm,),
        in_specs=[
            pl.BlockSpec((tm, D_MODEL), lambda i: (i, 0)),
            pl.BlockSpec((1, D_MODEL), lambda i: (0, 0)),
            mod_spec, mod_spec,
            _weight_spec(w),
            pl.BlockSpec((1, W_C), lambda i: (0, 0)),
        ],
        out_specs=[row] * 4,
        out_shape=out_shape,
        scratch_shapes=[pltpu.VMEM((tm, D_MODEL), BF16)],
        compiler_params=_cparams("arbitrary"),
        name="inproj_hgrn",
    )(x, norm_g.reshape(1, D_MODEL), scale, shift, w[0], lb.reshape(1, W_C))


def _outproj_kernel(*refs, n_act, final_norm):
    acts = refs[:n_act]
    w_ref, x_ref, gate_ref = refs[n_act:n_act + 3]
    o_ref = refs[-1]
    acc = None
    lo = 0
    for a_ref in acts:
        width = a_ref.shape[-1]
        part = _dot(a_ref[...], w_ref[lo:lo + width, :])
        acc = part if acc is None else acc + part
        lo += width
    y = x_ref[...] + gate_ref[...] * acc
    if final_norm:
        y = _rms(y, refs[n_act + 3][...])
    o_ref[...] = y


def _outproj(acts, w, x, gate, tm, tiles_per_mod, final_g=None):
    m = x.shape[0]
    mod_spec = _mod_specs(gate.shape[1], tm, tiles_per_mod)
    in_specs = [pl.BlockSpec((tm, a.shape[1]), lambda i: (i, 0)) for a in acts]
    in_specs += [
        _weight_spec(w),
        pl.BlockSpec((tm, D_MODEL), lambda i: (i, 0)),
        mod_spec,
    ]
    args = list(acts) + [w[0], x, gate]
    if final_g is not None:
        in_specs.append(pl.BlockSpec((1, D_MODEL), lambda i: (0, 0)))
        args.append(final_g.reshape(1, D_MODEL))
    return pl.pallas_call(
        functools.partial(_outproj_kernel, n_act=len(acts), final_norm=final_g is not None),
        grid=(m // tm,),
        in_specs=in_specs,
        out_specs=pl.BlockSpec((tm, D_MODEL), lambda i: (i, 0)),
        out_shape=jax.ShapeDtypeStruct((m, D_MODEL), F32),
        compiler_params=_cparams("arbitrary"),
        name="outproj",
    )(*args)


def _attn_kernel(pt_ref, lam_ref, sgc_ref, sgr_ref, q_ref, k_ref, v_ref, gb_ref,
                 qs_ref, kn_ref, vn_ref, gbs_ref, *rest, tile, lam_init, n_dec, n_pages):
    pages = rest[:2 * n_dec * n_pages]
    o_ref, os_ref, q2_scr, s_a, s_b, m_scr, l_scr, acc_scr = rest[2 * n_dec * n_pages:]
    sg_ref = sgc_ref
    qi = pl.program_id(1)
    heads = [slice(h * LANES, (h + 1) * LANES) for h in range(H_B)]
    lane = lax.broadcasted_iota(jnp.int32, (tile, LANES), 1)
    for h, sl in enumerate(heads):
        q = q_ref[:, sl]
        zero = jnp.zeros_like(q)
        q2_scr[h] = jnp.concatenate([jnp.where(lane < HD_B, q, zero), jnp.where(lane >= HD_B, q, zero)], axis=0)
    m_scr[...] = jnp.full(m_scr.shape, NEG_INF, F32)
    l_scr[...] = jnp.zeros(l_scr.shape, F32)
    acc_scr[...] = jnp.zeros(acc_scr.shape, F32)

    def scores(t, s_scr):
        start = pl.multiple_of(t * tile, tile)
        for h, sl in enumerate(heads):
            s_scr[h] = _dot_nt(k_ref[pl.ds(start, tile), sl], q2_scr[h])

    def softmax_part(s_scr, mask):
        parts = []
        for h in range(H_B):
            sh = s_scr[h]
            if mask is not None:
                sh = jnp.where(mask, sh, NEG_INF)
            m_old = m_scr[h]
            m_new = jnp.maximum(m_old, jnp.max(sh, axis=0, keepdims=True))
            alpha = jnp.exp2(m_old - m_new)
            p = jnp.exp2(sh - m_new)
            m_scr[h] = m_new
            l_scr[h] = alpha * l_scr[h] + jnp.sum(p, axis=0, keepdims=True)
            parts.append((alpha, p.astype(BF16)))
        return parts

    def value_part(t, parts):
        start = pl.multiple_of(t * tile, tile)
        for h, sl in enumerate(heads):
            alpha, p = parts[h]
            acc_scr[h] = alpha * acc_scr[h] + _dot_tn(v_ref[pl.ds(start, tile), sl], p)

    def consume(s_scr, t, mask):
        value_part(t, softmax_part(s_scr, mask))

    odd = qi % 2

    @pl.when(odd == 1)
    def _():
        scores(0, s_b)
        scores(1, s_a)
        consume(s_b, 0, None)

    @pl.when(odd == 0)
    def _():
        scores(0, s_a)

    def body(jj, carry):
        t = odd + 2 * jj
        scores(t + 1, s_b)
        consume(s_a, t, None)
        scores(t + 2, s_a)
        consume(s_b, t + 1, None)
        return carry

    lax.fori_loop(0, qi // 2, body, 0)
    key = lax.broadcasted_iota(jnp.int32, (tile, 2 * tile), 0)
    qry = lax.broadcasted_iota(jnp.int32, (tile, 2 * tile), 1)
    diag = softmax_part(s_a, key <= jnp.where(qry >= tile, qry - tile, qry))
    lam = lam_ref[:, 0:1]
    dec = []
    for r in range(n_dec):
        k_pages = pages[2 * r * n_pages:(2 * r + 1) * n_pages]
        v_pages = pages[(2 * r + 1) * n_pages:(2 * r + 2) * n_pages]
        dec.append(_decode_scores(qs_ref[r].astype(F32), kn_ref[r], k_pages) + (v_pages,))
    value_part(qi, diag)
    for r, (s, s_new, v_pages) in enumerate(dec):
        os_ref[r] = _decode_finish(s, s_new, lam, vn_ref[r], v_pages, sgr_ref[...],
                                   gbs_ref[r].astype(F32), lam_init).astype(BF16)
    for h, sl in enumerate(heads):
        l, acc = l_scr[h], acc_scr[h]
        o = acc[:, :tile] / l[:, :tile] - lam_ref[:, 0:1] * (acc[:, tile:] / l[:, tile:])
        y = o * lax.rsqrt(jnp.mean(o * o, axis=0, keepdims=True) + EPS) * sg_ref[...]
        o_ref[:, sl] = (y.T * (1.0 - lam_init) * gb_ref[:, sl].astype(F32)).astype(BF16)


def _attention(q, kb, vb, gb, batch, seq, qs, kn_all, vn_all, gbs, cache_k4, cache_v4, layer, page_table,
               lam, subln_g, lam_init):
    tile = ATTN_TILE
    nq = seq // tile
    n_seq, n_pages = page_table.shape
    n_dec = n_seq // (batch * nq)
    assert n_dec * batch * nq == n_seq
    q3, k3, v3, g3 = (a.reshape(batch, seq, QK_W) for a in (q, kb, vb, gb))
    qspec = pl.BlockSpec((None, tile, QK_W), lambda b, i, pt: (b, i, 0))
    kvspec = pl.BlockSpec((None, seq, QK_W), lambda b, i, pt: (b, 0, 0))
    rows = pl.BlockSpec((n_dec, 1, QK_W), lambda b, i, pt: (b * nq + i, 0, 0))
    new_kv = pl.BlockSpec((None, n_dec, H_B, LANES), lambda b, i, pt: (layer, b * nq + i, 0, 0))

    def page_spec(r, j):
        return pl.BlockSpec((None, None, H_B * PAGE_SIZE, LANES),
                            lambda b, i, pt: (layer, pt[(b * nq + i) * n_dec + r, j], 0, 0))

    page_specs, page_args = [], []
    for r in range(n_dec):
        for cache in (cache_k4, cache_v4):
            page_specs += [page_spec(r, j) for j in range(n_pages)]
            page_args += [cache] * n_pages
    grid_spec = pltpu.PrefetchScalarGridSpec(
        num_scalar_prefetch=1,
        grid=(batch, nq),
        in_specs=[pl.BlockSpec((1, LANES), lambda b, i, pt: (0, 0)),
                  pl.BlockSpec((DV_B, 1), lambda b, i, pt: (0, 0)),
                  pl.BlockSpec((1, DV_B), lambda b, i, pt: (0, 0)),
                  qspec, kvspec, kvspec, qspec, rows, new_kv, new_kv, rows] + page_specs,
        out_specs=[qspec, rows],
        scratch_shapes=[pltpu.VMEM((H_B, 2 * tile, LANES), BF16),
                        pltpu.VMEM((H_B, tile, 2 * tile), F32),
                        pltpu.VMEM((H_B, tile, 2 * tile), F32),
                        pltpu.VMEM((H_B, 1, 2 * tile), F32),
                        pltpu.VMEM((H_B, 1, 2 * tile), F32),
                        pltpu.VMEM((H_B, DV_B, 2 * tile), F32)],
    )
    r3 = lambda a: a.reshape(n_seq, 1, QK_W)
    h4 = lambda a: a.reshape(N_AC, n_seq, H_B, LANES)
    out, out_s = pl.pallas_call(
        functools.partial(_attn_kernel, tile=tile, lam_init=lam_init, n_dec=n_dec, n_pages=n_pages),
        grid_spec=grid_spec,
        out_shape=[jax.ShapeDtypeStruct((batch, seq, W_B), BF16),
                   jax.ShapeDtypeStruct((n_seq, 1, W_B), BF16)],
        compiler_params=_cparams("arbitrary", "arbitrary"),
        name="diff_attention",
    )(page_table, lam, subln_g.reshape(DV_B, 1), subln_g.reshape(1, DV_B), q3, k3, v3, g3,
      r3(qs), h4(kn_all), h4(vn_all), r3(gbs), *page_args)
    return out.reshape(batch * seq, W_B), out_s.reshape(n_seq, W_B)


def _conv_post(y, ln_g, ln_b, ga):
    yc = y - jnp.mean(y, axis=-1, keepdims=True)
    yn = yc * lax.rsqrt(jnp.mean(yc * yc, axis=-1, keepdims=True) + EPS) * ln_g + ln_b
    return _silu(yn) * ga


def _conv_kernel(u_ref, ga_ref, w_ref, cb_ref, lg_ref, lb_ref, y_ref, st_ref, ext_scr, sh_scr, *, tile, rows):
    t = pl.program_id(1)
    first = CONV_HALO - (CONV_W - 1)
    sh_rows = sh_scr.shape[1]

    @pl.when(t == 0)
    def _():
        ext_scr[0:CONV_HALO, :] = jnp.zeros((CONV_HALO, W_A), F32)

    @pl.when(t > 0)
    def _():
        ext_scr[0:CONV_HALO, :] = ext_scr[tile:tile + CONV_HALO, :]

    ext_scr[CONV_HALO:CONV_HALO + tile, :] = u_ref[...]
    for r in range(1, SUBLANES):
        sh_scr[r - 1] = ext_scr[r:r + sh_rows, :]
    for r0 in range(0, tile, rows):
        acc = jnp.zeros((rows, W_A), F32) + cb_ref[...]
        for j in range(CONV_W):
            off = first + j
            r, base = off % SUBLANES, off - off % SUBLANES + r0
            src = ext_scr[base:base + rows, :] if r == 0 else sh_scr[r - 1, base:base + rows, :]
            acc = acc + src * w_ref[j:j + 1, :]
        y_ref[r0:r0 + rows, :] = _conv_post(acc, lg_ref[...], lb_ref[...],
                                            ga_ref[r0:r0 + rows, :].astype(F32)).astype(BF16)

    @pl.when(t == pl.num_programs(1) - 1)
    def _():
        st_ref[...] = ext_scr[CONV_HALO + tile - (CONV_W - 1):CONV_HALO + tile, :]


def _prompt_conv(u, ga, conv_w, conv_b, ln_g, ln_b, batch, seq):
    tile = CONV_TILE
    u3, g3 = u.reshape(batch, seq, W_A), ga.reshape(batch, seq, W_A)
    tspec = pl.BlockSpec((None, tile, W_A), lambda b, t: (b, t, 0))
    vec = pl.BlockSpec((1, W_A), lambda b, t: (0, 0))
    y, st = pl.pallas_call(
        functools.partial(_conv_kernel, tile=tile, rows=64),
        grid=(batch, seq // tile),
        in_specs=[tspec, tspec, pl.BlockSpec((CONV_W, W_A), lambda b, t: (0, 0)), vec, vec, vec],
        out_specs=[tspec, pl.BlockSpec((None, CONV_W - 1, W_A), lambda b, t: (b, 0, 0))],
        out_shape=[jax.ShapeDtypeStruct((batch, seq, W_A), BF16),
                   jax.ShapeDtypeStruct((batch, CONV_W - 1, W_A), F32)],
        scratch_shapes=[pltpu.VMEM((CONV_HALO + tile, W_A), F32),
                        pltpu.VMEM((SUBLANES - 1, CONV_HALO + tile - SUBLANES, W_A), F32)],
        compiler_params=_cparams("arbitrary", "arbitrary"),
        name="prompt_conv",
    )(u3, g3, conv_w, conv_b.reshape(1, W_A), ln_g.reshape(1, W_A), ln_b.reshape(1, W_A))
    return y.reshape(batch * seq, W_A), st


def _hgrn_tables(chunk):
    levels = int(math.log2(chunk))
    t = np.arange(chunk)[:, None]
    u = np.arange(chunk)[None, :]
    masks = []
    for v in range(levels):
        m = 1 << v
        masks.append((t // (2 * m) == u // (2 * m)) & (t % (2 * m) >= m) & (u % (2 * m) < m))
    return jnp.asarray(u <= t, BF16), jnp.asarray(np.stack(masks), BF16), levels


def _split3(x):
    hi = x.astype(BF16)
    r = x - hi.astype(F32)
    mid = r.astype(BF16)
    lo = (r - mid.astype(F32)).astype(BF16)
    return hi, mid, lo


def _hgrn_level_exponent(b, g2, m, row):
    chunk, width = b.shape
    if m == 1:
        return jnp.where((row & 1) != 0, g2, 0.0)
    if 2 * m < SUBLANES:
        b3 = b.reshape(chunk // SUBLANES, SUBLANES, width)
        sub = lax.broadcasted_iota(jnp.int32, b3.shape, 1)
        bm = b3[:, m - 1:m, :]
        for blk in range(1, SUBLANES // (2 * m)):
            bm = jnp.where(sub < blk * 2 * m, bm, b3[:, blk * 2 * m + m - 1:blk * 2 * m + m, :])
    else:
        b3 = b.reshape(chunk // (2 * m), 2 * m, width)
        bm = b3[:, m - 1:m, :]
    d = lax.bitcast_convert_type((b3 - bm).reshape(chunk, width), jnp.uint32)
    return lax.bitcast_convert_type(d | jnp.uint32(0x80000000), F32)


def _hgrn_chunk_stages(q_ref, lg_ref, i_ref, sg_ref, tri_ref, msk_ref, gn_ref, o_ref, st_scr, chunk, levels):
    heads = [slice(h * LANES, (h + 1) * LANES) for h in range(H_C)]
    g2 = lg_ref[...]
    q = q_ref[...]
    v = i_ref[...]
    k = 1.0 - jnp.exp2(g2)
    b3 = _dot(tri_ref[...], jnp.concatenate(_split3(g2), axis=1))
    yield
    b = b3[:, :W_C] + b3[:, W_C:2 * W_C] + b3[:, 2 * W_C:]
    row = lax.broadcasted_iota(jnp.int32, (chunk, W_C), 0)
    qb, kb = q.astype(BF16), k.astype(BF16)
    a = [jnp.zeros((chunk, chunk), BF16)] * H_C
    for lv in range(levels):
        m = 1 << lv
        x = jnp.where((row & m) != 0, qb, kb) * jnp.exp2(_hgrn_level_exponent(b, g2, m, row).astype(BF16))
        msk = msk_ref[lv]
        a = [a[h] + msk * _dot_nt(x[:, sl], x[:, sl]).astype(BF16) for h, sl in enumerate(heads)]
        yield
    vb = v.astype(BF16)
    qe = (q * jnp.exp2(b)).astype(BF16)
    b_last = b[chunk - 1:chunk, :]
    kd = (k * jnp.exp2(b_last - b)).astype(BF16)
    decay = jnp.exp2(b_last)
    qk = q * k
    for h, sl in enumerate(heads):
        st = st_scr[h]
        o = _dot(a[h], vb[:, sl]) + jnp.sum(qk[:, sl], axis=1, keepdims=True) * v[:, sl]
        o = o + _dot_nt(qe[:, sl], st.astype(BF16))
        st_scr[h] = st * decay[:, sl] + _dot_tn(vb[:, sl], kd[:, sl])
        o_ref[:, sl] = (_rms(o, gn_ref[...]) * sg_ref[:, sl].astype(F32)).astype(BF16)


def _inproj_hgrn_kernel(x_ref, ng_ref, sc_ref, sh_ref, w_ref, lb_ref, tri_ref, msk_ref, gn_ref,
                        o_ref, s_ref, h_scr, q_scr, lg_scr, iv_scr, sg_scr, st_scr,
                        *, chunk, levels, tiles_per_batch):
    i = pl.program_id(0)
    slot = i % 2
    prev = 1 - slot
    tm = x_ref.shape[0]
    assert tm // chunk == 4

    @pl.when(i == 0)
    def _():
        for scr in (q_scr, lg_scr, iv_scr, sg_scr):
            scr[1] = jnp.zeros(scr.shape[1:], scr.dtype)

    @pl.when((i + tiles_per_batch - 1) % tiles_per_batch == 0)
    def _():
        st_scr[...] = jnp.zeros_like(st_scr)

    h_scr[...] = _modulated(x_ref, ng_ref, sc_ref, sh_ref)
    lb = lb_ref[...]

    sections = [
        (q_scr, lambda z, cols: _silu(z)),
        (lg_scr, lambda z, cols: jnp.log(lb[:, cols] + (1.0 - lb[:, cols]) * jax.nn.sigmoid(z)) * LOG2_E),
        (iv_scr, lambda z, cols: z),
        (sg_scr, lambda z, cols: _silu(z).astype(BF16)),
    ]
    n_pieces = W_C // MXU_WIDTH
    for s, (scr, post) in enumerate(sections):
        rows = pl.ds(s * chunk, chunk)
        stages = _hgrn_chunk_stages(q_scr.at[prev, rows], lg_scr.at[prev, rows], iv_scr.at[prev, rows],
                                    sg_scr.at[prev, rows], tri_ref, msk_ref, gn_ref, o_ref.at[rows],
                                    st_scr, chunk, levels)
        for p in range(n_pieces):
            cols = slice(p * MXU_WIDTH, (p + 1) * MXU_WIDTH)
            z = _dot(h_scr[...], w_ref[:, s * W_C + p * MXU_WIDTH:s * W_C + (p + 1) * MXU_WIDTH])
            scr[slot, :, cols] = post(z, cols)
            next(stages, None)
            next(stages, None)
        for _ in stages:
            pass

    @pl.when(jnp.logical_and(i > 0, i % tiles_per_batch == 0))
    def _():
        for h in range(H_C):
            s_ref[h] = st_scr[h].T


def _inproj_hgrn(x, norm_g, scale, shift, w, lb, gn_g, batch, seq):
    tm, chunk = ROW_TILE, HGRN_CHUNK
    m = x.shape[0]
    nt = m // tm
    tiles_per_batch = seq // tm
    tri, masks, levels = _hgrn_tables(chunk)
    cur = lambda i: jnp.minimum(i, nt - 1)
    done = lambda i: jnp.maximum(i - 1, 0)
    mod_spec = pl.BlockSpec((None, 1, D_MODEL), lambda i: (cur(i) // tiles_per_batch, 0, 0))
    const = lambda shape: pl.BlockSpec(shape, lambda i: (0,) * len(shape))
    o, s = pl.pallas_call(
        functools.partial(_inproj_hgrn_kernel, chunk=chunk, levels=levels, tiles_per_batch=tiles_per_batch),
        grid=(nt + 1,),
        in_specs=[
            pl.BlockSpec((tm, D_MODEL), lambda i: (cur(i), 0)),
            const((1, D_MODEL)),
            mod_spec, mod_spec,
            _weight_spec(w),
            const((1, W_C)),
            const(tri.shape), const(masks.shape), const((1, DV_C)),
        ],
        out_specs=[pl.BlockSpec((tm, W_C), lambda i: (done(i), 0)),
                   pl.BlockSpec((None, H_C, DK_C, DV_C), lambda i: (done(i) // tiles_per_batch, 0, 0, 0))],
        out_shape=[jax.ShapeDtypeStruct((m, W_C), BF16),
                   jax.ShapeDtypeStruct((batch, H_C, DK_C, DV_C), F32)],
        scratch_shapes=[pltpu.VMEM((tm, D_MODEL), BF16),
                        pltpu.VMEM((2, tm, W_C), F32),
                        pltpu.VMEM((2, tm, W_C), F32),
                        pltpu.VMEM((2, tm, W_C), F32),
                        pltpu.VMEM((2, tm, W_C), BF16),
                        pltpu.VMEM((H_C, DV_C, DK_C), F32)],
        compiler_params=_cparams("arbitrary"),
        name="inproj_hgrn_recurrence",
    )(x, norm_g.reshape(1, D_MODEL), scale, shift, w[0], lb.reshape(1, W_C), tri, masks, gn_g.reshape(1, DV_C))
    return o, s


_DEC_ROWS = 2 * H_B


def _decode_head_rows(pieces):
    row = lax.broadcasted_iota(jnp.int32, (_DEC_ROWS, LANES), 0)
    out = jnp.zeros((_DEC_ROWS, LANES), F32)
    for h in range(H_B):
        out = jnp.where(row // 2 == h, jnp.broadcast_to(pieces[h], (_DEC_ROWS, LANES)), out)
    return out


def _decode_scores(q, kn, k_pages):
    row = lax.broadcasted_iota(jnp.int32, (_DEC_ROWS, LANES), 0)
    lane = lax.broadcasted_iota(jnp.int32, (_DEC_ROWS, LANES), 1)
    qm32 = jnp.where(lane // HD_B == row % 2,
                     _decode_head_rows([q[:, h * LANES:(h + 1) * LANES] for h in range(H_B)]), 0.0)
    qm = qm32.astype(BF16)
    s = jnp.concatenate([_dot_nt(qm, kp[...].astype(BF16)) for kp in k_pages], axis=1)
    key_head = lax.broadcasted_iota(jnp.int32, s.shape, 1) % H_B
    s = jnp.where(key_head == lax.broadcasted_iota(jnp.int32, s.shape, 0) // 2, s, NEG_INF)
    s_new = jnp.sum(qm32 * _decode_head_rows([kn[h:h + 1, :] for h in range(H_B)]), axis=-1, keepdims=True)
    return s, s_new


def _decode_finish(s, s_new, lam, vn, v_pages, subln_row, gb, lam_init):
    page_rows = H_B * PAGE_SIZE
    m = jnp.maximum(jnp.max(s, axis=-1, keepdims=True), s_new)
    p = jnp.exp2(s - m)
    p_new = jnp.exp2(s_new - m)
    l = jnp.sum(p, axis=-1, keepdims=True) + p_new
    comp = lax.broadcasted_iota(jnp.int32, (_DEC_ROWS, 1), 0) % 2
    wgt = jnp.where(comp == 0, 1.0, -lam) / l
    pw = (p * wgt).astype(BF16)
    o8 = (p_new * wgt) * _decode_head_rows([vn[h:h + 1, :] for h in range(H_B)])
    for j, vp in enumerate(v_pages):
        o8 = o8 + _dot(pw[:, j * page_rows:(j + 1) * page_rows], vp[...].astype(BF16))
    out = []
    for h in range(H_B):
        o = o8[2 * h:2 * h + 1, :] + o8[2 * h + 1:2 * h + 2, :]
        out.append(_rms(o, subln_row) * (1.0 - lam_init) * gb[:, h * DV_B:(h + 1) * DV_B])
    return jnp.concatenate(out, axis=1)


def _decode_conv_kernel(st_ref, u_ref, ga_ref, w_ref, cb_ref, lg_ref, lb_ref, *rest):
    y_ref, ns_ref = rest[-2:]
    u = u_ref[...]
    y = u * w_ref[CONV_W - 1:CONV_W, :] + cb_ref[...]
    for j in range(CONV_W - 1):
        y = y + st_ref[j] * w_ref[j:j + 1, :]
        if j > 0:
            ns_ref[j - 1] = st_ref[j]
    ns_ref[CONV_W - 2] = u
    y_ref[...] = _conv_post(y, lg_ref[...], lb_ref[...], ga_ref[...].astype(F32)).astype(BF16)


def _decode_conv(state_t, layer, u, ga, conv_w, conv_b, ln_g, ln_b, new_prev):
    n = u.shape[0]
    tile = 32
    vec = pl.BlockSpec((1, W_A), lambda i: (0, 0))
    sspec = pl.BlockSpec((None, CONV_W - 1, tile, W_A), lambda i: (layer, 0, i, 0))
    rspec = pl.BlockSpec((tile, W_A), lambda i: (i, 0))
    in_specs = [sspec, rspec, rspec, pl.BlockSpec((CONV_W, W_A), lambda i: (0, 0)), vec, vec, vec]
    args = [state_t, u, ga, conv_w, conv_b.reshape(1, W_A), ln_g.reshape(1, W_A), ln_b.reshape(1, W_A)]
    aliases = {}
    if new_prev is not None:
        aliases = {len(args): 1}
        in_specs.append(pl.BlockSpec(memory_space=pl.ANY))
        args.append(new_prev)
    return pl.pallas_call(
        _decode_conv_kernel,
        grid=(n // tile,),
        in_specs=in_specs,
        out_specs=[rspec, sspec],
        out_shape=[jax.ShapeDtypeStruct((n, W_A), BF16),
                   jax.ShapeDtypeStruct((N_AC, CONV_W - 1, n, W_A), F32)],
        input_output_aliases=aliases,
        compiler_params=_cparams("arbitrary"),
        name="decode_conv",
    )(*args)


def _decode_spread_table():
    rows = np.arange(3 * 3 * H_C)
    cols = np.arange(3 * W_C)
    same_vec = (rows[:, None] // (3 * H_C)) == (cols[None, :] // W_C)
    same_head = (rows[:, None] % H_C) == ((cols[None, :] % W_C) // DV_C)
    return jnp.asarray(same_vec & same_head, BF16)


def _decode_hgrn_kernel(q_ref, lg_ref, i_ref, sg_ref, gn_ref, e_ref, s_ref, *rest, n_seq):
    o_ref, ns_ref = rest[-2:]

    def split_rows(x):
        return [p.astype(F32) for p in _split3(x)]

    def one(n):
        f8 = jnp.exp2(lg_ref[n])
        parts = jnp.concatenate(split_rows(f8) + split_rows(1.0 - f8) + split_rows(q_ref[n]), axis=0)
        spread = _dot(parts.T.astype(BF16), e_ref[...])
        v8 = i_ref[n]
        rows = []
        for h in range(H_C):
            f_b, k_b, q_b =(spread[:, j * W_C + h * DV_C:j * W_C + (h + 1) * DV_C] for j in range(3))
            s_new = f_b * s_ref[n, h] + k_b * v8[h:h + 1, :]
            ns_ref[n, h] = s_new
            rows.append(jnp.sum(q_b * s_new, axis=0, keepdims=True))
        o = jnp.concatenate(rows, axis=0)
        o_ref[n] = (_rms(o, gn_ref[...]) * sg_ref[n].astype(F32)).astype(BF16)

    per_trip = 4

    def body(group, carry):
        for r in range(per_trip):
            one(per_trip * group + r)
        return carry

    assert n_seq % per_trip == 0
    lax.fori_loop(0, n_seq // per_trip, body, 0)


def _decode_hgrn(qs, lg, iv, sg, gn_g, state_all, layer, new_prev):
    n = qs.shape[0]
    tile = SAMPLE_STATE_TILE
    hspec = pl.BlockSpec((tile, H_C, DV_C), lambda i: (i, 0, 0))
    sspec = pl.BlockSpec((None, tile, H_C, DK_C, DV_C), lambda i: (layer, i, 0, 0, 0))
    h3 = lambda a: a.reshape(n, H_C, DV_C)
    table = _decode_spread_table()
    in_specs = [hspec, hspec, hspec, hspec, pl.BlockSpec((1, DV_C), lambda i: (0, 0)),
                pl.BlockSpec(table.shape, lambda i: (0, 0)), sspec]
    args = [h3(qs), h3(lg), h3(iv), h3(sg), gn_g.reshape(1, DV_C), table, state_all]
    aliases = {}
    if new_prev is not None:
        aliases = {len(args): 1}
        in_specs.append(pl.BlockSpec(memory_space=pl.ANY))
        args.append(new_prev)
    o, ns = pl.pallas_call(
        functools.partial(_decode_hgrn_kernel, n_seq=tile),
        grid=(n // tile,),
        in_specs=in_specs,
        out_specs=[hspec, sspec],
        out_shape=[jax.ShapeDtypeStruct((n, H_C, DV_C), BF16),
                   jax.ShapeDtypeStruct((N_C, n, H_C, DK_C, DV_C), F32)],
        input_output_aliases=aliases,
        compiler_params=_cparams("arbitrary"),
        name="decode_hgrn",
    )(*args)
    return o.reshape(n, W_C), ns


def _rope_tables(pos, rows):
    half = ROT_DIM // 2
    inv_freq = ROPE_THETA ** (-jnp.arange(half, dtype=F32) / half)
    ang = pos.astype(F32)[:, None] * inv_freq[None, :]
    cos, sin = jnp.cos(ang), jnp.sin(ang)
    n = pos.shape[0]
    ones = jnp.ones((n, HD_B - ROT_DIM), F32)
    zeros = jnp.zeros((n, HD_B - ROT_DIM), F32)
    z8 = jnp.zeros((n, half), F32)
    cos_t = jnp.concatenate([cos, cos, ones], axis=1)
    sa_t = jnp.concatenate([-sin, z8, zeros], axis=1)
    sb_t = jnp.concatenate([z8, sin, zeros], axis=1)
    out = []
    for t in (cos_t, sa_t, sb_t):
        t = jnp.concatenate([t, t], axis=1)
        out.append(jnp.broadcast_to(t, (rows, LANES)) if n == 1 else t)
    return out


def kernel(x_prompt, x_sample, c_prompt, c_sample, cache_k, cache_v, page_table, state_conv, state_hgrn,
           norm_g, ada_w, ada_b, w_in_ac, w_out_ac, conv_w, conv_b, ln_g, ln_b, lam_q1, lam_k1, lam_q2,
           lam_k2, subln_g, w_in_c, w_out_c, gn_g, lb_logits, final_g):
    bp, tp = x_prompt.shape[:2]
    bs, ts = x_sample.shape[:2]
    assert ts == 1
    n_pool = cache_k.shape[1]
    n_past = page_table.shape[1] * PAGE_SIZE
    mp = bp * tp

    lb_sm = jax.nn.softmax(lb_logits.astype(F32), axis=0)
    lb_all = jnp.cumsum(lb_sm, axis=0) - lb_sm[0]

    mod = _ada_modulation(jnp.concatenate([c_prompt, c_sample], axis=0), ada_w, ada_b)

    def mods(l):
        out = []
        for j in range(3):
            m = mod[l, :, j * D_MODEL:(j + 1) * D_MODEL]
            out.append((m[:bp].reshape(bp, 1, D_MODEL), m[bp:].reshape(1, bs, D_MODEL)))
        return out

    tabs_p = _rope_tables(jnp.arange(tp), tp)
    tabs_s = _rope_tables(jnp.full((1,), n_past), bs)
    ck4 = cache_k.reshape(N_AC, n_pool, PAGE_SIZE * H_B, 2 * HD_B)
    cv4 = cache_v.reshape(N_AC, n_pool, PAGE_SIZE * H_B, DV_B)

    w_in_ac_b, w_out_ac_b, w_in_c_b, w_out_c_b = (w.astype(BF16) for w in (w_in_ac, w_out_ac, w_in_c, w_out_c))

    tiles_p = tp // ROW_TILE
    hp = x_prompt.reshape(mp, D_MODEL)
    hs = x_sample.reshape(bs, D_MODEL)
    state_conv_t = jnp.swapaxes(state_conv, 1, 2)
    cp_l, sp_l = [], []
    kv_p = kv_s = ss = cs = None
    for l in range(DEPTH):
        (sh_p, sh_s), (sc_p, sc_s), (gt_p, gt_s) = mods(l)
        last = final_g if l == DEPTH - 1 else None
        if l % 2 == 0:
            a = l // 2
            lam_init = 0.8 - 0.6 * math.exp(-0.3 * l)
            lam = (jnp.exp(jnp.sum(lam_q1[a].astype(F32) * lam_k1[a].astype(F32)))
                   - jnp.exp(jnp.sum(lam_q2[a].astype(F32) * lam_k2[a].astype(F32))) + lam_init)
            lam = jnp.full((1, LANES), lam, F32)
            w_in, w_out = (w_in_ac_b, a), (w_out_ac_b, a)
            q, k, kb, v, vb, gb, u, ga = _inproj_ac(hp, norm_g[l], sc_p, sh_p, w_in, *tabs_p,
                                                    ROW_TILE, tiles_p, a, kv_p)
            kv_p = (k, v)
            qs, k, _, v, _, gbs, us, gas = _inproj_ac(hs, norm_g[l], sc_s, sh_s, w_in, *tabs_s, bs, 1, a, kv_s)
            kv_s = (k, v)
            ob, obs = _attention(q, kb, vb, gb, bp, tp, qs, k, v, gbs, ck4, cv4, a, page_table,
                                 lam, subln_g[a], lam_init)
            y, cst = _prompt_conv(u, ga, conv_w[a], conv_b[a], ln_g[a], ln_b[a], bp, tp)
            hp = _outproj([ob, y], w_out, hp, gt_p, OUT_ROW_TILE, tp // OUT_ROW_TILE, last)
            cp_l.append(cst)
            y, cs = _decode_conv(state_conv_t, a, us, gas, conv_w[a], conv_b[a], ln_g[a], ln_b[a], cs)
            hs = _outproj([obs, y], w_out, hs, gt_s, bs, 1, last)
        else:
            ci = l // 2
            w_in, w_out = (w_in_c_b, ci), (w_out_c_b, ci)
            o, st = _inproj_hgrn(hp, norm_g[l], sc_p, sh_p, w_in, lb_all[l], gn_g[ci], bp, tp)
            hp = _outproj([o], w_out, hp, gt_p, OUT_ROW_TILE, tp // OUT_ROW_TILE, last)
            sp_l.append(st)
            qs, lg, iv, sg = _inproj_c(hs, norm_g[l], sc_s, sh_s, w_in, lb_all[l], bs, 1)
            o, ss = _decode_hgrn(qs, lg, iv, sg, gn_g[ci], state_hgrn, ci, ss)
            hs = _outproj([o], w_out, hs, gt_s, bs, 1, last)

    y_prompt = hp.reshape(bp, tp, D_MODEL)
    y_sample = hs.reshape(bs, ts, D_MODEL)
    return (y_prompt, y_sample,
            kv_p[0].reshape(N_AC, bp, tp, H_B, 2 * HD_B), kv_p[1].reshape(N_AC, bp, tp, H_B, DV_B),
            jnp.stack(cp_l), jnp.stack(sp_l),
            kv_s[0].reshape(N_AC, bs, ts, H_B, 2 * HD_B), kv_s[1].reshape(N_AC, bs, ts, H_B, DV_B),
            jnp.swapaxes(cs, 1, 2), ss)
```

```python
import functools
import math

import numpy as np
import jax
import jax.numpy as jnp
from jax import lax
from jax.experimental import pallas as pl
from jax.experimental.pallas import tpu as pltpu

F32 = jnp.float32
BF16 = jnp.bfloat16

D_MODEL = 1024
DEPTH = 4
PAGE_SIZE = 128
N_AC = (DEPTH + 1) // 2
N_C = DEPTH // 2

H_B = 4
HD_B = 64
DV_B = 2 * HD_B
W_B = H_B * DV_B
QK_W = H_B * 2 * HD_B
ROT_DIM = HD_B // 4
ROPE_THETA = 500000.0
NEG_INF = -1e30

W_A = D_MODEL // 2
CONV_W = 31

H_C = 8
DK_C = D_MODEL // H_C
DV_C = D_MODEL // H_C
W_C = H_C * DV_C

EPS = 1e-6
LOG2_E = math.log2(math.e)
Q_SCALE = HD_B ** -0.5 * LOG2_E

AC_IN = 2 * QK_W + 2 * W_B + 3 * W_A
C_IN = 2 * H_C * DK_C + 2 * W_C

LANES = 128
SUBLANES = 8
MXU_WIDTH = 256
VMEM_LIMIT_BYTES = 56 * 1024 * 1024

ROW_TILE = 512
OUT_ROW_TILE = 1024
ATTN_TILE = 256
CONV_TILE = 512
CONV_HALO = 32
HGRN_CHUNK = 128
SAMPLE_STATE_TILE = 8


def _cparams(*sem):
    return pltpu.CompilerParams(dimension_semantics=sem, vmem_limit_bytes=VMEM_LIMIT_BYTES)


def _silu(x):
    return x * jax.nn.sigmoid(x)


def _dot(a, b):
    return jnp.dot(a, b, preferred_element_type=F32)


def _dot_nt(a, b):
    return lax.dot_general(a, b, (((1,), (1,)), ((), ())), preferred_element_type=F32)


def _dot_tn(a, b):
    return lax.dot_general(a, b, (((0,), (0,)), ((), ())), preferred_element_type=F32)


def _rms(x, g):
    return x * lax.rsqrt(jnp.mean(x * x, axis=-1, keepdims=True) + EPS) * g


def _ada_kernel(c_ref, w_ref, b_ref, o_ref):
    s = _silu(c_ref[...]).astype(BF16)
    o_ref[...] = _dot(s, w_ref[...].astype(BF16)) + b_ref[...]


def _ada_modulation(c_all, ada_w, ada_b):
    rows = c_all.shape[0]
    return pl.pallas_call(
        _ada_kernel,
        grid=(DEPTH, 3),
        in_specs=[
            pl.BlockSpec((rows, D_MODEL), lambda l, j: (0, 0)),
            pl.BlockSpec((None, D_MODEL, D_MODEL), lambda l, j: (l, 0, j)),
            pl.BlockSpec((None, 1, D_MODEL), lambda l, j: (l, 0, j)),
        ],
        out_specs=pl.BlockSpec((None, rows, D_MODEL), lambda l, j: (l, 0, j)),
        out_shape=jax.ShapeDtypeStruct((DEPTH, rows, 3 * D_MODEL), F32),
        compiler_params=_cparams("arbitrary", "arbitrary"),
        name="ada_modulation",
    )(c_all, ada_w, ada_b.reshape(DEPTH, 1, 3 * D_MODEL))


def _weight_spec(w):
    stacked, layer = w
    return pl.BlockSpec((None,) + stacked.shape[1:], lambda i: (layer, 0, 0))


def _modulated(x_ref, ng_ref, sc_ref, sh_ref):
    h = _rms(x_ref[...], ng_ref[...])
    return (h * (1.0 + sc_ref[...]) + sh_ref[...]).astype(BF16)


def _inproj_ac_kernel(x_ref, ng_ref, sc_ref, sh_ref, w_ref, cos_ref, sa_ref, sb_ref, *rest):
    q_ref, k_ref, kb_ref, v_ref, vb_ref, gb_ref, u_ref, ga_ref, h_scr = rest[-9:]
    h_scr[...] = _modulated(x_ref, ng_ref, sc_ref, sh_ref)

    def proj(s):
        return _dot(h_scr[...], w_ref[:, s * QK_W:(s + 1) * QK_W])

    cos, sa, sb = cos_ref[...], sa_ref[...], sb_ref[...]

    def rope(z):
        return (z * cos + pltpu.roll(z, LANES - ROT_DIM // 2, 1) * sa
                + pltpu.roll(z, ROT_DIM // 2, 1) * sb)

    zq = proj(0)
    for g in range(H_B):
        sl = slice(g * LANES, (g + 1) * LANES)
        q_ref[:, sl] = (rope(zq[:, sl]) * Q_SCALE).astype(BF16)
    tm = x_ref.shape[0]
    zk = proj(1)
    for g in range(H_B):
        sl = slice(g * LANES, (g + 1) * LANES)
        r = rope(zk[:, sl])
        k_ref[pl.ds(g, tm, stride=H_B), :] = r
        kb_ref[:, sl] = r.astype(BF16)
    zv = proj(2)
    for g in range(H_B):
        v_ref[pl.ds(g, tm, stride=H_B), :] = zv[:, g * LANES:(g + 1) * LANES]
    vb_ref[...] = zv.astype(BF16)
    gb_ref[...] = _silu(proj(3)).astype(BF16)
    a_val = proj(4)
    u_ref[...] = a_val * jax.nn.sigmoid(proj(5))
    ga_ref[...] = _silu(proj(6)).astype(BF16)


def _mod_specs(rows_per_mod, tm, tiles_per_mod):
    r = rows_per_mod
    return pl.BlockSpec((None, r, D_MODEL), lambda i: (i // tiles_per_mod, 0, 0))


def _inproj_ac(x, norm_g, scale, shift, w, cos_t, sa_t, sb_t, tm, tiles_per_mod, layer, kv_prev):
    m = x.shape[0]
    tab_tiles = cos_t.shape[0] // tm
    mod_spec = _mod_specs(scale.shape[1], tm, tiles_per_mod)
    row512 = pl.BlockSpec((tm, QK_W), lambda i: (i, 0))
    heads = pl.BlockSpec((None, H_B * tm, LANES), lambda i: (layer, i, 0))
    tab_spec = pl.BlockSpec((tm, LANES), lambda i: (i % tab_tiles, 0))
    out_shape = [
        jax.ShapeDtypeStruct((m, QK_W), BF16),
        jax.ShapeDtypeStruct((N_AC, H_B * m, LANES), F32),
        jax.ShapeDtypeStruct((m, QK_W), BF16),
        jax.ShapeDtypeStruct((N_AC, H_B * m, LANES), F32),
        jax.ShapeDtypeStruct((m, W_B), BF16),
        jax.ShapeDtypeStruct((m, W_B), BF16),
        jax.ShapeDtypeStruct((m, W_A), F32),
        jax.ShapeDtypeStruct((m, W_A), BF16),
    ]
    in_specs = [
        pl.BlockSpec((tm, D_MODEL), lambda i: (i, 0)),
        pl.BlockSpec((1, D_MODEL), lambda i: (0, 0)),
        mod_spec, mod_spec,
        _weight_spec(w),
        tab_spec, tab_spec, tab_spec,
    ]
    args = [x, norm_g.reshape(1, D_MODEL), scale, shift, w[0], cos_t, sa_t, sb_t]
    aliases = {}
    if kv_prev is not None:
        aliases = {len(args): 1, len(args) + 1: 3}
        in_specs += [pl.BlockSpec(memory_space=pl.ANY)] * 2
        args += list(kv_prev)
    return pl.pallas_call(
        _inproj_ac_kernel,
        grid=(m // tm,),
        in_specs=in_specs,
        out_specs=[row512, heads, row512, heads, row512, row512, row512, row512],
        out_shape=out_shape,
        input_output_aliases=aliases,
        scratch_shapes=[pltpu.VMEM((tm, D_MODEL), BF16)],
        compiler_params=_cparams("arbitrary"),
        name="inproj_conv_attn",
    )(*args)


def _inproj_c_kernel(x_ref, ng_ref, sc_ref, sh_ref, w_ref, lb_ref, q_ref, lg_ref, i_ref, sg_ref, h_scr):
    h_scr[...] = _modulated(x_ref, ng_ref, sc_ref, sh_ref)

    def proj(s):
        return _dot(h_scr[...], w_ref[:, s * W_C:(s + 1) * W_C])

    q_ref[...] = _silu(proj(0))
    lb = lb_ref[...]
    lg_ref[...] = jnp.log(lb + (1.0 - lb) * jax.nn.sigmoid(proj(1))) * LOG2_E
    i_ref[...] = proj(2)
    sg_ref[...] = _silu(proj(3)).astype(BF16)


def _inproj_c(x, norm_g, scale, shift, w, lb, tm, tiles_per_mod):
    m = x.shape[0]
    mod_spec = _mod_specs(scale.shape[1], tm, tiles_per_mod)
    row = pl.BlockSpec((tm, W_C), lambda i: (i, 0))
    out_shape = [
        jax.ShapeDtypeStruct((m, W_C), F32),
        jax.ShapeDtypeStruct((m, W_C), F32),
        jax.ShapeDtypeStruct((m, W_C), F32),
        jax.ShapeDtypeStruct((m, W_C), BF16),
    ]
    return pl.pallas_call(
        _inproj_c_kernel,
        grid=(m // tm,),
        in_specs=[
            pl.BlockSpec((tm, D_MODEL), lambda i: (i, 0)),
            pl.BlockSpec((1, D_MODEL), lambda i: (0, 0)),
            mod_spec, mod_spec,
            _weight_spec(w),
            pl.BlockSpec((1, W_C), lambda i: (0, 0)),
        ],
        out_specs=[row] * 4,
        out_shape=out_shape,
        scratch_shapes=[pltpu.VMEM((tm, D_MODEL), BF16)],
        compiler_params=_cparams("arbitrary"),
        name="inproj_hgrn",
    )(x, norm_g.reshape(1, D_MODEL), scale, shift, w[0], lb.reshape(1, W_C))


def _outproj_kernel(*refs, n_act, final_norm):
    acts = refs[:n_act]
    w_ref, x_ref, gate_ref = refs[n_act:n_act + 3]
    o_ref = refs[-1]
    acc = None
    lo = 0
    for a_ref in acts:
        width = a_ref.shape[-1]
        part = _dot(a_ref[...], w_ref[lo:lo + width, :])
        acc = part if acc is None else acc + part
        lo += width
    y = x_ref[...] + gate_ref[...] * acc
    if final_norm:
        y = _rms(y, refs[n_act + 3][...])
    o_ref[...] = y


def _outproj(acts, w, x, gate, tm, tiles_per_mod, final_g=None):
    m = x.shape[0]
    mod_spec = _mod_specs(gate.shape[1], tm, tiles_per_mod)
    in_specs = [pl.BlockSpec((tm, a.shape[1]), lambda i: (i, 0)) for a in acts]
    in_specs += [
        _weight_spec(w),
        pl.BlockSpec((tm, D_MODEL), lambda i: (i, 0)),
        mod_spec,
    ]
    args = list(acts) + [w[0], x, gate]
    if final_g is not None:
        in_specs.append(pl.BlockSpec((1, D_MODEL), lambda i: (0, 0)))
        args.append(final_g.reshape(1, D_MODEL))
    return pl.pallas_call(
        functools.partial(_outproj_kernel, n_act=len(acts), final_norm=final_g is not None),
        grid=(m // tm,),
        in_specs=in_specs,
        out_specs=pl.BlockSpec((tm, D_MODEL), lambda i: (i, 0)),
        out_shape=jax.ShapeDtypeStruct((m, D_MODEL), F32),
        compiler_params=_cparams("arbitrary"),
        name="outproj",
    )(*args)


def _attn_kernel(pt_ref, lam_ref, sgc_ref, sgr_ref, q_ref, k_ref, v_ref, gb_ref,
                 qs_ref, kn_ref, vn_ref, gbs_ref, *rest, tile, lam_init, n_dec, n_pages):
    pages = rest[:2 * n_dec * n_pages]
    o_ref, os_ref, q2_scr, s_a, s_b, m_scr, l_scr, acc_scr = rest[2 * n_dec * n_pages:]
    sg_ref = sgc_ref
    qi = pl.program_id(1)
    heads = [slice(h * LANES, (h + 1) * LANES) for h in range(H_B)]
    lane = lax.broadcasted_iota(jnp.int32, (tile, LANES), 1)
    for h, sl in enumerate(heads):
        q = q_ref[:, sl]
        zero = jnp.zeros_like(q)
        q2_scr[h] = jnp.concatenate([jnp.where(lane < HD_B, q, zero), jnp.where(lane >= HD_B, q, zero)], axis=0)
    m_scr[...] = jnp.full(m_scr.shape, NEG_INF, F32)
    l_scr[...] = jnp.zeros(l_scr.shape, F32)
    acc_scr[...] = jnp.zeros(acc_scr.shape, F32)

    def scores(t, s_scr):
        start = pl.multiple_of(t * tile, tile)
        for h, sl in enumerate(heads):
            s_scr[h] = _dot_nt(k_ref[pl.ds(start, tile), sl], q2_scr[h])

    def softmax_part(s_scr, mask):
        parts = []
        for h in range(H_B):
            sh = s_scr[h]
            if mask is not None:
                sh = jnp.where(mask, sh, NEG_INF)
            m_old = m_scr[h]
            m_new = jnp.maximum(m_old, jnp.max(sh, axis=0, keepdims=True))
            alpha = jnp.exp2(m_old - m_new)
            p = jnp.exp2(sh - m_new)
            m_scr[h] = m_new
            l_scr[h] = alpha * l_scr[h] + jnp.sum(p, axis=0, keepdims=True)
            parts.append((alpha, p.astype(BF16)))
        return parts

    def value_part(t, parts):
        start = pl.multiple_of(t * tile, tile)
        for h, sl in enumerate(heads):
            alpha, p = parts[h]
            acc_scr[h] = alpha * acc_scr[h] + _dot_tn(v_ref[pl.ds(start, tile), sl], p)

    def consume(s_scr, t, mask):
        value_part(t, softmax_part(s_scr, mask))

    odd = qi % 2

    @pl.when(odd == 1)
    def _():
        scores(0, s_b)
        scores(1, s_a)
        consume(s_b, 0, None)

    @pl.when(odd == 0)
    def _():
        scores(0, s_a)

    def body(jj, carry):
        t = odd + 2 * jj
        scores(t + 1, s_b)
        consume(s_a, t, None)
        scores(t + 2, s_a)
        consume(s_b, t + 1, None)
        return carry

    lax.fori_loop(0, qi // 2, body, 0)
    key = lax.broadcasted_iota(jnp.int32, (tile, 2 * tile), 0)
    qry = lax.broadcasted_iota(jnp.int32, (tile, 2 * tile), 1)
    diag = softmax_part(s_a, key <= jnp.where(qry >= tile, qry - tile, qry))
    lam = lam_ref[:, 0:1]
    dec = []
    for r in range(n_dec):
        k_pages = pages[2 * r * n_pages:(2 * r + 1) * n_pages]
        v_pages = pages[(2 * r + 1) * n_pages:(2 * r + 2) * n_pages]
        dec.append(_decode_scores(qs_ref[r].astype(F32), kn_ref[r], k_pages) + (v_pages,))
    value_part(qi, diag)
    for r, (s, s_new, v_pages) in enumerate(dec):
        os_ref[r] = _decode_finish(s, s_new, lam, vn_ref[r], v_pages, sgr_ref[...],
                                   gbs_ref[r].astype(F32), lam_init).astype(BF16)
    for h, sl in enumerate(heads):
        l, acc = l_scr[h], acc_scr[h]
        o = acc[:, :tile] / l[:, :tile] - lam_ref[:, 0:1] * (acc[:, tile:] / l[:, tile:])
        y = o * lax.rsqrt(jnp.mean(o * o, axis=0, keepdims=True) + EPS) * sg_ref[...]
        o_ref[:, sl] = (y.T * (1.0 - lam_init) * gb_ref[:, sl].astype(F32)).astype(BF16)


def _attention(q, kb, vb, gb, batch, seq, qs, kn_all, vn_all, gbs, cache_k4, cache_v4, layer, page_table,
               lam, subln_g, lam_init):
    tile = ATTN_TILE
    nq = seq // tile
    n_seq, n_pages = page_table.shape
    n_dec = n_seq // (batch * nq)
    assert n_dec * batch * nq == n_seq
    q3, k3, v3, g3 = (a.reshape(batch, seq, QK_W) for a in (q, kb, vb, gb))
    qspec = pl.BlockSpec((None, tile, QK_W), lambda b, i, pt: (b, i, 0))
    kvspec = pl.BlockSpec((None, seq, QK_W), lambda b, i, pt: (b, 0, 0))
    rows = pl.BlockSpec((n_dec, 1, QK_W), lambda b, i, pt: (b * nq + i, 0, 0))
    new_kv = pl.BlockSpec((None, n_dec, H_B, LANES), lambda b, i, pt: (layer, b * nq + i, 0, 0))

    def page_spec(r, j):
        return pl.BlockSpec((None, None, H_B * PAGE_SIZE, LANES),
                            lambda b, i, pt: (layer, pt[(b * nq + i) * n_dec + r, j], 0, 0))

    page_specs, page_args = [], []
    for r in range(n_dec):
        for cache in (cache_k4, cache_v4):
            page_specs += [page_spec(r, j) for j in range(n_pages)]
            page_args += [cache] * n_pages
    grid_spec = pltpu.PrefetchScalarGridSpec(
        num_scalar_prefetch=1,
        grid=(batch, nq),
        in_specs=[pl.BlockSpec((1, LANES), lambda b, i, pt: (0, 0)),
                  pl.BlockSpec((DV_B, 1), lambda b, i, pt: (0, 0)),
                  pl.BlockSpec((1, DV_B), lambda b, i, pt: (0, 0)),
                  qspec, kvspec, kvspec, qspec, rows, new_kv, new_kv, rows] + page_specs,
        out_specs=[qspec, rows],
        scratch_shapes=[pltpu.VMEM((H_B, 2 * tile, LANES), BF16),
                        pltpu.VMEM((H_B, tile, 2 * tile), F32),
                        pltpu.VMEM((H_B, tile, 2 * tile), F32),
                        pltpu.VMEM((H_B, 1, 2 * tile), F32),
                        pltpu.VMEM((H_B, 1, 2 * tile), F32),
                        pltpu.VMEM((H_B, DV_B, 2 * tile), F32)],
    )
    r3 = lambda a: a.reshape(n_seq, 1, QK_W)
    h4 = lambda a: a.reshape(N_AC, n_seq, H_B, LANES)
    out, out_s = pl.pallas_call(
        functools.partial(_attn_kernel, tile=tile, lam_init=lam_init, n_dec=n_dec, n_pages=n_pages),
        grid_spec=grid_spec,
        out_shape=[jax.ShapeDtypeStruct((batch, seq, W_B), BF16),
                   jax.ShapeDtypeStruct((n_seq, 1, W_B), BF16)],
        compiler_params=_cparams("arbitrary", "arbitrary"),
        name="diff_attention",
    )(page_table, lam, subln_g.reshape(DV_B, 1), subln_g.reshape(1, DV_B), q3, k3, v3, g3,
      r3(qs), h4(kn_all), h4(vn_all), r3(gbs), *page_args)
    return out.reshape(batch * seq, W_B), out_s.reshape(n_seq, W_B)


def _conv_post(y, ln_g, ln_b, ga):
    yc = y - jnp.mean(y, axis=-1, keepdims=True)
    yn = yc * lax.rsqrt(jnp.mean(yc * yc, axis=-1, keepdims=True) + EPS) * ln_g + ln_b
    return _silu(yn) * ga


def _conv_kernel(u_ref, ga_ref, w_ref, cb_ref, lg_ref, lb_ref, y_ref, st_ref, ext_scr, sh_scr, *, tile, rows):
    t = pl.program_id(1)
    first = CONV_HALO - (CONV_W - 1)
    sh_rows = sh_scr.shape[1]

    @pl.when(t == 0)
    def _():
        ext_scr[0:CONV_HALO, :] = jnp.zeros((CONV_HALO, W_A), F32)

    @pl.when(t > 0)
    def _():
        ext_scr[0:CONV_HALO, :] = ext_scr[tile:tile + CONV_HALO, :]

    ext_scr[CONV_HALO:CONV_HALO + tile, :] = u_ref[...]
    for r in range(1, SUBLANES):
        sh_scr[r - 1] = ext_scr[r:r + sh_rows, :]
    for r0 in range(0, tile, rows):
        acc = jnp.zeros((rows, W_A), F32) + cb_ref[...]
        for j in range(CONV_W):
            off = first + j
            r, base = off % SUBLANES, off - off % SUBLANES + r0
            src = ext_scr[base:base + rows, :] if r == 0 else sh_scr[r - 1, base:base + rows, :]
            acc = acc + src * w_ref[j:j + 1, :]
        y_ref[r0:r0 + rows, :] = _conv_post(acc, lg_ref[...], lb_ref[...],
                                            ga_ref[r0:r0 + rows, :].astype(F32)).astype(BF16)

    @pl.when(t == pl.num_programs(1) - 1)
    def _():
        st_ref[...] = ext_scr[CONV_HALO + tile - (CONV_W - 1):CONV_HALO + tile, :]


def _prompt_conv(u, ga, conv_w, conv_b, ln_g, ln_b, batch, seq):
    tile = CONV_TILE
    u3, g3 = u.reshape(batch, seq, W_A), ga.reshape(batch, seq, W_A)
    tspec = pl.BlockSpec((None, tile, W_A), lambda b, t: (b, t, 0))
    vec = pl.BlockSpec((1, W_A), lambda b, t: (0, 0))
    y, st = pl.pallas_call(
        functools.partial(_conv_kernel, tile=tile, rows=32),
        grid=(batch, seq // tile),
        in_specs=[tspec, tspec, pl.BlockSpec((CONV_W, W_A), lambda b, t: (0, 0)), vec, vec, vec],
        out_specs=[tspec, pl.BlockSpec((None, CONV_W - 1, W_A), lambda b, t: (b, 0, 0))],
        out_shape=[jax.ShapeDtypeStruct((batch, seq, W_A), BF16),
                   jax.ShapeDtypeStruct((batch, CONV_W - 1, W_A), F32)],
        scratch_shapes=[pltpu.VMEM((CONV_HALO + tile, W_A), F32),
                        pltpu.VMEM((SUBLANES - 1, CONV_HALO + tile - SUBLANES, W_A), F32)],
        compiler_params=_cparams("arbitrary", "arbitrary"),
        name="prompt_conv",
    )(u3, g3, conv_w, conv_b.reshape(1, W_A), ln_g.reshape(1, W_A), ln_b.reshape(1, W_A))
    return y.reshape(batch * seq, W_A), st


def _hgrn_tables(chunk):
    levels = int(math.log2(chunk))
    t = np.arange(chunk)[:, None]
    u = np.arange(chunk)[None, :]
    masks = []
    for v in range(levels):
        m = 1 << v
        masks.append((t // (2 * m) == u // (2 * m)) & (t % (2 * m) >= m) & (u % (2 * m) < m))
    return jnp.asarray(u <= t, BF16), jnp.asarray(np.stack(masks), BF16), levels


def _split3(x):
    hi = x.astype(BF16)
    r = x - hi.astype(F32)
    mid = r.astype(BF16)
    lo = (r - mid.astype(F32)).astype(BF16)
    return hi, mid, lo


def _hgrn_level_exponent(b, g2, m, row):
    chunk, width = b.shape
    if m == 1:
        return jnp.where((row & 1) != 0, g2, 0.0)
    if 2 * m < SUBLANES:
        b3 = b.reshape(chunk // SUBLANES, SUBLANES, width)
        sub = lax.broadcasted_iota(jnp.int32, b3.shape, 1)
        bm = b3[:, m - 1:m, :]
        for blk in range(1, SUBLANES // (2 * m)):
            bm = jnp.where(sub < blk * 2 * m, bm, b3[:, blk * 2 * m + m - 1:blk * 2 * m + m, :])
    else:
        b3 = b.reshape(chunk // (2 * m), 2 * m, width)
        bm = b3[:, m - 1:m, :]
    d = lax.bitcast_convert_type((b3 - bm).reshape(chunk, width), jnp.uint32)
    return lax.bitcast_convert_type(d | jnp.uint32(0x80000000), F32)


def _hgrn_chunk_stages(q_ref, lg_ref, i_ref, sg_ref, tri_ref, msk_ref, gn_ref, o_ref, st_scr, chunk, levels):
    heads = [slice(h * LANES, (h + 1) * LANES) for h in range(H_C)]
    g2 = lg_ref[...]
    q = q_ref[...]
    v = i_ref[...]
    k = 1.0 - jnp.exp2(g2)
    b3 = _dot(tri_ref[...], jnp.concatenate(_split3(g2), axis=1))
    yield
    b = b3[:, :W_C] + b3[:, W_C:2 * W_C] + b3[:, 2 * W_C:]
    row = lax.broadcasted_iota(jnp.int32, (chunk, W_C), 0)
    qb, kb = q.astype(BF16), k.astype(BF16)
    a = [jnp.zeros((chunk, chunk), BF16)] * H_C
    for lv in range(levels):
        m = 1 << lv
        x = jnp.where((row & m) != 0, qb, kb) * jnp.exp2(_hgrn_level_exponent(b, g2, m, row).astype(BF16))
        msk = msk_ref[lv]
        a = [a[h] + msk * _dot_nt(x[:, sl], x[:, sl]).astype(BF16) for h, sl in enumerate(heads)]
        yield
    vb = v.astype(BF16)
    qe = (q * jnp.exp2(b)).astype(BF16)
    b_last = b[chunk - 1:chunk, :]
    kd = (k * jnp.exp2(b_last - b)).astype(BF16)
    decay = jnp.exp2(b_last)
    qk = q * k
    for h, sl in enumerate(heads):
        st = st_scr[h]
        o = _dot(a[h], vb[:, sl]) + jnp.sum(qk[:, sl], axis=1, keepdims=True) * v[:, sl]
        o = o + _dot_nt(qe[:, sl], st.astype(BF16))
        st_scr[h] = st * decay[:, sl] + _dot_tn(vb[:, sl], kd[:, sl])
        o_ref[:, sl] = (_rms(o, gn_ref[...]) * sg_ref[:, sl].astype(F32)).astype(BF16)


def _inproj_hgrn_kernel(x_ref, ng_ref, sc_ref, sh_ref, w_ref, lb_ref, tri_ref, msk_ref, gn_ref,
                        o_ref, s_ref, h_scr, q_scr, lg_scr, iv_scr, sg_scr, st_scr,
                        *, chunk, levels, tiles_per_batch):
    i = pl.program_id(0)
    slot = i % 2
    prev = 1 - slot
    tm = x_ref.shape[0]
    assert tm // chunk == 4

    @pl.when(i == 0)
    def _():
        for scr in (q_scr, lg_scr, iv_scr, sg_scr):
            scr[1] = jnp.zeros(scr.shape[1:], scr.dtype)

    @pl.when((i + tiles_per_batch - 1) % tiles_per_batch == 0)
    def _():
        st_scr[...] = jnp.zeros_like(st_scr)

    h_scr[...] = _modulated(x_ref, ng_ref, sc_ref, sh_ref)
    lb = lb_ref[...]

    sections = [
        (q_scr, lambda z, cols: _silu(z)),
        (lg_scr, lambda z, cols: jnp.log(lb[:, cols] + (1.0 - lb[:, cols]) * jax.nn.sigmoid(z)) * LOG2_E),
        (iv_scr, lambda z, cols: z),
        (sg_scr, lambda z, cols: _silu(z).astype(BF16)),
    ]
    n_pieces = W_C // MXU_WIDTH
    for s, (scr, post) in enumerate(sections):
        rows = pl.ds(s * chunk, chunk)
        stages = _hgrn_chunk_stages(q_scr.at[prev, rows], lg_scr.at[prev, rows], iv_scr.at[prev, rows],
                                    sg_scr.at[prev, rows], tri_ref, msk_ref, gn_ref, o_ref.at[rows],
                                    st_scr, chunk, levels)
        for p in range(n_pieces):
            cols = slice(p * MXU_WIDTH, (p + 1) * MXU_WIDTH)
            z = _dot(h_scr[...], w_ref[:, s * W_C + p * MXU_WIDTH:s * W_C + (p + 1) * MXU_WIDTH])
            scr[slot, :, cols] = post(z, cols)
            next(stages, None)
            next(stages, None)
        for _ in stages:
            pass

    @pl.when(jnp.logical_and(i > 0, i % tiles_per_batch == 0))
    def _():
        for h in range(H_C):
            s_ref[h] = st_scr[h].T


def _inproj_hgrn(x, norm_g, scale, shift, w, lb, gn_g, batch, seq):
    tm, chunk = ROW_TILE, HGRN_CHUNK
    m = x.shape[0]
    nt = m // tm
    tiles_per_batch = seq // tm
    tri, masks, levels = _hgrn_tables(chunk)
    cur = lambda i: jnp.minimum(i, nt - 1)
    done = lambda i: jnp.maximum(i - 1, 0)
    mod_spec = pl.BlockSpec((None, 1, D_MODEL), lambda i: (cur(i) // tiles_per_batch, 0, 0))
    const = lambda shape: pl.BlockSpec(shape, lambda i: (0,) * len(shape))
    o, s = pl.pallas_call(
        functools.partial(_inproj_hgrn_kernel, chunk=chunk, levels=levels, tiles_per_batch=tiles_per_batch),
        grid=(nt + 1,),
        in_specs=[
            pl.BlockSpec((tm, D_MODEL), lambda i: (cur(i), 0)),
            const((1, D_MODEL)),
            mod_spec, mod_spec,
            _weight_spec(w),
            const((1, W_C)),
            const(tri.shape), const(masks.shape), const((1, DV_C)),
        ],
        out_specs=[pl.BlockSpec((tm, W_C), lambda i: (done(i), 0)),
                   pl.BlockSpec((None, H_C, DK_C, DV_C), lambda i: (done(i) // tiles_per_batch, 0, 0, 0))],
        out_shape=[jax.ShapeDtypeStruct((m, W_C), BF16),
                   jax.ShapeDtypeStruct((batch, H_C, DK_C, DV_C), F32)],
        scratch_shapes=[pltpu.VMEM((tm, D_MODEL), BF16),
                        pltpu.VMEM((2, tm, W_C), F32),
                        pltpu.VMEM((2, tm, W_C), F32),
                        pltpu.VMEM((2, tm, W_C), F32),
                        pltpu.VMEM((2, tm, W_C), BF16),
                        pltpu.VMEM((H_C, DV_C, DK_C), F32)],
        compiler_params=_cparams("arbitrary"),
        name="inproj_hgrn_recurrence",
    )(x, norm_g.reshape(1, D_MODEL), scale, shift, w[0], lb.reshape(1, W_C), tri, masks, gn_g.reshape(1, DV_C))
    return o, s


_DEC_ROWS = 2 * H_B


def _decode_head_rows(pieces):
    row = lax.broadcasted_iota(jnp.int32, (_DEC_ROWS, LANES), 0)
    out = jnp.zeros((_DEC_ROWS, LANES), F32)
    for h in range(H_B):
        out = jnp.where(row // 2 == h, jnp.broadcast_to(pieces[h], (_DEC_ROWS, LANES)), out)
    return out


def _decode_scores(q, kn, k_pages):
    row = lax.broadcasted_iota(jnp.int32, (_DEC_ROWS, LANES), 0)
    lane = lax.broadcasted_iota(jnp.int32, (_DEC_ROWS, LANES), 1)
    qm32 = jnp.where(lane // HD_B == row % 2,
                     _decode_head_rows([q[:, h * LANES:(h + 1) * LANES] for h in range(H_B)]), 0.0)
    qm = qm32.astype(BF16)
    s = jnp.concatenate([_dot_nt(qm, kp[...].astype(BF16)) for kp in k_pages], axis=1)
    key_head = lax.broadcasted_iota(jnp.int32, s.shape, 1) % H_B
    s = jnp.where(key_head == lax.broadcasted_iota(jnp.int32, s.shape, 0) // 2, s, NEG_INF)
    s_new = jnp.sum(qm32 * _decode_head_rows([kn[h:h + 1, :] for h in range(H_B)]), axis=-1, keepdims=True)
    return s, s_new


def _decode_finish(s, s_new, lam, vn, v_pages, subln_row, gb, lam_init):
    page_rows = H_B * PAGE_SIZE
    m = jnp.maximum(jnp.max(s, axis=-1, keepdims=True), s_new)
    p = jnp.exp2(s - m)
    p_new = jnp.exp2(s_new - m)
    l = jnp.sum(p, axis=-1, keepdims=True) + p_new
    comp = lax.broadcasted_iota(jnp.int32, (_DEC_ROWS, 1), 0) % 2
    wgt = jnp.where(comp == 0, 1.0, -lam) / l
    pw = (p * wgt).astype(BF16)
    o8 = (p_new * wgt) * _decode_head_rows([vn[h:h + 1, :] for h in range(H_B)])
    for j, vp in enumerate(v_pages):
        o8 = o8 + _dot(pw[:, j * page_rows:(j + 1) * page_rows], vp[...].astype(BF16))
    out = []
    for h in range(H_B):
        o = o8[2 * h:2 * h + 1, :] + o8[2 * h + 1:2 * h + 2, :]
        out.append(_rms(o, subln_row) * (1.0 - lam_init) * gb[:, h * DV_B:(h + 1) * DV_B])
    return jnp.concatenate(out, axis=1)


def _decode_conv_kernel(st_ref, u_ref, ga_ref, w_ref, cb_ref, lg_ref, lb_ref, *rest):
    y_ref, ns_ref = rest[-2:]
    u = u_ref[...]
    y = u * w_ref[CONV_W - 1:CONV_W, :] + cb_ref[...]
    for j in range(CONV_W - 1):
        y = y + st_ref[j] * w_ref[j:j + 1, :]
        if j > 0:
            ns_ref[j - 1] = st_ref[j]
    ns_ref[CONV_W - 2] = u
    y_ref[...] = _conv_post(y, lg_ref[...], lb_ref[...], ga_ref[...].astype(F32)).astype(BF16)


def _decode_conv(state_t, layer, u, ga, conv_w, conv_b, ln_g, ln_b, new_prev):
    n = u.shape[0]
    tile = 32
    vec = pl.BlockSpec((1, W_A), lambda i: (0, 0))
    sspec = pl.BlockSpec((None, CONV_W - 1, tile, W_A), lambda i: (layer, 0, i, 0))
    rspec = pl.BlockSpec((tile, W_A), lambda i: (i, 0))
    in_specs = [sspec, rspec, rspec, pl.BlockSpec((CONV_W, W_A), lambda i: (0, 0)), vec, vec, vec]
    args = [state_t, u, ga, conv_w, conv_b.reshape(1, W_A), ln_g.reshape(1, W_A), ln_b.reshape(1, W_A)]
    aliases = {}
    if new_prev is not None:
        aliases = {len(args): 1}
        in_specs.append(pl.BlockSpec(memory_space=pl.ANY))
        args.append(new_prev)
    return pl.pallas_call(
        _decode_conv_kernel,
        grid=(n // tile,),
        in_specs=in_specs,
        out_specs=[rspec, sspec],
        out_shape=[jax.ShapeDtypeStruct((n, W_A), BF16),
                   jax.ShapeDtypeStruct((N_AC, CONV_W - 1, n, W_A), F32)],
        input_output_aliases=aliases,
        compiler_params=_cparams("arbitrary"),
        name="decode_conv",
    )(*args)


def _decode_spread_table():
    rows = np.arange(3 * 3 * H_C)
    cols = np.arange(3 * W_C)
    same_vec = (rows[:, None] // (3 * H_C)) == (cols[None, :] // W_C)
    same_head = (rows[:, None] % H_C) == ((cols[None, :] % W_C) // DV_C)
    return jnp.asarray(same_vec & same_head, BF16)


def _decode_hgrn_kernel(q_ref, lg_ref, i_ref, sg_ref, gn_ref, e_ref, s_ref, *rest, n_seq):
    o_ref, ns_ref = rest[-2:]

    def split_rows(x):
        return [p.astype(F32) for p in _split3(x)]

    def one(n):
        f8 = jnp.exp2(lg_ref[n])
        parts = jnp.concatenate(split_rows(f8) + split_rows(1.0 - f8) + split_rows(q_ref[n]), axis=0)
        spread = _dot(parts.T.astype(BF16), e_ref[...])
        v8 = i_ref[n]
        rows = []
        for h in range(H_C):
            f_b, k_b, q_b =(spread[:, j * W_C + h * DV_C:j * W_C + (h + 1) * DV_C] for j in range(3))
            s_new = f_b * s_ref[n, h] + k_b * v8[h:h + 1, :]
            ns_ref[n, h] = s_new
            rows.append(jnp.sum(q_b * s_new, axis=0, keepdims=True))
        o = jnp.concatenate(rows, axis=0)
        o_ref[n] = (_rms(o, gn_ref[...]) * sg_ref[n].astype(F32)).astype(BF16)

    per_trip = 4

    def body(group, carry):
        for r in range(per_trip):
            one(per_trip * group + r)
        return carry

    assert n_seq % per_trip == 0
    lax.fori_loop(0, n_seq // per_trip, body, 0)


def _decode_hgrn(qs, lg, iv, sg, gn_g, state_all, layer, new_prev):
    n = qs.shape[0]
    tile = SAMPLE_STATE_TILE
    hspec = pl.BlockSpec((tile, H_C, DV_C), lambda i: (i, 0, 0))
    sspec = pl.BlockSpec((None, tile, H_C, DK_C, DV_C), lambda i: (layer, i, 0, 0, 0))
    h3 = lambda a: a.reshape(n, H_C, DV_C)
    table = _decode_spread_table()
    in_specs = [hspec, hspec, hspec, hspec, pl.BlockSpec((1, DV_C), lambda i: (0, 0)),
                pl.BlockSpec(table.shape, lambda i: (0, 0)), sspec]
    args = [h3(qs), h3(lg), h3(iv), h3(sg), gn_g.reshape(1, DV_C), table, state_all]
    aliases = {}
    if new_prev is not None:
        aliases = {len(args): 1}
        in_specs.append(pl.BlockSpec(memory_space=pl.ANY))
        args.append(new_prev)
    o, ns = pl.pallas_call(
        functools.partial(_decode_hgrn_kernel, n_seq=tile),
        grid=(n // tile,),
        in_specs=in_specs,
        out_specs=[hspec, sspec],
        out_shape=[jax.ShapeDtypeStruct((n, H_C, DV_C), BF16),
                   jax.ShapeDtypeStruct((N_C, n, H_C, DK_C, DV_C), F32)],
        input_output_aliases=aliases,
        compiler_params=_cparams("arbitrary"),
        name="decode_hgrn",
    )(*args)
    return o.reshape(n, W_C), ns


def _rope_tables(pos, rows):
    half = ROT_DIM // 2
    inv_freq = ROPE_THETA ** (-jnp.arange(half, dtype=F32) / half)
    ang = pos.astype(F32)[:, None] * inv_freq[None, :]
    cos, sin = jnp.cos(ang), jnp.sin(ang)
    n = pos.shape[0]
    ones = jnp.ones((n, HD_B - ROT_DIM), F32)
    zeros = jnp.zeros((n, HD_B - ROT_DIM), F32)
    z8 = jnp.zeros((n, half), F32)
    cos_t = jnp.concatenate([cos, cos, ones], axis=1)
    sa_t = jnp.concatenate([-sin, z8, zeros], axis=1)
    sb_t = jnp.concatenate([z8, sin, zeros], axis=1)
    out = []
    for t in (cos_t, sa_t, sb_t):
        t = jnp.concatenate([t, t], axis=1)
        out.append(jnp.broadcast_to(t, (rows, LANES)) if n == 1 else t)
    return out


def kernel(x_prompt, x_sample, c_prompt, c_sample, cache_k, cache_v, page_table, state_conv, state_hgrn,
           norm_g, ada_w, ada_b, w_in_ac, w_out_ac, conv_w, conv_b, ln_g, ln_b, lam_q1, lam_k1, lam_q2,
           lam_k2, subln_g, w_in_c, w_out_c, gn_g, lb_logits, final_g):
    bp, tp = x_prompt.shape[:2]
    bs, ts = x_sample.shape[:2]
    assert ts == 1
    n_pool = cache_k.shape[1]
    n_past = page_table.shape[1] * PAGE_SIZE
    mp = bp * tp

    lb_sm = jax.nn.softmax(lb_logits.astype(F32), axis=0)
    lb_all = jnp.cumsum(lb_sm, axis=0) - lb_sm[0]

    mod = _ada_modulation(jnp.concatenate([c_prompt, c_sample], axis=0), ada_w, ada_b)

    def mods(l):
        out = []
        for j in range(3):
            m = mod[l, :, j * D_MODEL:(j + 1) * D_MODEL]
            out.append((m[:bp].reshape(bp, 1, D_MODEL), m[bp:].reshape(1, bs, D_MODEL)))
        return out

    tabs_p = _rope_tables(jnp.arange(tp), tp)
    tabs_s = _rope_tables(jnp.full((1,), n_past), bs)
    ck4 = cache_k.reshape(N_AC, n_pool, PAGE_SIZE * H_B, 2 * HD_B)
    cv4 = cache_v.reshape(N_AC, n_pool, PAGE_SIZE * H_B, DV_B)

    w_in_ac_b, w_out_ac_b, w_in_c_b, w_out_c_b = (w.astype(BF16) for w in (w_in_ac, w_out_ac, w_in_c, w_out_c))

    tiles_p = tp // ROW_TILE
    hp = x_prompt.reshape(mp, D_MODEL)
    hs = x_sample.reshape(bs, D_MODEL)
    state_conv_t = jnp.swapaxes(state_conv, 1, 2)
    cp_l, sp_l = [], []
    kv_p = kv_s = ss = cs = None
    for l in range(DEPTH):
        (sh_p, sh_s), (sc_p, sc_s), (gt_p, gt_s) = mods(l)
        last = final_g if l == DEPTH - 1 else None
        if l % 2 == 0:
            a = l // 2
            lam_init = 0.8 - 0.6 * math.exp(-0.3 * l)
            lam = (jnp.exp(jnp.sum(lam_q1[a].astype(F32) * lam_k1[a].astype(F32)))
                   - jnp.exp(jnp.sum(lam_q2[a].astype(F32) * lam_k2[a].astype(F32))) + lam_init)
            lam = jnp.full((1, LANES), lam, F32)
            w_in, w_out = (w_in_ac_b, a), (w_out_ac_b, a)
            q, k, kb, v, vb, gb, u, ga = _inproj_ac(hp, norm_g[l], sc_p, sh_p, w_in, *tabs_p,
                                                    ROW_TILE, tiles_p, a, kv_p)
            kv_p = (k, v)
            qs, k, _, v, _, gbs, us, gas = _inproj_ac(hs, norm_g[l], sc_s, sh_s, w_in, *tabs_s, bs, 1, a, kv_s)
            kv_s = (k, v)
            ob, obs = _attention(q, kb, vb, gb, bp, tp, qs, k, v, gbs, ck4, cv4, a, page_table,
                                 lam, subln_g[a], lam_init)
            y, cst = _prompt_conv(u, ga, conv_w[a], conv_b[a], ln_g[a], ln_b[a], bp, tp)
            hp = _outproj([ob, y], w_out, hp, gt_p, OUT_ROW_TILE, tp // OUT_ROW_TILE, last)
            cp_l.append(cst)
            y, cs = _decode_conv(state_conv_t, a, us, gas, conv_w[a], conv_b[a], ln_g[a], ln_b[a], cs)
            hs = _outproj([obs, y], w_out, hs, gt_s, bs, 1, last)
        else:
            ci = l // 2
            w_in, w_out = (w_in_c_b, ci), (w_out_c_b, ci)
            o, st = _inproj_hgrn(hp, norm_g[l], sc_p, sh_p, w_in, lb_all[l], gn_g[ci], bp, tp)
            hp = _outproj([o], w_out, hp, gt_p, OUT_ROW_TILE, tp // OUT_ROW_TILE, last)
            sp_l.append(st)
            qs, lg, iv, sg = _inproj_c(hs, norm_g[l], sc_s, sh_s, w_in, lb_all[l], bs, 1)
            o, ss = _decode_hgrn(qs, lg, iv, sg, gn_g[ci], state_hgrn, ci, ss)
            hs = _outproj([o], w_out, hs, gt_s, bs, 1, last)

    y_prompt = hp.reshape(bp, tp, D_MODEL)
    y_sample = hs.reshape(bs, ts, D_MODEL)
    return (y_prompt, y_sample,
            kv_p[0].reshape(N_AC, bp, tp, H_B, 2 * HD_B), kv_p[1].reshape(N_AC, bp, tp, H_B, DV_B),
            jnp.stack(cp_l), jnp.stack(sp_l),
            kv_s[0].reshape(N_AC, bs, ts, H_B, 2 * HD_B), kv_s[1].reshape(N_AC, bs, ts, H_B, DV_B),
            jnp.swapaxes(cs, 1, 2), ss)
```
